```python
import jax
import jax.numpy as jnp
from jax import lax
import numpy as np

D_MODEL = 1024
BATCH = 2
SEQ = 16384
DEPTH = 2

GRID_W = 64
CTX_LEN = 256
RMS_EPS = 1e-6

MLA_HEADS = 4
MLA_Q_LORA = 256
MLA_KV_LORA = 128
MLA_NOPE = 128
MLA_ROPE = 64
MLA_V = 128
MLA_SCALE = (MLA_NOPE + MLA_ROPE) ** -0.5
ROPE_THETA = 10000.0
Q_BLOCK = 128

ML_HEADS = 4
ML_QK = 64
ML_V = 128
ML_CONV = 3
ML_CHUNK = 128

AB_SPLITS = (MLA_Q_LORA, MLA_KV_LORA, MLA_ROPE, ML_HEADS * ML_QK, ML_HEADS * ML_QK,
             ML_HEADS * ML_V, ML_HEADS * ML_V, 4 * ML_HEADS)
AB_IN = sum(AB_SPLITS)
AB_OUT = MLA_HEADS * MLA_V + ML_HEADS * ML_V

HG_EXPAND = 128
HG_HEADS = D_MODEL // HG_EXPAND
HG_V = D_MODEL // HG_HEADS
HG_F = HG_HEADS * HG_EXPAND
HG_CHUNK = 64
C_SPLITS = (HG_F, HG_F, HG_F, HG_HEADS * HG_V, HG_HEADS * HG_V)
C_IN = sum(C_SPLITS)

D_FF = 2816
N_EXPERTS = 8
TOP_K = 2
D_FF_EXPERT = 2816
MOE_BLOCK = 512

N_EVEN = (DEPTH + 1) // 2
N_ODD = DEPTH // 2

kernel_name = 'hybrid_mla_mlstm_hgrn2_moe_dit_block'


def rms_norm(x, gain):
    xf = x.astype(jnp.float32)
    y = xf * lax.rsqrt(jnp.mean(xf * xf, axis=-1, keepdims=True) + RMS_EPS)
    return (y * gain.astype(jnp.float32)).astype(x.dtype)


def modulate(x, shift, scale):
    return x * (1 + scale) + shift


def split_cols(a, sizes):
    return jnp.split(a, [int(s) for s in np.cumsum(sizes)[:-1]], axis=-1)


def to_heads(a, n_heads):
    b, t, _ = a.shape
    return a.reshape(b, t, n_heads, -1).transpose(0, 2, 1, 3)


def to_chunks(a, size):
    b, h, t = a.shape[:3]
    return jnp.moveaxis(a.reshape((b, h, t // size, size) + a.shape[3:]), 2, 0)


def from_chunks(a):
    a = jnp.moveaxis(a, 0, 2)
    return a.reshape(a.shape[:2] + (-1,) + a.shape[4:])


def axial_rope_tables(n_lat):
    rows = n_lat // GRID_W
    t = jnp.arange(rows * GRID_W)
    row = (t // GRID_W).astype(jnp.float32)
    col = (t % GRID_W).astype(jnp.float32)
    n_freq = MLA_ROPE // 4
    inv_freq = ROPE_THETA ** (-jnp.arange(n_freq, dtype=jnp.float32) / n_freq)
    ang = jnp.concatenate([row[:, None] * inv_freq, col[:, None] * inv_freq], axis=-1)
    ang = jnp.concatenate([ang, ang], axis=-1)
    return jnp.cos(ang), jnp.sin(ang)


def apply_rope(x, cos, sin):
    xf = x.astype(jnp.float32)
    half = xf.shape[-1] // 2
    rot = jnp.concatenate([-xf[..., half:], xf[..., :half]], axis=-1)
    return (xf * cos + rot * sin).astype(x.dtype)


def dwconv_centred(x, w, b):
    k, ch = w.shape
    y = lax.conv_general_dilated(x, w[:, None, :].astype(x.dtype), window_strides=(1,),
                                 padding=[(k // 2, k // 2)], dimension_numbers=('NWC', 'WIO', 'NWC'),
                                 feature_group_count=ch)
    return y + b.astype(x.dtype)


def attend(q, k, v):
    s = jnp.einsum('bqhd,bkhd->bhqk', q, k).astype(jnp.float32) * MLA_SCALE
    p = jax.nn.softmax(s, axis=-1).astype(v.dtype)
    return jnp.einsum('bhqk,bkhd->bqhd', p, v)


def blocked_attend(q, k, v):
    b, s, h, d = q.shape
    nb = s // Q_BLOCK
    qb = jnp.moveaxis(q.reshape(b, nb, Q_BLOCK, h, d), 1, 0)
    o = lax.map(lambda qq: attend(qq, k, v), qb)
    return jnp.moveaxis(o, 0, 1).reshape(b, s, -1)


def mla_qkv(cq, ckv, k_rope, q_gain, w_uq, kv_gain, w_ukv, rope):
    b, t, _ = cq.shape
    q = (rms_norm(cq, q_gain) @ w_uq).reshape(b, t, MLA_HEADS, MLA_NOPE + MLA_ROPE)
    kv = (rms_norm(ckv, kv_gain) @ w_ukv).reshape(b, t, MLA_HEADS, MLA_NOPE + MLA_V)
    q_nope, q_rope = q[..., :MLA_NOPE], q[..., MLA_NOPE:]
    k_nope, v = kv[..., :MLA_NOPE], kv[..., MLA_NOPE:]
    if rope is not None:
        cos, sin = rope
        q_rope = apply_rope(q_rope, cos[:, None, :], sin[:, None, :])
        k_rope = apply_rope(k_rope, cos, sin)
    q = jnp.concatenate([q_nope, q_rope], axis=-1)
    k = jnp.concatenate([k_nope, jnp.broadcast_to(k_rope[:, :, None, :], (b, t, MLA_HEADS, MLA_ROPE))], axis=-1)
    return q, k, v


def mlstm_chunk_scan(q, k, v, log_i, log_f, state):
    tril = jnp.tril(jnp.ones((ML_CHUNK, ML_CHUNK), bool))

    def step(carry, inp):
        c_st, n_st, m_st = carry
        qc, kc, vc, ic, fc = inp
        g = jnp.cumsum(fc, axis=-1)
        d = jnp.where(tril, g[..., :, None] - g[..., None, :] + ic[..., None, :], -jnp.inf)
        inter = g + m_st[..., None]
        m_t = jnp.maximum(jnp.max(d, axis=-1), inter)
        s = jnp.einsum('bhtd,bhsd->bhts', qc, kc) * jnp.exp(d - m_t[..., None])
        w_inter = jnp.exp(inter - m_t)
        num = jnp.einsum('bhts,bhse->bhte', s, vc) + w_inter[..., None] * jnp.einsum('bhtd,bhde->bhte', qc, c_st)
        den = jnp.sum(s, axis=-1) + w_inter * jnp.einsum('bhtd,bhd->bht', qc, n_st)
        h = num / jnp.maximum(jnp.abs(den), jnp.exp(-m_t))[..., None]
        g_end = g[..., -1]
        w_log = g_end[..., None] - g + ic
        m_new = jnp.maximum(g_end + m_st, jnp.max(w_log, axis=-1))
        decay = jnp.exp(g_end + m_st - m_new)
        w_in = jnp.exp(w_log - m_new[..., None])
        c_st = decay[..., None, None] * c_st + jnp.einsum('bhsd,bhse->bhde', kc * w_in[..., None], vc)
        n_st = decay[..., None] * n_st + jnp.einsum('bhs,bhsd->bhd', w_in, kc)
        return (c_st, n_st, m_new), h

    xs = tuple(to_chunks(a, ML_CHUNK) for a in (q, k, v, log_i, log_f))
    state, h = lax.scan(step, state, xs)
    return from_chunks(h), state


def hgrn2_chunk_scan(q, k, v, log_f, state):
    tril = jnp.tril(jnp.ones((HG_CHUNK, HG_CHUNK), bool))[:, :, None]

    def step(s_st, inp):
        qc, kc, vc, fc = inp
        g = jnp.cumsum(fc, axis=2)
        decay = jnp.exp(jnp.where(tril, g[:, :, :, None, :] - g[:, :, None, :, :], -jnp.inf))
        a = jnp.sum(qc[:, :, :, None, :] * kc[:, :, None, :, :] * decay, axis=-1)
        o = jnp.einsum('bhts,bhse->bhte', a, vc) + jnp.einsum('bhtd,bhde->bhte', qc * jnp.exp(g), s_st)
        g_end = g[:, :, -1]
        s_st = jnp.exp(g_end)[..., None] * s_st + jnp.einsum(
            'bhsd,bhse->bhde', kc * jnp.exp(g_end[:, :, None, :] - g), vc)
        return s_st, o

    xs = tuple(to_chunks(a, HG_CHUNK) for a in (q, k, v, log_f))
    state, o = lax.scan(step, state, xs)
    return from_chunks(o), state


def bidirectional_scan(scan_fn, init, ctx_fwd, lat_fwd, ctx_bwd, lat_bwd):
    flip = lambda arrs: tuple(jnp.flip(a, axis=2) for a in arrs)
    h_cf, s_f = scan_fn(*ctx_fwd, init)
    h_lf, _ = scan_fn(*lat_fwd, s_f)
    h_cb, s_b = scan_fn(*flip(ctx_bwd), init)
    h_lb, _ = scan_fn(*flip(lat_bwd), s_b)
    return h_cf + jnp.flip(h_cb, axis=2), h_lf + jnp.flip(h_lb, axis=2)


def swiglu(h, w_gu, w_dn):
    g, u = jnp.split(h @ w_gu, 2, axis=-1)
    return (jax.nn.silu(g) * u) @ w_dn


def moe_swiglu(h, w_router, w_gu, w_dn):
    b, t, d = h.shape
    n_tok = b * t
    xf = h.reshape(n_tok, d)
    logits = (xf @ w_router).astype(jnp.float32)
    top_val, top_idx = lax.top_k(logits, TOP_K)
    gates = jax.nn.softmax(top_val, axis=-1)
    flat_e = top_idx.reshape(-1)
    order = jnp.argsort(flat_e)
    e_sorted = flat_e[order]
    tok_sorted = (order // TOP_K).astype(jnp.int32)
    g_sorted = gates.reshape(-1)[order]
    counts = jnp.bincount(flat_e, length=N_EXPERTS)
    padded = (counts + MOE_BLOCK - 1) // MOE_BLOCK * MOE_BLOCK
    start = jnp.cumsum(counts) - counts
    ends = jnp.cumsum(padded)
    dest = (ends - padded)[e_sorted] + jnp.arange(n_tok * TOP_K) - start[e_sorted]
    n_blocks = -(-(n_tok * TOP_K) // MOE_BLOCK) + N_EXPERTS
    n_slots = n_blocks * MOE_BLOCK
    slot_tok = jnp.full((n_slots,), n_tok, jnp.int32).at[dest].set(tok_sorted)
    slot_gate = jnp.zeros((n_slots,), jnp.float32).at[dest].set(g_sorted)
    block_expert = jnp.minimum(jnp.searchsorted(ends, jnp.arange(n_blocks) * MOE_BLOCK, side='right'), N_EXPERTS - 1)
    x_pad = jnp.concatenate([xf, jnp.zeros((1, d), xf.dtype)], axis=0)
    xs = x_pad[slot_tok].reshape(n_blocks, MOE_BLOCK, d)

    def expert_block(args):
        xb, e = args
        return swiglu(xb, w_gu[e], w_dn[e])

    ys = lax.map(expert_block, (xs, block_expert)).reshape(n_slots, d)
    out = jnp.zeros((n_tok + 1, d), jnp.float32).at[slot_tok].add(ys.astype(jnp.float32) * slot_gate[:, None])
    return out[:n_tok].astype(h.dtype).reshape(b, t, d)


def mla_mlstm_mixer(h_lat, h_ctx, rope, w_in, q_gain, w_uq, kv_gain, w_ukv, conv_w, conv_b, gate_b,
                    head_gain, w_out, need_ctx):
    b = h_lat.shape[0]
    p_lat = split_cols(h_lat @ w_in, AB_SPLITS)
    p_ctx = split_cols(h_ctx @ w_in, AB_SPLITS)
    q_l, k_l, v_l = mla_qkv(*p_lat[:3], q_gain, w_uq, kv_gain, w_ukv, rope)
    q_c, k_c, v_c = mla_qkv(*p_ctx[:3], q_gain, w_uq, kv_gain, w_ukv, None)
    a_lat = blocked_attend(q_l, jnp.concatenate([k_c, k_l], axis=1), jnp.concatenate([v_c, v_l], axis=1))

    def prepare(mq, mk, mv, mg):
        bb, t, _ = mq.shape
        qk = jax.nn.silu(dwconv_centred(jnp.concatenate([mq, mk], axis=-1), conv_w, conv_b))
        q = to_heads(qk[..., :ML_HEADS * ML_QK], ML_HEADS).astype(jnp.float32)
        k = to_heads(qk[..., ML_HEADS * ML_QK:], ML_HEADS).astype(jnp.float32) * ML_QK ** -0.5
        v = to_heads(mv, ML_HEADS).astype(jnp.float32)
        g = (mg + gate_b).astype(jnp.float32).reshape(bb, t, 4, ML_HEADS).transpose(2, 0, 3, 1)
        fwd = (q, k, v, g[0], jax.nn.log_sigmoid(g[1]))
        bwd = (q, k, v, g[2], jax.nn.log_sigmoid(g[3]))
        return fwd, bwd

    c_f, c_b = prepare(p_ctx[3], p_ctx[4], p_ctx[5], p_ctx[7])
    l_f, l_b = prepare(p_lat[3], p_lat[4], p_lat[5], p_lat[7])
    init = (jnp.zeros((b, ML_HEADS, ML_QK, ML_V), jnp.float32),
            jnp.zeros((b, ML_HEADS, ML_QK), jnp.float32),
            jnp.zeros((b, ML_HEADS), jnp.float32))
    m_ctx, m_lat = bidirectional_scan(mlstm_chunk_scan, init, c_f, l_f, c_b, l_b)

    def readout(h, o_pre):
        bb, _, t, _ = h.shape
        h = rms_norm(h.transpose(0, 2, 1, 3), head_gain.reshape(ML_HEADS, ML_V)).reshape(bb, t, -1)
        return (h * jax.nn.sigmoid(o_pre.astype(jnp.float32))).astype(h_lat.dtype)

    y_lat = jnp.concatenate([a_lat, readout(m_lat, p_lat[6])], axis=-1) @ w_out
    y_ctx = None
    if need_ctx:
        a_ctx = attend(q_c, k_c, v_c).reshape(b, q_c.shape[1], -1)
        y_ctx = jnp.concatenate([a_ctx, readout(m_ctx, p_ctx[6])], axis=-1) @ w_out
    return y_lat, y_ctx


def hgrn2_mixer(h_lat, h_ctx, lower_bound, w_in, head_gain, w_out, need_ctx):
    b = h_lat.shape[0]
    lb = lower_bound.reshape(HG_HEADS, 1, HG_EXPAND)

    def prepare(h):
        hq, zf, zb, hi, hg = split_cols(h @ w_in, C_SPLITS)
        q = to_heads(jax.nn.silu(hq), HG_HEADS).astype(jnp.float32)
        v = to_heads(hi, HG_HEADS).astype(jnp.float32)

        def direction(z):
            z = to_heads(z, HG_HEADS).astype(jnp.float32)
            log_f = jnp.log(lb + (1.0 - lb) * jax.nn.sigmoid(z))
            k = (1.0 - lb) * jax.nn.sigmoid(-z)
            return (q, k, v, log_f)

        return direction(zf), direction(zb), hg

    c_f, c_b, g_ctx = prepare(h_ctx)
    l_f, l_b, g_lat = prepare(h_lat)
    init = jnp.zeros((b, HG_HEADS, HG_EXPAND, HG_V), jnp.float32)
    o_ctx, o_lat = bidirectional_scan(hgrn2_chunk_scan, init, c_f, l_f, c_b, l_b)

    def readout(o, g):
        bb, _, t, _ = o.shape
        o = rms_norm(o.transpose(0, 2, 1, 3), head_gain.reshape(HG_HEADS, HG_V))
        gate = jax.nn.silu(g.astype(jnp.float32)).reshape(bb, t, HG_HEADS, HG_V)
        return (o * gate).reshape(bb, t, -1).astype(h_lat.dtype) @ w_out

    y_ctx = readout(o_ctx, g_ctx) if need_ctx else None
    return readout(o_lat, g_lat), y_ctx


def setup_inputs(seed: int = 0) -> dict:
    key = jax.random.key(seed)
    keys = iter(jax.random.split(key, 40))
    d = D_MODEL

    def dense(shape, fan_in):
        return jax.random.normal(next(keys), shape, jnp.float32) * fan_in ** -0.5

    def gain(shape, s=0.05):
        return 1.0 + s * jax.random.normal(next(keys), shape, jnp.float32)

    def small(shape, s=0.02):
        return s * jax.random.normal(next(keys), shape, jnp.float32)

    f_bias = jnp.linspace(3.0, 6.0, ML_HEADS)
    gate_base = jnp.concatenate([jnp.zeros((ML_HEADS,)), f_bias, jnp.zeros((ML_HEADS,)), f_bias])
    return {
        'x': jax.random.normal(next(keys), (BATCH, SEQ, d), jnp.float32),
        'c': jax.random.normal(next(keys), (BATCH, d), jnp.float32),
        'ctx': jax.random.normal(next(keys), (BATCH, CTX_LEN, d), jnp.float32),
        'c_ctx': jax.random.normal(next(keys), (d,), jnp.float32),
        'w_mod': dense((DEPTH, d, 6 * d), d),
        'b_mod': small((DEPTH, 6 * d)),
        'norm_gains': gain((DEPTH, 4, d)),
        'w_in_ab': dense((N_EVEN, d, AB_IN), d),
        'mla_q_gain': gain((N_EVEN, MLA_Q_LORA)),
        'mla_w_uq': dense((N_EVEN, MLA_Q_LORA, MLA_HEADS * (MLA_NOPE + MLA_ROPE)), MLA_Q_LORA),
        'mla_kv_gain': gain((N_EVEN, MLA_KV_LORA)),
        'mla_w_ukv': dense((N_EVEN, MLA_KV_LORA, MLA_HEADS * (MLA_NOPE + MLA_V)), MLA_KV_LORA),
        'ml_conv_w': dense((N_EVEN, ML_CONV, 2 * ML_HEADS * ML_QK), ML_CONV),
        'ml_conv_b': small((N_EVEN, 2 * ML_HEADS * ML_QK)),
        'ml_gate_b': gate_base + small((N_EVEN, 4 * ML_HEADS), 0.1),
        'ml_head_gain': gain((N_EVEN, ML_HEADS * ML_V)),
        'w_out_ab': dense((N_EVEN, AB_OUT, d), AB_OUT),
        'ffn_w_gu': dense((N_EVEN, d, 2 * D_FF), d),
        'ffn_w_dn': dense((N_EVEN, D_FF, d), D_FF),
        'w_in_c': dense((N_ODD, d, C_IN), d),
        'hg_lb_logits': gain((DEPTH, HG_F), 0.3),
        'hg_head_gain': gain((N_ODD, HG_HEADS * HG_V)),
        'w_out_c': dense((N_ODD, HG_HEADS * HG_V, d), HG_HEADS * HG_V),
        'moe_router': dense((N_ODD, d, N_EXPERTS), d),
        'moe_w_gu': dense((N_ODD, N_EXPERTS, d, 2 * D_FF_EXPERT), d),
        'moe_w_dn': dense((N_ODD, N_EXPERTS, D_FF_EXPERT, d), D_FF_EXPERT),
    }


def reference(x, c, ctx, c_ctx, w_mod, b_mod, norm_gains, w_in_ab, mla_q_gain, mla_w_uq, mla_kv_gain,
              mla_w_ukv, ml_conv_w, ml_conv_b, ml_gate_b, ml_head_gain, w_out_ab, ffn_w_gu, ffn_w_dn,
              w_in_c, hg_lb_logits, hg_head_gain, w_out_c, moe_router, moe_w_gu, moe_w_dn):
    rope = axial_rope_tables(x.shape[1])
    lbs = jnp.cumsum(jax.nn.softmax(hg_lb_logits.astype(jnp.float32), axis=0), axis=0)
    lbs = lbs - lbs[0]
    for layer in range(DEPTH):
        j = layer // 2
        need_ctx = layer < DEPTH - 1
        gains = norm_gains[layer]
        mod = jax.nn.silu(c) @ w_mod[layer] + b_mod[layer]
        mod_c = jax.nn.silu(c_ctx) @ w_mod[layer] + b_mod[layer]
        sh1, sc1, g1, sh2, sc2, g2 = jnp.split(mod[:, None, :], 6, axis=-1)
        csh1, csc1, cg1, csh2, csc2, cg2 = jnp.split(mod_c, 6, axis=-1)

        h = modulate(rms_norm(x, gains[0]), sh1, sc1)
        hc = modulate(rms_norm(ctx, gains[0]), csh1, csc1)
        if layer % 2 == 0:
            y, yc = mla_mlstm_mixer(h, hc, rope, w_in_ab[j], mla_q_gain[j], mla_w_uq[j], mla_kv_gain[j],
                                    mla_w_ukv[j], ml_conv_w[j], ml_conv_b[j], ml_gate_b[j], ml_head_gain[j],
                                    w_out_ab[j], need_ctx)
        else:
            y, yc = hgrn2_mixer(h, hc, lbs[layer], w_in_c[j], hg_head_gain[j], w_out_c[j], need_ctx)
        x = x + g1 * rms_norm(y, gains[1])

        h = modulate(rms_norm(x, gains[2]), sh2, sc2)
        if layer % 2 == 0:
            y = swiglu(h, ffn_w_gu[j], ffn_w_dn[j])
        else:
            y = moe_swiglu(h, moe_router[j], moe_w_gu[j], moe_w_dn[j])
        x = x + g2 * rms_norm(y, gains[3])

        if need_ctx:
            ctx = ctx + cg1 * rms_norm(yc, gains[1])
            hc = modulate(rms_norm(ctx, gains[2]), csh2, csc2)
            if layer % 2 == 0:
                yc = swiglu(hc, ffn_w_gu[j], ffn_w_dn[j])
            else:
                yc = moe_swiglu(hc, moe_router[j], moe_w_gu[j], moe_w_dn[j])
            ctx = ctx + cg2 * rms_norm(yc, gains[3])
    return x
```

```python
import functools

import jax
import jax.numpy as jnp
import numpy as np
from jax import lax
from jax.experimental import pallas as pl
from jax.experimental.pallas import tpu as pltpu

F32 = jnp.float32
BF16 = jnp.bfloat16
HIGHEST = lax.Precision.HIGHEST

D = 1024
CTX = 256
EPS = 1e-6
HALF = 256
TM = 512
VMEM_LIMIT = 56 * 1024 * 1024

A_HEADS = 4
A_QLORA = 256
A_KVLORA = 128
A_NOPE = 128
A_ROPE = 64
A_V = 128
A_HD = 256
A_SCALE = (A_NOPE + A_ROPE) ** -0.5
ROPE_THETA = 10000.0
GRID_W = 64
A_TQ = 256
A_TK = 512

M_HEADS = 4
M_QK = 64
M_V = 128
M_CHUNK = 128
M_QKP = 128

G_HEADS = 8
G_DK = 128
G_DV = 128
G_CHUNK = 64
G_SUB = 8

D_FF = 2816
N_EXP = 8
MOE_BLOCK = 512
FF_CHUNK = 256

P0_CQ = 0
P0_CKV = 256
P0_KR = 384
P0_MV = 512
P0_MQ = 1024
P0_MK = 1536
P0_OG = 2048
P0_GATE = 2560
P0_KRR = 2688
P0_N = 2816


def _cparams(sem):
    return pltpu.CompilerParams(dimension_semantics=sem, vmem_limit_bytes=VMEM_LIMIT)


def _const_spec(shape):
    nd = len(shape)
    return pl.BlockSpec(shape, lambda *_: (0,) * nd, pipeline_mode=pl.Buffered(1))


def _rms(x, gain_row):
    ms = jnp.mean(x * x, axis=-1, keepdims=True)
    return x * lax.rsqrt(ms + EPS) * gain_row


def _mod_index(half_idx, per):
    return jnp.where(half_idx % per == 0, 2, half_idx // per)


def _silu(x):
    return x * _sigmoid(x)


def _sigmoid(x):
    return 1.0 / (1.0 + jnp.exp(-x))


def _norm_mod_rows(x, gain_row, tab_ref, k_shift, k_scale, idx):
    shift = tab_ref[k_shift, pl.ds(idx, 1), :]
    scale = tab_ref[k_scale, pl.ds(idx, 1), :]
    return _rms(x, gain_row) * (1.0 + scale) + shift


def _norm_mod_tile(x_ref, gain_row, tab_ref, k_shift, k_scale, tile, per, rows):
    parts = []
    for s in range(rows // HALF):
        idx = _mod_index(tile * (rows // HALF) + s, per)
        parts.append(_norm_mod_rows(x_ref[s * HALF:(s + 1) * HALF, :], gain_row, tab_ref, k_shift, k_scale, idx))
    return parts[0] if len(parts) == 1 else jnp.concatenate(parts, axis=0)


def _gated_residual_tile(x_ref, y, gain_row, tab_ref, k_gate, tile, per, rows, o_ref):
    yn = _rms(y, gain_row)
    for s in range(rows // HALF):
        idx = _mod_index(tile * (rows // HALF) + s, per)
        gate = tab_ref[k_gate, pl.ds(idx, 1), :]
        sl = slice(s * HALF, (s + 1) * HALF)
        o_ref[sl, :] = x_ref[sl, :] + gate * yn[sl, :]


def _mod_kernel(c_ref, w_ref, b_ref, o_ref):
    a = _silu(c_ref[...])
    o_ref[0, 0] = jnp.dot(a, w_ref[0], precision=HIGHEST, preferred_element_type=F32) + b_ref[0, 0]


def _mod_table(cc, w_mod, b_mod):
    depth = w_mod.shape[0]
    return pl.pallas_call(
        _mod_kernel,
        out_shape=jax.ShapeDtypeStruct((depth, 6, 8, D), F32),
        grid=(depth, 6),
        in_specs=[
            pl.BlockSpec((8, D), lambda l, k: (0, 0)),
            pl.BlockSpec((1, D, D), lambda l, k: (l, 0, k)),
            pl.BlockSpec((1, 1, 1, D), lambda l, k: (l, k, 0, 0)),
        ],
        out_specs=pl.BlockSpec((1, 1, 8, D), lambda l, k: (l, k, 0, 0)),
        compiler_params=_cparams(("arbitrary", "arbitrary")),
        name="mod_table",
    )(cc, w_mod, b_mod.reshape(depth, 6, 1, D))


def _inproj_kernel(x_ref, gains_ref, tab_ref, w_ref, o_ref, *, per, nchunk):
    i = pl.program_id(0)
    h = _norm_mod_tile(x_ref, gains_ref[0:1, :], tab_ref, 0, 1, i, per, TM).astype(BF16)
    n = w_ref.shape[1]
    for n0 in range(0, n, nchunk):
        o_ref[:, n0:n0 + nchunk] = jnp.dot(h, w_ref[:, n0:n0 + nchunk], preferred_element_type=F32)


def _inproj(xs, gains, tab, w, per, nchunk):
    rows = xs.shape[0]
    n = w.shape[1]
    return pl.pallas_call(
        functools.partial(_inproj_kernel, per=per, nchunk=nchunk),
        out_shape=jax.ShapeDtypeStruct((rows, n), F32),
        grid=(rows // TM,),
        in_specs=[
            pl.BlockSpec((TM, D), lambda i: (i, 0)),
            _const_spec((4, D)),
            _const_spec((6, 8, D)),
            _const_spec((D, n)),
        ],
        out_specs=pl.BlockSpec((TM, n), lambda i: (i, 0)),
        compiler_params=_cparams(("parallel",)),
        name="inproj",
    )(xs, gains, tab, w)


def _mla_prep_kernel(cq_ref, ckr_ref, krr_ref, cos_ref, sin_ref, qg_ref, kvg_ref, wqn_ref, wqr_ref, wqrr_ref,
                     wk_ref, wv_ref, q_ref, k_ref, v_ref):
    qn = _rms(cq_ref[...], qg_ref[...]).astype(BF16)
    ckr = ckr_ref[...]
    kvn = _rms(ckr[:, :A_KVLORA], kvg_ref[...]).astype(BF16)
    cos = cos_ref[...]
    sin = sin_ref[...]
    q_nope = jnp.dot(qn, wqn_ref[...], preferred_element_type=F32)
    q_r = jnp.dot(qn, wqr_ref[...], preferred_element_type=F32)
    q_rr = jnp.dot(qn, wqrr_ref[...], preferred_element_type=F32)
    k_nope = jnp.dot(kvn, wk_ref[...], preferred_element_type=F32)
    vv = jnp.dot(kvn, wv_ref[...], preferred_element_type=F32)
    k_rope = (ckr[:, A_KVLORA:] * cos + krr_ref[...] * sin).astype(BF16)
    for h in range(A_HEADS):
        sl = slice(h * 128, (h + 1) * 128)
        q_rope = q_r[:, sl] * cos + q_rr[:, sl] * sin
        q_ref[0, h, :, 0:128] = (q_nope[:, sl] * A_SCALE).astype(BF16)
        q_ref[0, h, :, 128:256] = (q_rope * A_SCALE).astype(BF16)
        k_ref[0, h, :, 0:128] = k_nope[:, sl].astype(BF16)
        k_ref[0, h, :, 128:256] = k_rope
        v_ref[0, h] = vv[:, sl].astype(BF16)


def _mla_prep(p0, cos, sin, q_gain, kv_gain, wqn, wqr, wqrr, wk, wv, batch, t_len):
    rows = p0.shape[0]
    per = t_len // HALF
    qk_shape = jax.ShapeDtypeStruct((batch, A_HEADS, t_len, A_HD), BF16)
    v_shape = jax.ShapeDtypeStruct((batch, A_HEADS, t_len, A_V), BF16)
    out_map = lambda i: (i // per, 0, i % per, 0)
    return pl.pallas_call(
        _mla_prep_kernel,
        out_shape=(qk_shape, qk_shape, v_shape),
        grid=(rows // HALF,),
        in_specs=[
            pl.BlockSpec((HALF, 256), lambda i: (i, P0_CQ // 256)),
            pl.BlockSpec((HALF, 256), lambda i: (i, P0_CKV // 256)),
            pl.BlockSpec((HALF, 128), lambda i: (i, P0_KRR // 128)),
            pl.BlockSpec((HALF, 128), lambda i: (i, 0)),
            pl.BlockSpec((HALF, 128), lambda i: (i, 0)),
            _const_spec((1, A_QLORA)),
            _const_spec((1, A_KVLORA)),
            _const_spec((A_QLORA, 512)),
            _const_spec((A_QLORA, 512)),
            _const_spec((A_QLORA, 512)),
            _const_spec((A_KVLORA, 512)),
            _const_spec((A_KVLORA, 512)),
        ],
        out_specs=(
            pl.BlockSpec((1, A_HEADS, HALF, A_HD), out_map),
            pl.BlockSpec((1, A_HEADS, HALF, A_HD), out_map),
            pl.BlockSpec((1, A_HEADS, HALF, A_V), out_map),
        ),
        compiler_params=_cparams(("parallel",)),
        name="mla_prep",
    )(p0, p0, p0, cos, sin, q_gain, kv_gain, wqn, wqr, wqrr, wk, wv)


def _attn_kernel(q_ref, k_ref, v_ref, o_ref, *, n_lat_chunks):
    qi = pl.program_id(2)
    q = q_ref[0, 0]

    def scores(k_blk):
        return lax.dot_general(q, k_blk, (((1,), (1,)), ((), ())), preferred_element_type=F32)

    s0 = scores(k_ref[0, 0, 0:CTX, :])
    m0 = jnp.max(s0, axis=-1, keepdims=True)
    p0 = jnp.exp(s0 - m0)
    l0 = jnp.sum(p0, axis=-1, keepdims=True)
    acc0 = jnp.dot(p0.astype(BF16), v_ref[0, 0, 0:CTX, :], preferred_element_type=F32)

    def body(c, carry):
        m, l, acc = carry
        start = pl.multiple_of(CTX + c * A_TK, 256)
        s = scores(k_ref[0, 0, pl.ds(start, A_TK), :])
        m_new = jnp.maximum(m, jnp.max(s, axis=-1, keepdims=True))
        alpha = jnp.exp(m - m_new)
        p = jnp.exp(s - m_new)
        l = alpha * l + jnp.sum(p, axis=-1, keepdims=True)
        acc = alpha * acc + jnp.dot(p.astype(BF16), v_ref[0, 0, pl.ds(start, A_TK), :], preferred_element_type=F32)
        return m_new, l, acc

    n_iter = jnp.where(qi > 0, n_lat_chunks, 0)
    m, l, acc = lax.fori_loop(0, n_iter, body, (m0, l0, acc0))
    o_ref[0] = acc / l


def _attention(q, k, v):
    batch, heads, t_len, _ = q.shape
    n_lat_chunks = (t_len - CTX) // A_TK
    return pl.pallas_call(
        functools.partial(_attn_kernel, n_lat_chunks=n_lat_chunks),
        out_shape=jax.ShapeDtypeStruct((batch, t_len, heads * A_V), F32),
        grid=(batch, heads, t_len // A_TQ),
        in_specs=[
            pl.BlockSpec((1, 1, A_TQ, A_HD), lambda b, h, i: (b, h, i, 0)),
            pl.BlockSpec((1, 1, t_len, A_HD), lambda b, h, i: (b, h, 0, 0)),
            pl.BlockSpec((1, 1, t_len, A_V), lambda b, h, i: (b, h, 0, 0)),
        ],
        out_specs=pl.BlockSpec((1, A_TQ, A_V), lambda b, h, i: (b, i, h)),
        compiler_params=_cparams(("parallel", "parallel", "arbitrary")),
        name="mla_attention",
    )(q, k, v)


def _conv_kernel(x_ref, prev_ref, next_ref, w_ref, b_ref, post_ref, o_ref, *, t_len):
    i = pl.program_id(0)
    x = x_ref[...]
    row = lax.broadcasted_iota(jnp.int32, (TM, 1), 0)
    pos = (i * TM + row) % t_len
    x_prev = jnp.where(row == 0, prev_ref[7:8, :], pltpu.roll(x, 1, axis=0))
    x_next = jnp.where(row == TM - 1, next_ref[0:1, :], pltpu.roll(x, TM - 1, axis=0))
    seq_start = (pos == 0) | (pos == CTX)
    seq_end = (pos == CTX - 1) | (pos == t_len - 1)
    x_prev = jnp.where(seq_start, 0.0, x_prev)
    x_next = jnp.where(seq_end, 0.0, x_next)
    y = w_ref[0:1, :] * x_prev + w_ref[1:2, :] * x + w_ref[2:3, :] * x_next + b_ref[...]
    o_ref[...] = _silu(y) * post_ref[...]


def _mlstm_conv(p0, conv_w, conv_b, post, t_len):
    rows = p0.shape[0]
    n8 = rows // 8
    cb = P0_MQ // 1024
    return pl.pallas_call(
        functools.partial(_conv_kernel, t_len=t_len),
        out_shape=jax.ShapeDtypeStruct((rows, 1024), F32),
        grid=(rows // TM,),
        in_specs=[
            pl.BlockSpec((TM, 1024), lambda i: (i, cb)),
            pl.BlockSpec((8, 1024), lambda i: (jnp.maximum(i * (TM // 8) - 1, 0), cb)),
            pl.BlockSpec((8, 1024), lambda i: (jnp.minimum((i + 1) * (TM // 8), n8 - 1), cb)),
            _const_spec((3, 1024)),
            _const_spec((1, 1024)),
            _const_spec((1, 1024)),
        ],
        out_specs=pl.BlockSpec((TM, 1024), lambda i: (i, 0)),
        compiler_params=_cparams(("parallel",)),
        name="mlstm_conv",
    )(p0, p0, p0, conv_w, conv_b, post)


def _log_sigmoid(x):
    return jnp.minimum(x, 0.0) - jnp.log(1.0 + jnp.exp(-jnp.abs(x)))


def _mlstm_kernel(qkf_ref, qkb_ref, vf_ref, vb_ref, gf_ref, gb_ref, gbias_ref, hf_ref, hb_ref, c_ref, m_ref):
    j = pl.program_id(1)
    L = M_CHUNK

    @pl.when(j == 0)
    def _():
        c_ref[...] = jnp.zeros_like(c_ref)
        m_ref[...] = jnp.zeros_like(m_ref)

    r_i = lax.broadcasted_iota(jnp.int32, (L, L), 0)
    c_i = lax.broadcasted_iota(jnp.int32, (L, L), 1)
    lane = lax.broadcasted_iota(jnp.int32, (L, 128), 1)
    ones_col = jnp.where(lane == 0, 1.0, 0.0).astype(BF16)

    for d, (qk_ref, v_ref, g_ref, h_ref) in enumerate(((qkf_ref, vf_ref, gf_ref, hf_ref),
                                                        (qkb_ref, vb_ref, gb_ref, hb_ref))):
        rev = d == 1
        keep = (c_i >= r_i) if rev else (c_i <= r_i)
        tri = jnp.where(keep, 1.0, 0.0).astype(F32)
        pre = g_ref[...] + gbias_ref[...]
        csum = jnp.dot(tri, _log_sigmoid(pre), precision=HIGHEST, preferred_element_type=F32)
        csum_t = csum.T
        pre_t = pre.T
        for h in range(M_HEADS):
            ci = 8 * d + h
            cf = 8 * d + 4 + h
            g_col = csum[:, cf:cf + 1]
            g_row = csum_t[cf:cf + 1, :]
            i_col = pre[:, ci:ci + 1]
            i_row = pre_t[ci:ci + 1, :]
            g_end = g_row[:, 0:1] if rev else g_row[:, L - 1:L]
            m_prev = m_ref[4 * d + h][:, 0:1]
            dmat = jnp.where(keep, g_col - g_row + i_row, -jnp.inf)
            inter = g_col + m_prev
            m_t = jnp.maximum(jnp.max(dmat, axis=-1, keepdims=True), inter)
            qh = qk_ref[:, h * 128:(h + 1) * 128].astype(BF16)
            kh32 = qk_ref[:, 512 + h * 128:512 + (h + 1) * 128]
            sc = lax.dot_general(qh, kh32.astype(BF16), (((1,), (1,)), ((), ())), preferred_element_type=F32)
            s = sc * jnp.exp(dmat - m_t)
            w_inter = jnp.exp(inter - m_t)
            v_ext = jnp.concatenate([v_ref[:, h * 128:(h + 1) * 128].astype(BF16), ones_col], axis=1)
            c_st = c_ref[4 * d + h]
            tot = (jnp.dot(s.astype(BF16), v_ext, preferred_element_type=F32)
                   + w_inter * jnp.dot(qh, c_st.astype(BF16), preferred_element_type=F32))
            den = tot[:, 128:129]
            h_ref[:, h * 128:(h + 1) * 128] = tot[:, 0:128] / jnp.maximum(jnp.abs(den), jnp.exp(-m_t))
            w_log = g_end - g_col + i_col
            m_new = jnp.maximum(g_end + m_prev, jnp.max(w_log, axis=0, keepdims=True))
            decay = jnp.exp(g_end + m_prev - m_new)
            kw = (kh32 * jnp.exp(w_log - m_new)).astype(BF16)
            upd = lax.dot_general(kw, v_ext, (((0,), (0,)), ((), ())), preferred_element_type=F32)
            c_ref[4 * d + h] = decay * c_st + upd
            m_ref[4 * d + h] = jnp.broadcast_to(m_new, (1, 128))


def _mlstm_scan(qk, p0, gate_bias, batch, t_len):
    rows = qk.shape[0]
    nch = t_len // M_CHUNK
    nctx = CTX // M_CHUNK
    fwd = lambda b, j: b * nch + j
    bwd = lambda b, j: b * nch + jnp.where(j < nctx, nctx - 1 - j, nch + nctx - 1 - j)
    out = jax.ShapeDtypeStruct((rows, M_HEADS * M_V), F32)
    return pl.pallas_call(
        _mlstm_kernel,
        out_shape=(out, out),
        grid=(batch, nch),
        in_specs=[
            pl.BlockSpec((M_CHUNK, 1024), lambda b, j: (fwd(b, j), 0)),
            pl.BlockSpec((M_CHUNK, 1024), lambda b, j: (bwd(b, j), 0)),
            pl.BlockSpec((M_CHUNK, 512), lambda b, j: (fwd(b, j), P0_MV // 512)),
            pl.BlockSpec((M_CHUNK, 512), lambda b, j: (bwd(b, j), P0_MV // 512)),
            pl.BlockSpec((M_CHUNK, 128), lambda b, j: (fwd(b, j), P0_GATE // 128)),
            pl.BlockSpec((M_CHUNK, 128), lambda b, j: (bwd(b, j), P0_GATE // 128)),
            _const_spec((1, 128)),
        ],
        out_specs=(
            pl.BlockSpec((M_CHUNK, 512), lambda b, j: (fwd(b, j), 0)),
            pl.BlockSpec((M_CHUNK, 512), lambda b, j: (bwd(b, j), 0)),
        ),
        scratch_shapes=[
            pltpu.VMEM((2 * M_HEADS, M_QKP, 256), F32),
            pltpu.VMEM((2 * M_HEADS, 1, 128), F32),
        ],
        compiler_params=_cparams(("parallel", "arbitrary")),
        name="mlstm_scan",
    )(qk, qk, p0, p0, p0, p0, gate_bias)


def _head_rms(x, gain_row, n_heads, width):
    parts = []
    for h in range(n_heads):
        sl = slice(h * width, (h + 1) * width)
        parts.append(_rms(x[:, sl], gain_row[:, sl]))
    return jnp.concatenate(parts, axis=1)


def _ab_out_kernel(x_ref, a_ref, hf_ref, hb_ref, og_ref, hg_ref, wa_ref, wm_ref, gains_ref, tab_ref, o_ref, *, per):
    i = pl.program_id(0)
    m = _head_rms(hf_ref[...] + hb_ref[...], hg_ref[...], M_HEADS, M_V) * _sigmoid(og_ref[...])
    y = (jnp.dot(a_ref[...].astype(BF16), wa_ref[...], preferred_element_type=F32)
         + jnp.dot(m.astype(BF16), wm_ref[...], preferred_element_type=F32))
    _gated_residual_tile(x_ref, y, gains_ref[1:2, :], tab_ref, 2, i, per, TM, o_ref)


def _ab_out(xs, a, hf, hb, p0, head_gain, wa, wm, gains, tab, per):
    rows = xs.shape[0]
    row_spec = lambda w, cb=0: pl.BlockSpec((TM, w), lambda i: (i, cb))
    return pl.pallas_call(
        functools.partial(_ab_out_kernel, per=per),
        out_shape=jax.ShapeDtypeStruct((rows, D), F32),
        grid=(rows // TM,),
        in_specs=[
            row_spec(D), row_spec(512), row_spec(512), row_spec(512), row_spec(512, P0_OG // 512),
            _const_spec((1, 512)), _const_spec((512, D)), _const_spec((512, D)),
            _const_spec((4, D)), _const_spec((6, 8, D)),
        ],
        out_specs=row_spec(D),
        compiler_params=_cparams(("parallel",)),
        name="ab_out",
    )(xs, a, hf, hb, p0, head_gain, wa, wm, gains, tab)


def _swiglu(h, wgu_ref, wdn_ref, lead):
    acc = None
    for c0 in range(0, D_FF, FF_CHUNK):
        g = jnp.dot(h, wgu_ref[lead + (slice(None), slice(c0, c0 + FF_CHUNK))], preferred_element_type=F32)
        u = jnp.dot(h, wgu_ref[lead + (slice(None), slice(D_FF + c0, D_FF + c0 + FF_CHUNK))],
                    preferred_element_type=F32)
        a = (_silu(g) * u).astype(BF16)
        part = jnp.dot(a, wdn_ref[lead + (slice(c0, c0 + FF_CHUNK), slice(None))], preferred_element_type=F32)
        acc = part if acc is None else acc + part
    return acc


def _ffn_kernel(x_ref, gains_ref, tab_ref, wgu_ref, wdn_ref, o_ref, *, per):
    i = pl.program_id(0)
    h = _norm_mod_tile(x_ref, gains_ref[2:3, :], tab_ref, 3, 4, i, per, TM).astype(BF16)
    y = _swiglu(h, wgu_ref, wdn_ref, ())
    _gated_residual_tile(x_ref, y, gains_ref[3:4, :], tab_ref, 5, i, per, TM, o_ref)


def _ffn(xs, gains, tab, wgu, wdn, per):
    rows = xs.shape[0]
    return pl.pallas_call(
        functools.partial(_ffn_kernel, per=per),
        out_shape=jax.ShapeDtypeStruct((rows, D), F32),
        grid=(rows // TM,),
        in_specs=[
            pl.BlockSpec((TM, D), lambda i: (i, 0)),
            _const_spec((4, D)), _const_spec((6, 8, D)),
            _const_spec((D, 2 * D_FF)), _const_spec((D_FF, D)),
        ],
        out_specs=pl.BlockSpec((TM, D), lambda i: (i, 0)),
        compiler_params=_cparams(("parallel",)),
        name="ffn",
    )(xs, gains, tab, wgu, wdn)


def _hgrn_stream(q, k, v, g, st, rev):
    c = G_SUB
    ns = G_CHUNK // c
    blk = lambda a, i: a[c * i:c * (i + 1)]
    row = lambda a, r: a[r:r + 1]
    bound = [row(g, c * j) if rev else row(g, c * j + c - 1) for j in range(ns)]
    khat = jnp.concatenate([blk(k, j) * jnp.exp(bound[j] - blk(g, j)) for j in range(ns)], axis=0)
    pairs = [(i, j) for i in range(ns) for j in range(ns) if (i < j if rev else i > j)]
    qst = jnp.concatenate([blk(q, i) * jnp.exp(blk(g, i) - bound[j]) for (i, j) in pairs], axis=0)
    cross = lax.dot_general(qst.astype(BF16), khat.astype(BF16), (((1,), (1,)), ((), ())),
                            preferred_element_type=F32)
    lane = lax.broadcasted_iota(jnp.int32, (c, G_CHUNK), 1)
    sub = lax.broadcasted_iota(jnp.int32, (c, 128), 0)
    prods = []
    for i in range(ns):
        gi, qi, ki = blk(g, i), blk(q, i), blk(k, i)
        for s in range(c):
            seen = (sub <= s) if rev else (sub >= s)
            prods.append(qi * row(ki, s) * jnp.exp(jnp.where(seen, gi - row(gi, s), -jnp.inf)))
    red = jnp.dot(jnp.concatenate(prods, axis=0).astype(BF16), jnp.ones((128, G_CHUNK), BF16),
                  preferred_element_type=F32)
    a_rows = []
    for i in range(ns):
        acc = jnp.zeros((c, G_CHUNK), F32)
        for p, (ii, j) in enumerate(pairs):
            if ii == i:
                acc = jnp.where(lane // c == j, cross[c * p:c * (p + 1)], acc)
        for s in range(c):
            r0 = (i * c + s) * c
            acc = jnp.where(lane == i * c + s, red[r0:r0 + c], acc)
        a_rows.append(acc)
    a = jnp.concatenate(a_rows, axis=0).astype(BF16)
    g_end = row(g, 0) if rev else row(g, G_CHUNK - 1)
    o = (jnp.dot(a, v.astype(BF16), preferred_element_type=F32)
         + lax.dot_general((q * jnp.exp(g)).astype(BF16), st.astype(BF16), (((1,), (1,)), ((), ())),
                           preferred_element_type=F32))
    kw = (k * jnp.exp(g_end - g)).astype(BF16)
    st_new = st * jnp.exp(g_end) + lax.dot_general(v.astype(BF16), kw, (((0,), (0,)), ((), ())),
                                                    preferred_element_type=F32)
    return o, st_new


def _hgrn_kernel(qf_ref, zf_ref, vf_ref, qb_ref, zb_ref, vb_ref, lb_ref, of_ref, ob_ref, st_ref):
    j = pl.program_id(1)
    L = G_CHUNK

    @pl.when(j == 0)
    def _():
        st_ref[...] = jnp.zeros_like(st_ref)

    r_i = lax.broadcasted_iota(jnp.int32, (L, L), 0)
    c_i = lax.broadcasted_iota(jnp.int32, (L, L), 1)
    lb = lb_ref[...]
    for d, (q_ref, z_ref, v_ref, o_ref) in enumerate(((qf_ref, zf_ref, vf_ref, of_ref),
                                                       (qb_ref, zb_ref, vb_ref, ob_ref))):
        rev = d == 1
        tri = jnp.where((c_i >= r_i) if rev else (c_i <= r_i), 1.0, 0.0).astype(F32)
        z = z_ref[...]
        e = jnp.exp(-jnp.abs(z))
        r = 1.0 / (1.0 + e)
        sig_pos = jnp.where(z >= 0, r, e * r)
        sig_neg = jnp.where(z >= 0, e * r, r)
        log_f = jnp.log(lb + (1.0 - lb) * sig_pos)
        k_all = (1.0 - lb) * sig_neg
        g_all = jnp.dot(tri, log_f, precision=HIGHEST, preferred_element_type=F32)
        q_all = _silu(q_ref[...])
        for h in range(G_HEADS):
            sl = slice(h * 128, (h + 1) * 128)
            o, st_new = _hgrn_stream(q_all[:, sl], k_all[:, sl], v_ref[:, sl], g_all[:, sl], st_ref[d * G_HEADS + h], rev)
            o_ref[:, sl] = o
            st_ref[d * G_HEADS + h] = st_new


def _hgrn_scan(p1, lb_row, batch, t_len):
    rows = p1.shape[0]
    nch = t_len // G_CHUNK
    nctx = CTX // G_CHUNK
    fwd = lambda b, j: b * nch + j
    bwd = lambda b, j: b * nch + jnp.where(j < nctx, nctx - 1 - j, nch + nctx - 1 - j)
    spec = lambda order, cb: pl.BlockSpec((G_CHUNK, 1024), lambda b, j: (order(b, j), cb))
    out = jax.ShapeDtypeStruct((rows, G_HEADS * G_DV), F32)
    return pl.pallas_call(
        _hgrn_kernel,
        out_shape=(out, out),
        grid=(batch, nch),
        in_specs=[spec(fwd, 0), spec(fwd, 1), spec(fwd, 3), spec(bwd, 0), spec(bwd, 2), spec(bwd, 3),
                  _const_spec((1, 1024))],
        out_specs=(spec(fwd, 0), spec(bwd, 0)),
        scratch_shapes=[pltpu.VMEM((2 * G_HEADS, G_DV, G_DK), F32)],
        compiler_params=_cparams(("parallel", "arbitrary")),
        name="hgrn_scan",
    )(p1, p1, p1, p1, p1, p1, lb_row)


def _c_out_kernel(x_ref, of_ref, ob_ref, hg_ref, gain_ref, w_ref, gains_ref, tab_ref, o_ref, *, per):
    i = pl.program_id(0)
    o = _head_rms(of_ref[...] + ob_ref[...], gain_ref[...], G_HEADS, G_DV) * _silu(hg_ref[...])
    y = jnp.dot(o.astype(BF16), w_ref[...], preferred_element_type=F32)
    _gated_residual_tile(x_ref, y, gains_ref[1:2, :], tab_ref, 2, i, per, TM, o_ref)


def _c_out(xs, of, ob, p1, head_gain, w, gains, tab, per):
    rows = xs.shape[0]
    row_spec = lambda cb=0: pl.BlockSpec((TM, D), lambda i: (i, cb))
    return pl.pallas_call(
        functools.partial(_c_out_kernel, per=per),
        out_shape=jax.ShapeDtypeStruct((rows, D), F32),
        grid=(rows // TM,),
        in_specs=[row_spec(), row_spec(), row_spec(), row_spec(4), _const_spec((1, D)), _const_spec((D, D)),
                  _const_spec((4, D)), _const_spec((6, 8, D))],
        out_specs=row_spec(),
        compiler_params=_cparams(("parallel",)),
        name="c_out",
    )(xs, of, ob, p1, head_gain, w, gains, tab)


def _router_kernel(x_ref, gains_ref, tab_ref, wr_ref, h_ref, r_ref, *, per):
    i = pl.program_id(0)
    h = _norm_mod_tile(x_ref, gains_ref[2:3, :], tab_ref, 3, 4, i, per, TM)
    h_ref[...] = h
    logits = lax.dot_general(wr_ref[...], h, (((1,), (1,)), ((), ())), precision=HIGHEST,
                             preferred_element_type=F32)
    eid = lax.broadcasted_iota(jnp.int32, logits.shape, 0).astype(F32)
    m1 = jnp.max(logits, axis=0, keepdims=True)
    i1 = jnp.min(jnp.where(logits == m1, eid, float(N_EXP)), axis=0, keepdims=True)
    rest = jnp.where(eid == i1, -jnp.inf, logits)
    m2 = jnp.max(rest, axis=0, keepdims=True)
    i2 = jnp.min(jnp.where(rest == m2, eid, float(N_EXP)), axis=0, keepdims=True)
    e2 = jnp.exp(m2 - m1)
    g1 = 1.0 / (1.0 + e2)
    g2 = e2 * g1
    rid = lax.broadcasted_iota(jnp.int32, (128, TM), 0)
    packed = jnp.where(rid == 0, i1, jnp.where(rid == 1, i2, jnp.where(rid == 2, g1, jnp.where(rid == 3, g2, 0.0))))
    r_ref[...] = packed.T


def _router(xs, gains, tab, wr_t, per):
    rows = xs.shape[0]
    return pl.pallas_call(
        functools.partial(_router_kernel, per=per),
        out_shape=(jax.ShapeDtypeStruct((rows, D), F32), jax.ShapeDtypeStruct((rows, 128), F32)),
        grid=(rows // TM,),
        in_specs=[pl.BlockSpec((TM, D), lambda i: (i, 0)), _const_spec((4, D)), _const_spec((6, 8, D)),
                  _const_spec((N_EXP, D))],
        out_specs=(pl.BlockSpec((TM, D), lambda i: (i, 0)), pl.BlockSpec((TM, 128), lambda i: (i, 0))),
        compiler_params=_cparams(("parallel",)),
        name="moe_router",
    )(xs, gains, tab, wr_t)


def _row_copy(src_hbm, src_row, dst_vmem, dst_row, sem):
    return pltpu.make_async_copy(src_hbm.at[pl.ds(src_row, 1)], dst_vmem.at[pl.ds(dst_row, 1)], sem)


def _expert_kernel(be_ref, nact_ref, tok_ref, h_hbm, wgu_ref, wdn_ref, o_ref, xbuf, sem):
    b = pl.program_id(0)

    @pl.when(b < nact_ref[0])
    def _():
        def issue(r, carry):
            _row_copy(h_hbm, tok_ref[0, 0, r], xbuf, r, sem).start()
            return carry

        lax.fori_loop(0, MOE_BLOCK, issue, 0)

        def wait(r, carry):
            _row_copy(h_hbm, 0, xbuf, r, sem).wait()
            return carry

        lax.fori_loop(0, MOE_BLOCK, wait, 0)
        o_ref[...] = _swiglu(xbuf[...].astype(BF16), wgu_ref, wdn_ref, (0,))

    @pl.when(b >= nact_ref[0])
    def _():
        o_ref[...] = jnp.zeros_like(o_ref)


def _experts(block_expert, n_active, slot_tok, h, wgu, wdn):
    n_blocks = block_expert.shape[0]
    grid_spec = pltpu.PrefetchScalarGridSpec(
        num_scalar_prefetch=2,
        grid=(n_blocks,),
        in_specs=[
            pl.BlockSpec((1, 1, MOE_BLOCK), lambda b, be, na: (b, 0, 0), memory_space=pltpu.SMEM),
            pl.BlockSpec(memory_space=pl.ANY),
            pl.BlockSpec((1, D, 2 * D_FF), lambda b, be, na: (be[b], 0, 0)),
            pl.BlockSpec((1, D_FF, D), lambda b, be, na: (be[b], 0, 0)),
        ],
        out_specs=pl.BlockSpec((MOE_BLOCK, D), lambda b, be, na: (b, 0)),
        scratch_shapes=[pltpu.VMEM((MOE_BLOCK, D), F32), pltpu.SemaphoreType.DMA(())],
    )
    return pl.pallas_call(
        _expert_kernel,
        out_shape=jax.ShapeDtypeStruct((n_blocks * MOE_BLOCK, D), F32),
        grid_spec=grid_spec,
        compiler_params=_cparams(("arbitrary",)),
        name="moe_experts",
    )(block_expert, n_active, slot_tok.reshape(n_blocks, 1, MOE_BLOCK), h, wgu, wdn)


def _combine_kernel(pos_ref, ys_hbm, r_ref, x_ref, gains_ref, tab_ref, o_ref, buf, sem, *, per, lat_only):
    if lat_only:
        tile = pl.program_id(0) * per + 1 + pl.program_id(1)
    else:
        tile = pl.program_id(0)

    def issue(r, carry):
        _row_copy(ys_hbm, pos_ref[0, 0, r], buf, r, sem).start()
        return carry

    lax.fori_loop(0, 2 * HALF, issue, 0)

    def wait(r, carry):
        _row_copy(ys_hbm, 0, buf, r, sem).wait()
        return carry

    lax.fori_loop(0, 2 * HALF, wait, 0)
    gates = r_ref[...]
    y = gates[:, 2:3] * buf[0:HALF, :] + gates[:, 3:4] * buf[HALF:2 * HALF, :]
    idx = _mod_index(tile, per)
    gate = tab_ref[5, pl.ds(idx, 1), :]
    o_ref[...] = x_ref[...] + gate * _rms(y, gains_ref[3:4, :])


def _combine_latents(pos, ys, rout, xs, gains, tab, batch, s_len, per):
    nl = s_len // HALF
    tile = lambda b, j: b * per + 1 + j
    return pl.pallas_call(
        functools.partial(_combine_kernel, per=per, lat_only=True),
        out_shape=jax.ShapeDtypeStruct((batch * s_len, D), F32),
        grid=(batch, nl),
        in_specs=[
            pl.BlockSpec((1, 1, 2 * HALF), lambda b, j: (tile(b, j), 0, 0), memory_space=pltpu.SMEM),
            pl.BlockSpec(memory_space=pl.ANY),
            pl.BlockSpec((HALF, 128), lambda b, j: (tile(b, j), 0)),
            pl.BlockSpec((HALF, D), lambda b, j: (tile(b, j), 0)),
            _const_spec((4, D)), _const_spec((6, 8, D)),
        ],
        out_specs=pl.BlockSpec((HALF, D), lambda b, j: (b * nl + j, 0)),
        scratch_shapes=[pltpu.VMEM((2 * HALF, D), F32), pltpu.SemaphoreType.DMA(())],
        compiler_params=_cparams(("arbitrary", "arbitrary")),
        name="moe_combine",
    )(pos, ys, rout, xs, gains, tab)


def _moe_plan(rout):
    n_tok = rout.shape[0]
    n_flat = 2 * n_tok
    flat_e = rout[:, 0:2].astype(jnp.int32).reshape(-1)
    onehot = (flat_e[:, None] == jnp.arange(N_EXP, dtype=jnp.int32)[None, :]).astype(jnp.int32)
    csum = jnp.cumsum(onehot, axis=0)
    counts = csum[-1]
    rank = jnp.sum(csum * onehot, axis=1) - 1
    padded = (counts + MOE_BLOCK - 1) // MOE_BLOCK * MOE_BLOCK
    pend = jnp.cumsum(padded)
    pstart = pend - padded
    start = jnp.cumsum(counts) - counts
    dest = pstart[flat_e] + rank
    n_blocks = -(-n_flat // MOE_BLOCK) + N_EXP
    block_expert = jnp.minimum(
        jnp.searchsorted(pend, jnp.arange(n_blocks, dtype=jnp.int32) * MOE_BLOCK, side='right'), N_EXP - 1
    ).astype(jnp.int32)
    n_active = (pend[-1] // MOE_BLOCK).astype(jnp.int32).reshape(1)
    order = jnp.argsort(flat_e, stable=True).astype(jnp.int32)
    tok_sorted = order // 2
    slot = jnp.arange(n_blocks * MOE_BLOCK, dtype=jnp.int32)
    e_slot = block_expert[slot // MOE_BLOCK]
    off = slot - pstart[e_slot]
    valid = (off >= 0) & (off < counts[e_slot])
    src = jnp.clip(start[e_slot] + off, 0, n_flat - 1)
    slot_tok = jnp.where(valid, tok_sorted[src], 0).astype(jnp.int32)
    dest2 = dest.reshape(n_tok, 2).astype(jnp.int32)
    pos = jnp.concatenate([dest2[:, 0].reshape(-1, 1, HALF), dest2[:, 1].reshape(-1, 1, HALF)], axis=2)
    return block_expert, n_active, slot_tok, pos


def _rot_half_cols(w):
    half = A_ROPE // 2
    g = w.reshape(w.shape[0], -1, A_ROPE)
    return jnp.concatenate([-g[..., half:], g[..., :half]], axis=-1).reshape(w.shape)


def _pad_groups(w, width, to):
    k = w.shape[0]
    g = w.reshape(k, -1, width)
    return jnp.pad(g, ((0, 0), (0, 0), (0, to - width))).reshape(k, -1)


def _layer0_inproj_weight(w_in):
    splits = np.cumsum([A_QLORA, A_KVLORA, A_ROPE, M_HEADS * M_QK, M_HEADS * M_QK, M_HEADS * M_V, M_HEADS * M_V])
    cq, ckv, kr, mq, mk, mv, og, gt = jnp.split(w_in, [int(s) for s in splits], axis=1)
    pad64 = lambda a: jnp.pad(a, ((0, 0), (0, 64)))
    cols = [cq, ckv, pad64(kr), mv, _pad_groups(mq, M_QK, M_QKP), _pad_groups(mk, M_QK, M_QKP), og,
            jnp.pad(gt, ((0, 0), (0, 128 - gt.shape[1]))), pad64(_rot_half_cols(kr))]
    w = jnp.concatenate(cols, axis=1)
    assert w.shape[1] == P0_N
    return w.astype(BF16)


def _rope_tables(batch, s_len):
    t = jnp.arange(s_len)
    row = (t // GRID_W).astype(F32)
    col = (t % GRID_W).astype(F32)
    n_freq = A_ROPE // 4
    inv_freq = ROPE_THETA ** (-jnp.arange(n_freq, dtype=F32) / n_freq)
    ang = jnp.concatenate([row[:, None] * inv_freq, col[:, None] * inv_freq], axis=-1)
    ang = jnp.concatenate([ang, ang], axis=-1)

    def table(vals, fill):
        lat = jnp.pad(vals, ((0, 0), (0, 128 - A_ROPE)), constant_values=fill)
        one = jnp.concatenate([jnp.full((CTX, 128), fill, F32), lat], axis=0)
        return jnp.tile(one, (batch, 1))

    return table(jnp.cos(ang), 1.0), table(jnp.sin(ang), 0.0)


def kernel(x, c, ctx, c_ctx, w_mod, b_mod, norm_gains, w_in_ab, mla_q_gain, mla_w_uq, mla_kv_gain, mla_w_ukv,
           ml_conv_w, ml_conv_b, ml_gate_b, ml_head_gain, w_out_ab, ffn_w_gu, ffn_w_dn, w_in_c, hg_lb_logits,
           hg_head_gain, w_out_c, moe_router, moe_w_gu, moe_w_dn):
    batch, s_len, d = x.shape
    assert d == D and ctx.shape[1] == CTX and batch == 2
    assert s_len % A_TK == 0 and (batch * (CTX + s_len)) % TM == 0
    t_len = CTX + s_len
    per = t_len // HALF

    xs = jnp.concatenate([ctx, x], axis=1).reshape(batch * t_len, D)
    cc = jnp.concatenate([c, c_ctx[None, :], jnp.zeros((8 - batch - 1, D), F32)], axis=0)
    tabs = _mod_table(cc, w_mod, b_mod)

    gains = norm_gains[0]
    tab = tabs[0]
    p0 = _inproj(xs, gains, tab, _layer0_inproj_weight(w_in_ab[0]), per, P0_N // 2)

    cos, sin = _rope_tables(batch, s_len)
    w_uq = mla_w_uq[0].reshape(A_QLORA, A_HEADS, A_NOPE + A_ROPE)
    wq_nope = w_uq[:, :, :A_NOPE].reshape(A_QLORA, A_HEADS * A_NOPE)
    wq_rope = w_uq[:, :, A_NOPE:].reshape(A_QLORA, A_HEADS * A_ROPE)
    w_ukv = mla_w_ukv[0].reshape(A_KVLORA, A_HEADS, A_NOPE + A_V)
    q, k, v = _mla_prep(
        p0, cos, sin, mla_q_gain[0][None, :], mla_kv_gain[0][None, :],
        wq_nope.astype(BF16),
        _pad_groups(wq_rope, A_ROPE, 128).astype(BF16),
        _pad_groups(_rot_half_cols(wq_rope), A_ROPE, 128).astype(BF16),
        w_ukv[:, :, :A_NOPE].reshape(A_KVLORA, -1).astype(BF16),
        w_ukv[:, :, A_NOPE:].reshape(A_KVLORA, -1).astype(BF16),
        batch, t_len)
    attn = _attention(q, k, v).reshape(batch * t_len, A_HEADS * A_V)

    conv_w = jnp.concatenate([_pad_groups(ml_conv_w[0][:, :M_HEADS * M_QK], M_QK, M_QKP),
                              _pad_groups(ml_conv_w[0][:, M_HEADS * M_QK:], M_QK, M_QKP)], axis=1)
    conv_b = jnp.concatenate([_pad_groups(ml_conv_b[0][None, :M_HEADS * M_QK], M_QK, M_QKP),
                              _pad_groups(ml_conv_b[0][None, M_HEADS * M_QK:], M_QK, M_QKP)], axis=1)
    post = jnp.concatenate([jnp.ones((1, 512), F32), jnp.full((1, 512), M_QK ** -0.5, F32)], axis=1)
    qk = _mlstm_conv(p0, conv_w, conv_b, post, t_len)
    gate_bias = jnp.pad(ml_gate_b[0][None, :], ((0, 0), (0, 128 - 4 * M_HEADS)))
    hf, hb = _mlstm_scan(qk, p0, gate_bias, batch, t_len)

    w_out = w_out_ab[0].astype(BF16)
    xs = _ab_out(xs, attn, hf, hb, p0, ml_head_gain[0][None, :], w_out[:A_HEADS * A_V], w_out[A_HEADS * A_V:],
                 gains, tab, per)
    xs = _ffn(xs, gains, tab, ffn_w_gu[0].astype(BF16), ffn_w_dn[0].astype(BF16), per)

    gains = norm_gains[1]
    tab = tabs[1]
    lbs = jnp.cumsum(jax.nn.softmax(hg_lb_logits.astype(F32), axis=0), axis=0)
    lb_row = (lbs - lbs[0])[1][None, :]
    p1 = _inproj(xs, gains, tab, w_in_c[0].astype(BF16), per, 1024)
    of, ob = _hgrn_scan(p1, lb_row, batch, t_len)
    xs = _c_out(xs, of, ob, p1, hg_head_gain[0][None, :], w_out_c[0].astype(BF16), gains, tab, per)

    h, rout = _router(xs, gains, tab, moe_router[0].T, per)
    block_expert, n_active, slot_tok, pos = _moe_plan(rout)
    ys = _experts(block_expert, n_active, slot_tok, h, moe_w_gu[0].astype(BF16), moe_w_dn[0].astype(BF16))
    out = _combine_latents(pos, ys, rout, xs, gains, tab, batch, s_len, per)
    return out.reshape(batch, s_len, D)
```

```python
import functools

import jax
import jax.numpy as jnp
import numpy as np
from jax import lax
from jax.experimental import pallas as pl
from jax.experimental.pallas import tpu as pltpu

F32 = jnp.float32
BF16 = jnp.bfloat16
HIGHEST = lax.Precision.HIGHEST

D = 1024
CTX = 256
EPS = 1e-6
HALF = 256
TM = 512
VMEM_LIMIT = 56 * 1024 * 1024

A_HEADS = 4
A_QLORA = 256
A_KVLORA = 128
A_NOPE = 128
A_ROPE = 64
A_V = 128
A_HD = 256
A_SCALE = (A_NOPE + A_ROPE) ** -0.5
ROPE_THETA = 10000.0
GRID_W = 64
A_QSCALE = A_SCALE * 1.4426950408889634
A_TQ = 256
A_TK = 4096
A_RS = 32

M_HEADS = 4
M_QK = 64
M_V = 128
M_CHUNK = 128
M_QKP = 128

G_HEADS = 8
G_DK = 128
G_DV = 128
G_CHUNK = 64
G_SUB = 8

D_FF = 2816
N_EXP = 8
MOE_BLOCK = 512
FF_CHUNK = 256

P0_CQ = 0
P0_CKV = 256
P0_KR = 384
P0_MV = 512
P0_MQ = 1024
P0_MK = 1536
P0_OG = 2048
P0_GATE = 2560
P0_KRR = 2688
P0_N = 2816


def _cparams(sem):
    return pltpu.CompilerParams(dimension_semantics=sem, vmem_limit_bytes=VMEM_LIMIT)


def _const_spec(shape):
    nd = len(shape)
    return pl.BlockSpec(shape, lambda *_: (0,) * nd, pipeline_mode=pl.Buffered(1))


def _rms(x, gain_row):
    ms = jnp.mean(x * x, axis=-1, keepdims=True)
    return x * lax.rsqrt(ms + EPS) * gain_row


def _mod_index(half_idx, per):
    return jnp.where(half_idx % per == 0, 2, half_idx // per)


def _silu(x):
    return x * _sigmoid(x)


def _sigmoid(x):
    return 1.0 / (1.0 + jnp.exp(-x))


def _norm_mod_rows(x, gain_row, tab_ref, k_shift, k_scale, idx):
    shift = tab_ref[k_shift, pl.ds(idx, 1), :]
    scale = tab_ref[k_scale, pl.ds(idx, 1), :]
    return _rms(x, gain_row) * (1.0 + scale) + shift


def _norm_mod_tile(x_ref, gain_row, tab_ref, k_shift, k_scale, tile, per, rows):
    parts = []
    for s in range(rows // HALF):
        idx = _mod_index(tile * (rows // HALF) + s, per)
        parts.append(_norm_mod_rows(x_ref[s * HALF:(s + 1) * HALF, :], gain_row, tab_ref, k_shift, k_scale, idx))
    return parts[0] if len(parts) == 1 else jnp.concatenate(parts, axis=0)


def _gated_residual_tile(x_ref, y, gain_row, tab_ref, k_gate, tile, per, rows, o_ref):
    yn = _rms(y, gain_row)
    for s in range(rows // HALF):
        idx = _mod_index(tile * (rows // HALF) + s, per)
        gate = tab_ref[k_gate, pl.ds(idx, 1), :]
        sl = slice(s * HALF, (s + 1) * HALF)
        o_ref[sl, :] = x_ref[sl, :] + gate * yn[sl, :]


def _mod_kernel(c_ref, w_ref, b_ref, o_ref):
    a = _silu(c_ref[...])
    o_ref[0, 0] = jnp.dot(a, w_ref[0], precision=HIGHEST, preferred_element_type=F32) + b_ref[0, 0]


def _mod_table(cc, w_mod, b_mod):
    depth = w_mod.shape[0]
    return pl.pallas_call(
        _mod_kernel,
        out_shape=jax.ShapeDtypeStruct((depth, 6, 8, D), F32),
        grid=(depth, 6),
        in_specs=[
            pl.BlockSpec((8, D), lambda l, k: (0, 0)),
            pl.BlockSpec((1, D, D), lambda l, k: (l, 0, k)),
            pl.BlockSpec((1, 1, 1, D), lambda l, k: (l, k, 0, 0)),
        ],
        out_specs=pl.BlockSpec((1, 1, 8, D), lambda l, k: (l, k, 0, 0)),
        compiler_params=_cparams(("arbitrary", "arbitrary")),
        name="mod_table",
    )(cc, w_mod, b_mod.reshape(depth, 6, 1, D))


def _inproj_kernel(x_ref, gains_ref, tab_ref, w_ref, o_ref, *, per, nchunk):
    i = pl.program_id(0)
    h = _norm_mod_tile(x_ref, gains_ref[0:1, :], tab_ref, 0, 1, i, per, TM).astype(BF16)
    n = w_ref.shape[1]
    for n0 in range(0, n, nchunk):
        o_ref[:, n0:n0 + nchunk] = jnp.dot(h, w_ref[:, n0:n0 + nchunk], preferred_element_type=F32)


def _inproj(xs, gains, tab, w, per, nchunk):
    rows = xs.shape[0]
    n = w.shape[1]
    return pl.pallas_call(
        functools.partial(_inproj_kernel, per=per, nchunk=nchunk),
        out_shape=jax.ShapeDtypeStruct((rows, n), F32),
        grid=(rows // TM,),
        in_specs=[
            pl.BlockSpec((TM, D), lambda i: (i, 0)),
            _const_spec((4, D)),
            _const_spec((6, 8, D)),
            _const_spec((D, n)),
        ],
        out_specs=pl.BlockSpec((TM, n), lambda i: (i, 0)),
        compiler_params=_cparams(("parallel",)),
        name="inproj",
    )(xs, gains, tab, w)


def _mla_prep_kernel(cq_ref, ckr_ref, krr_ref, cos_ref, sin_ref, qg_ref, kvg_ref, wqn_ref, wqr_ref, wqrr_ref,
                     wk_ref, wv_ref, q_ref, k_ref, v_ref):
    qn = _rms(cq_ref[...], qg_ref[...]).astype(BF16)
    ckr = ckr_ref[...]
    kvn = _rms(ckr[:, :A_KVLORA], kvg_ref[...]).astype(BF16)
    cos = cos_ref[...]
    sin = sin_ref[...]
    q_nope = jnp.dot(qn, wqn_ref[...], preferred_element_type=F32)
    q_r = jnp.dot(qn, wqr_ref[...], preferred_element_type=F32)
    q_rr = jnp.dot(qn, wqrr_ref[...], preferred_element_type=F32)
    k_nope = jnp.dot(kvn, wk_ref[...], preferred_element_type=F32)
    vv = jnp.dot(kvn, wv_ref[...], preferred_element_type=F32)
    k_rope = (ckr[:, A_KVLORA:] * cos + krr_ref[...] * sin).astype(BF16)
    lane = lax.broadcasted_iota(jnp.int32, (HALF, A_V), 1)
    ones_col = jnp.where(lane == 0, 1.0, 0.0).astype(BF16)
    for h in range(A_HEADS):
        sl = slice(h * 128, (h + 1) * 128)
        q_rope = q_r[:, sl] * cos + q_rr[:, sl] * sin
        q_ref[0, h, :, 0:128] = (q_nope[:, sl] * A_QSCALE).astype(BF16)
        q_ref[0, h, :, 128:256] = (q_rope * A_QSCALE).astype(BF16)
        k_ref[0, h, :, 0:128] = k_nope[:, sl].astype(BF16)
        k_ref[0, h, :, 128:256] = k_rope
        v_ref[0, h, :, 0:A_V] = vv[:, sl].astype(BF16)
        v_ref[0, h, :, A_V:2 * A_V] = ones_col


def _mla_prep(p0, cos, sin, q_gain, kv_gain, wqn, wqr, wqrr, wk, wv, batch, t_len):
    rows = p0.shape[0]
    per = t_len // HALF
    qk_shape = jax.ShapeDtypeStruct((batch, A_HEADS, t_len, A_HD), BF16)
    v_shape = jax.ShapeDtypeStruct((batch, A_HEADS, t_len, 2 * A_V), BF16)
    out_map = lambda i: (i // per, 0, i % per, 0)
    return pl.pallas_call(
        _mla_prep_kernel,
        out_shape=(qk_shape, qk_shape, v_shape),
        grid=(rows // HALF,),
        in_specs=[
            pl.BlockSpec((HALF, 256), lambda i: (i, P0_CQ // 256)),
            pl.BlockSpec((HALF, 256), lambda i: (i, P0_CKV // 256)),
            pl.BlockSpec((HALF, 128), lambda i: (i, P0_KRR // 128)),
            pl.BlockSpec((HALF, 128), lambda i: (i, 0)),
            pl.BlockSpec((HALF, 128), lambda i: (i, 0)),
            _const_spec((1, A_QLORA)),
            _const_spec((1, A_KVLORA)),
            _const_spec((A_QLORA, 512)),
            _const_spec((A_QLORA, 512)),
            _const_spec((A_QLORA, 512)),
            _const_spec((A_KVLORA, 512)),
            _const_spec((A_KVLORA, 512)),
        ],
        out_specs=(
            pl.BlockSpec((1, A_HEADS, HALF, A_HD), out_map),
            pl.BlockSpec((1, A_HEADS, HALF, A_HD), out_map),
            pl.BlockSpec((1, A_HEADS, HALF, 2 * A_V), out_map),
        ),
        compiler_params=_cparams(("parallel",)),
        name="mla_prep",
    )(p0, p0, p0, cos, sin, q_gain, kv_gain, wqn, wqr, wqrr, wk, wv)


def _attn_kernel(q_ref, k_ref, v_ref, o_ref, s_ref, p_ref, m_ref, acc_ref, *, n_lat_chunks, tk):
    qi = pl.program_id(2)
    q = q_ref[0, 0]

    def scores(start, width):
        return lax.dot_general(q, k_ref[0, 0, pl.ds(start, width), :], (((1,), (1,)), ((), ())),
                               preferred_element_type=F32)

    def softmax_step(slot, start, width, first):
        if not first:
            m_all = m_ref[...]
        m_parts, a_parts = [], []
        for r0 in range(0, A_TQ, A_RS):
            rows = slice(r0, r0 + A_RS)
            s = s_ref[slot, rows, 0:width]
            s_max = jnp.max(s, axis=-1, keepdims=True)
            if first:
                m_new = s_max
            else:
                m_prev = m_all[rows]
                m_new = jnp.maximum(m_prev, s_max)
                a_parts.append(jnp.exp2(m_prev - m_new))
            m_parts.append(m_new)
            p_ref[rows, 0:width] = jnp.exp2(s - m_new).astype(BF16)
        m_ref[...] = jnp.concatenate(m_parts, axis=0)
        pv = jnp.dot(p_ref[:, 0:width], v_ref[0, 0, pl.ds(start, width), :], preferred_element_type=F32)
        acc_ref[...] = pv if first else jnp.concatenate(a_parts, axis=0) * acc_ref[...] + pv

    def kstart(c):
        return pl.multiple_of(CTX + c * tk, 256)

    s_ref[0, :, 0:CTX] = scores(0, CTX)

    @pl.when(qi == 0)
    def _():
        softmax_step(0, 0, CTX, True)

    @pl.when(qi > 0)
    def _():
        s_ref[1] = scores(CTX, tk)
        softmax_step(0, 0, CTX, True)

        def body(c2, carry):
            c = 2 * c2
            s_ref[0] = scores(kstart(c + 1), tk)
            softmax_step(1, kstart(c), tk, False)
            s_ref[1] = scores(kstart(c + 2), tk)
            softmax_step(0, kstart(c + 1), tk, False)
            return carry

        lax.fori_loop(0, n_lat_chunks // 2 - 1, body, 0)
        c = n_lat_chunks - 2
        s_ref[0] = scores(kstart(c + 1), tk)
        softmax_step(1, kstart(c), tk, False)
        softmax_step(0, kstart(c + 1), tk, False)

    acc = acc_ref[...]
    o_ref[0] = acc[:, 0:A_V] / acc[:, A_V:A_V + 1]


def _attention(q, k, v):
    batch, heads, t_len, _ = q.shape
    tk = min(A_TK, (t_len - CTX) // 2)
    n_lat_chunks = (t_len - CTX) // tk
    assert n_lat_chunks % 2 == 0 and n_lat_chunks * tk == t_len - CTX and tk % 256 == 0
    return pl.pallas_call(
        functools.partial(_attn_kernel, n_lat_chunks=n_lat_chunks, tk=tk),
        out_shape=jax.ShapeDtypeStruct((batch, t_len, heads * A_V), F32),
        grid=(batch, heads, t_len // A_TQ),
        in_specs=[
            pl.BlockSpec((1, 1, A_TQ, A_HD), lambda b, h, i: (b, h, i, 0)),
            pl.BlockSpec((1, 1, t_len, A_HD), lambda b, h, i: (b, h, 0, 0)),
            pl.BlockSpec((1, 1, t_len, 2 * A_V), lambda b, h, i: (b, h, 0, 0)),
        ],
        out_specs=pl.BlockSpec((1, A_TQ, A_V), lambda b, h, i: (b, i, h)),
        scratch_shapes=[
            pltpu.VMEM((2, A_TQ, tk), F32),
            pltpu.VMEM((A_TQ, tk), BF16),
            pltpu.VMEM((A_TQ, 1), F32),
            pltpu.VMEM((A_TQ, 2 * A_V), F32),
        ],
        compiler_params=_cparams(("parallel", "parallel", "arbitrary")),
        name="mla_attention",
    )(q, k, v)


def _conv_kernel(x_ref, prev_ref, next_ref, w_ref, b_ref, post_ref, o_ref, *, t_len):
    i = pl.program_id(0)
    x = x_ref[...]
    row = lax.broadcasted_iota(jnp.int32, (TM, 1), 0)
    pos = (i * TM + row) % t_len
    x_prev = jnp.where(row == 0, prev_ref[7:8, :], pltpu.roll(x, 1, axis=0))
    x_next = jnp.where(row == TM - 1, next_ref[0:1, :], pltpu.roll(x, TM - 1, axis=0))
    seq_start = (pos == 0) | (pos == CTX)
    seq_end = (pos == CTX - 1) | (pos == t_len - 1)
    x_prev = jnp.where(seq_start, 0.0, x_prev)
    x_next = jnp.where(seq_end, 0.0, x_next)
    y = w_ref[0:1, :] * x_prev + w_ref[1:2, :] * x + w_ref[2:3, :] * x_next + b_ref[...]
    o_ref[...] = _silu(y) * post_ref[...]


def _mlstm_conv(p0, conv_w, conv_b, post, t_len):
    rows = p0.shape[0]
    n8 = rows // 8
    cb = P0_MQ // 1024
    return pl.pallas_call(
        functools.partial(_conv_kernel, t_len=t_len),
        out_shape=jax.ShapeDtypeStruct((rows, 1024), F32),
        grid=(rows // TM,),
        in_specs=[
            pl.BlockSpec((TM, 1024), lambda i: (i, cb)),
            pl.BlockSpec((8, 1024), lambda i: (jnp.maximum(i * (TM // 8) - 1, 0), cb)),
            pl.BlockSpec((8, 1024), lambda i: (jnp.minimum((i + 1) * (TM // 8), n8 - 1), cb)),
            _const_spec((3, 1024)),
            _const_spec((1, 1024)),
            _const_spec((1, 1024)),
        ],
        out_specs=pl.BlockSpec((TM, 1024), lambda i: (i, 0)),
        compiler_params=_cparams(("parallel",)),
        name="mlstm_conv",
    )(p0, p0, p0, conv_w, conv_b, post)


def _log_sigmoid(x):
    return jnp.minimum(x, 0.0) - jnp.log(1.0 + jnp.exp(-jnp.abs(x)))


def _mlstm_kernel(qkf_ref, qkb_ref, vf_ref, vb_ref, gf_ref, gb_ref, gbias_ref, hf_ref, hb_ref, c_ref, m_ref):
    j = pl.program_id(1)
    L = M_CHUNK

    @pl.when(j == 0)
    def _():
        c_ref[...] = jnp.zeros_like(c_ref)
        m_ref[...] = jnp.zeros_like(m_ref)

    r_i = lax.broadcasted_iota(jnp.int32, (L, L), 0)
    c_i = lax.broadcasted_iota(jnp.int32, (L, L), 1)
    lane = lax.broadcasted_iota(jnp.int32, (L, 128), 1)
    ones_col = jnp.where(lane == 0, 1.0, 0.0).astype(BF16)

    for d, (qk_ref, v_ref, g_ref, h_ref) in enumerate(((qkf_ref, vf_ref, gf_ref, hf_ref),
                                                        (qkb_ref, vb_ref, gb_ref, hb_ref))):
        rev = d == 1
        keep = (c_i >= r_i) if rev else (c_i <= r_i)
        tri = jnp.where(keep, 1.0, 0.0).astype(F32)
        pre = g_ref[...] + gbias_ref[...]
        csum = jnp.dot(tri, _log_sigmoid(pre), precision=HIGHEST, preferred_element_type=F32)
        csum_t = csum.T
        pre_t = pre.T
        for h in range(M_HEADS):
            ci = 8 * d + h
            cf = 8 * d + 4 + h
            g_col = csum[:, cf:cf + 1]
            g_row = csum_t[cf:cf + 1, :]
            i_col = pre[:, ci:ci + 1]
            i_row = pre_t[ci:ci + 1, :]
            g_end = g_row[:, 0:1] if rev else g_row[:, L - 1:L]
            m_prev = m_ref[4 * d + h][:, 0:1]
            dmat = jnp.where(keep, g_col - g_row + i_row, -jnp.inf)
            inter = g_col + m_prev
            m_t = jnp.maximum(jnp.max(dmat, axis=-1, keepdims=True), inter)
            qh = qk_ref[:, h * 128:(h + 1) * 128].astype(BF16)
            kh32 = qk_ref[:, 512 + h * 128:512 + (h + 1) * 128]
            sc = lax.dot_general(qh, kh32.astype(BF16), (((1,), (1,)), ((), ())), preferred_element_type=F32)
            s = sc * jnp.exp(dmat - m_t)
            w_inter = jnp.exp(inter - m_t)
            v_ext = jnp.concatenate([v_ref[:, h * 128:(h + 1) * 128].astype(BF16), ones_col], axis=1)
            c_st = c_ref[4 * d + h]
            tot = (jnp.dot(s.astype(BF16), v_ext, preferred_element_type=F32)
                   + w_inter * jnp.dot(qh, c_st.astype(BF16), preferred_element_type=F32))
            den = tot[:, 128:129]
            h_ref[:, h * 128:(h + 1) * 128] = tot[:, 0:128] / jnp.maximum(jnp.abs(den), jnp.exp(-m_t))
            w_log = g_end - g_col + i_col
            m_new = jnp.maximum(g_end + m_prev, jnp.max(w_log, axis=0, keepdims=True))
            decay = jnp.exp(g_end + m_prev - m_new)
            kw = (kh32 * jnp.exp(w_log - m_new)).astype(BF16)
            upd = lax.dot_general(kw, v_ext, (((0,), (0,)), ((), ())), preferred_element_type=F32)
            c_ref[4 * d + h] = decay * c_st + upd
            m_ref[4 * d + h] = jnp.broadcast_to(m_new, (1, 128))


def _mlstm_scan(qk, p0, gate_bias, batch, t_len):
    rows = qk.shape[0]
    nch = t_len // M_CHUNK
    nctx = CTX // M_CHUNK
    fwd = lambda b, j: b * nch + j
    bwd = lambda b, j: b * nch + jnp.where(j < nctx, nctx - 1 - j, nch + nctx - 1 - j)
    out = jax.ShapeDtypeStruct((rows, M_HEADS * M_V), F32)
    return pl.pallas_call(
        _mlstm_kernel,
        out_shape=(out, out),
        grid=(batch, nch),
        in_specs=[
            pl.BlockSpec((M_CHUNK, 1024), lambda b, j: (fwd(b, j), 0)),
            pl.BlockSpec((M_CHUNK, 1024), lambda b, j: (bwd(b, j), 0)),
            pl.BlockSpec((M_CHUNK, 512), lambda b, j: (fwd(b, j), P0_MV // 512)),
            pl.BlockSpec((M_CHUNK, 512), lambda b, j: (bwd(b, j), P0_MV // 512)),
            pl.BlockSpec((M_CHUNK, 128), lambda b, j: (fwd(b, j), P0_GATE // 128)),
            pl.BlockSpec((M_CHUNK, 128), lambda b, j: (bwd(b, j), P0_GATE // 128)),
            _const_spec((1, 128)),
        ],
        out_specs=(
            pl.BlockSpec((M_CHUNK, 512), lambda b, j: (fwd(b, j), 0)),
            pl.BlockSpec((M_CHUNK, 512), lambda b, j: (bwd(b, j), 0)),
        ),
        scratch_shapes=[
            pltpu.VMEM((2 * M_HEADS, M_QKP, 256), F32),
            pltpu.VMEM((2 * M_HEADS, 1, 128), F32),
        ],
        compiler_params=_cparams(("parallel", "arbitrary")),
        name="mlstm_scan",
    )(qk, qk, p0, p0, p0, p0, gate_bias)


def _head_rms(x, gain_row, n_heads, width):
    parts = []
    for h in range(n_heads):
        sl = slice(h * width, (h + 1) * width)
        parts.append(_rms(x[:, sl], gain_row[:, sl]))
    return jnp.concatenate(parts, axis=1)


def _ab_out_kernel(x_ref, a_ref, hf_ref, hb_ref, og_ref, hg_ref, wa_ref, wm_ref, gains_ref, tab_ref, o_ref, *, per):
    i = pl.program_id(0)
    m = _head_rms(hf_ref[...] + hb_ref[...], hg_ref[...], M_HEADS, M_V) * _sigmoid(og_ref[...])
    y = (jnp.dot(a_ref[...].astype(BF16), wa_ref[...], preferred_element_type=F32)
         + jnp.dot(m.astype(BF16), wm_ref[...], preferred_element_type=F32))
    _gated_residual_tile(x_ref, y, gains_ref[1:2, :], tab_ref, 2, i, per, TM, o_ref)


def _ab_out(xs, a, hf, hb, p0, head_gain, wa, wm, gains, tab, per):
    rows = xs.shape[0]
    row_spec = lambda w, cb=0: pl.BlockSpec((TM, w), lambda i: (i, cb))
    return pl.pallas_call(
        functools.partial(_ab_out_kernel, per=per),
        out_shape=jax.ShapeDtypeStruct((rows, D), F32),
        grid=(rows // TM,),
        in_specs=[
            row_spec(D), row_spec(512), row_spec(512), row_spec(512), row_spec(512, P0_OG // 512),
            _const_spec((1, 512)), _const_spec((512, D)), _const_spec((512, D)),
            _const_spec((4, D)), _const_spec((6, 8, D)),
        ],
        out_specs=row_spec(D),
        compiler_params=_cparams(("parallel",)),
        name="ab_out",
    )(xs, a, hf, hb, p0, head_gain, wa, wm, gains, tab)


def _swiglu(h, wgu_ref, wdn_ref, lead):
    acc = None
    for c0 in range(0, D_FF, FF_CHUNK):
        g = jnp.dot(h, wgu_ref[lead + (slice(None), slice(c0, c0 + FF_CHUNK))], preferred_element_type=F32)
        u = jnp.dot(h, wgu_ref[lead + (slice(None), slice(D_FF + c0, D_FF + c0 + FF_CHUNK))],
                    preferred_element_type=F32)
        a = (_silu(g) * u).astype(BF16)
        part = jnp.dot(a, wdn_ref[lead + (slice(c0, c0 + FF_CHUNK), slice(None))], preferred_element_type=F32)
        acc = part if acc is None else acc + part
    return acc


def _ffn_kernel(x_ref, gains_ref, tab_ref, wgu_ref, wdn_ref, o_ref, *, per):
    i = pl.program_id(0)
    h = _norm_mod_tile(x_ref, gains_ref[2:3, :], tab_ref, 3, 4, i, per, TM).astype(BF16)
    y = _swiglu(h, wgu_ref, wdn_ref, ())
    _gated_residual_tile(x_ref, y, gains_ref[3:4, :], tab_ref, 5, i, per, TM, o_ref)


def _ffn(xs, gains, tab, wgu, wdn, per):
    rows = xs.shape[0]
    return pl.pallas_call(
        functools.partial(_ffn_kernel, per=per),
        out_shape=jax.ShapeDtypeStruct((rows, D), F32),
        grid=(rows // TM,),
        in_specs=[
            pl.BlockSpec((TM, D), lambda i: (i, 0)),
            _const_spec((4, D)), _const_spec((6, 8, D)),
            _const_spec((D, 2 * D_FF)), _const_spec((D_FF, D)),
        ],
        out_specs=pl.BlockSpec((TM, D), lambda i: (i, 0)),
        compiler_params=_cparams(("parallel",)),
        name="ffn",
    )(xs, gains, tab, wgu, wdn)


def _hgrn_stream(q, k, v, g, st, rev):
    c = G_SUB
    ns = G_CHUNK // c
    blk = lambda a, i: a[c * i:c * (i + 1)]
    row = lambda a, r: a[r:r + 1]
    bound = [row(g, c * j) if rev else row(g, c * j + c - 1) for j in range(ns)]
    khat = jnp.concatenate([blk(k, j) * jnp.exp(bound[j] - blk(g, j)) for j in range(ns)], axis=0)
    pairs = [(i, j) for i in range(ns) for j in range(ns) if (i < j if rev else i > j)]
    qst = jnp.concatenate([blk(q, i) * jnp.exp(blk(g, i) - bound[j]) for (i, j) in pairs], axis=0)
    cross = lax.dot_general(qst.astype(BF16), khat.astype(BF16), (((1,), (1,)), ((), ())),
                            preferred_element_type=F32)
    lane = lax.broadcasted_iota(jnp.int32, (c, G_CHUNK), 1)
    sub = lax.broadcasted_iota(jnp.int32, (c, 128), 0)
    prods = []
    for i in range(ns):
        gi, qi, ki = blk(g, i), blk(q, i), blk(k, i)
        for s in range(c):
            seen = (sub <= s) if rev else (sub >= s)
            prods.append(qi * row(ki, s) * jnp.exp(jnp.where(seen, gi - row(gi, s), -jnp.inf)))
    red = jnp.dot(jnp.concatenate(prods, axis=0).astype(BF16), jnp.ones((128, G_CHUNK), BF16),
                  preferred_element_type=F32)
    a_rows = []
    for i in range(ns):
        acc = jnp.zeros((c, G_CHUNK), F32)
        for p, (ii, j) in enumerate(pairs):
            if ii == i:
                acc = jnp.where(lane // c == j, cross[c * p:c * (p + 1)], acc)
        for s in range(c):
            r0 = (i * c + s) * c
            acc = jnp.where(lane == i * c + s, red[r0:r0 + c], acc)
        a_rows.append(acc)
    a = jnp.concatenate(a_rows, axis=0).astype(BF16)
    g_end = row(g, 0) if rev else row(g, G_CHUNK - 1)
    o = (jnp.dot(a, v.astype(BF16), preferred_element_type=F32)
         + lax.dot_general((q * jnp.exp(g)).astype(BF16), st.astype(BF16), (((1,), (1,)), ((), ())),
                           preferred_element_type=F32))
    kw = (k * jnp.exp(g_end - g)).astype(BF16)
    st_new = st * jnp.exp(g_end) + lax.dot_general(v.astype(BF16), kw, (((0,), (0,)), ((), ())),
                                                    preferred_element_type=F32)
    return o, st_new


def _hgrn_kernel(qf_ref, zf_ref, vf_ref, qb_ref, zb_ref, vb_ref, lb_ref, of_ref, ob_ref, st_ref):
    j = pl.program_id(1)
    L = G_CHUNK

    @pl.when(j == 0)
    def _():
        st_ref[...] = jnp.zeros_like(st_ref)

    r_i = lax.broadcasted_iota(jnp.int32, (L, L), 0)
    c_i = lax.broadcasted_iota(jnp.int32, (L, L), 1)
    lb = lb_ref[...]
    for d, (q_ref, z_ref, v_ref, o_ref) in enumerate(((qf_ref, zf_ref, vf_ref, of_ref),
                                                       (qb_ref, zb_ref, vb_ref, ob_ref))):
        rev = d == 1
        tri = jnp.where((c_i >= r_i) if rev else (c_i <= r_i), 1.0, 0.0).astype(F32)
        z = z_ref[...]
        e = jnp.exp(-jnp.abs(z))
        r = 1.0 / (1.0 + e)
        sig_pos = jnp.where(z >= 0, r, e * r)
        sig_neg = jnp.where(z >= 0, e * r, r)
        log_f = jnp.log(lb + (1.0 - lb) * sig_pos)
        k_all = (1.0 - lb) * sig_neg
        g_all = jnp.dot(tri, log_f, precision=HIGHEST, preferred_element_type=F32)
        q_all = _silu(q_ref[...])
        for h in range(G_HEADS):
            sl = slice(h * 128, (h + 1) * 128)
            o, st_new = _hgrn_stream(q_all[:, sl], k_all[:, sl], v_ref[:, sl], g_all[:, sl], st_ref[d * G_HEADS + h], rev)
            o_ref[:, sl] = o
            st_ref[d * G_HEADS + h] = st_new


def _hgrn_scan(p1, lb_row, batch, t_len):
    rows = p1.shape[0]
    nch = t_len // G_CHUNK
    nctx = CTX // G_CHUNK
    fwd = lambda b, j: b * nch + j
    bwd = lambda b, j: b * nch + jnp.where(j < nctx, nctx - 1 - j, nch + nctx - 1 - j)
    spec = lambda order, cb: pl.BlockSpec((G_CHUNK, 1024), lambda b, j: (order(b, j), cb))
    out = jax.ShapeDtypeStruct((rows, G_HEADS * G_DV), F32)
    return pl.pallas_call(
        _hgrn_kernel,
        out_shape=(out, out),
        grid=(batch, nch),
        in_specs=[spec(fwd, 0), spec(fwd, 1), spec(fwd, 3), spec(bwd, 0), spec(bwd, 2), spec(bwd, 3),
                  _const_spec((1, 1024))],
        out_specs=(spec(fwd, 0), spec(bwd, 0)),
        scratch_shapes=[pltpu.VMEM((2 * G_HEADS, G_DV, G_DK), F32)],
        compiler_params=_cparams(("parallel", "arbitrary")),
        name="hgrn_scan",
    )(p1, p1, p1, p1, p1, p1, lb_row)


def _c_out_kernel(x_ref, of_ref, ob_ref, hg_ref, gain_ref, w_ref, gains_ref, tab_ref, o_ref, *, per):
    i = pl.program_id(0)
    o = _head_rms(of_ref[...] + ob_ref[...], gain_ref[...], G_HEADS, G_DV) * _silu(hg_ref[...])
    y = jnp.dot(o.astype(BF16), w_ref[...], preferred_element_type=F32)
    _gated_residual_tile(x_ref, y, gains_ref[1:2, :], tab_ref, 2, i, per, TM, o_ref)


def _c_out(xs, of, ob, p1, head_gain, w, gains, tab, per):
    rows = xs.shape[0]
    row_spec = lambda cb=0: pl.BlockSpec((TM, D), lambda i: (i, cb))
    return pl.pallas_call(
        functools.partial(_c_out_kernel, per=per),
        out_shape=jax.ShapeDtypeStruct((rows, D), F32),
        grid=(rows // TM,),
        in_specs=[row_spec(), row_spec(), row_spec(), row_spec(4), _const_spec((1, D)), _const_spec((D, D)),
                  _const_spec((4, D)), _const_spec((6, 8, D))],
        out_specs=row_spec(),
        compiler_params=_cparams(("parallel",)),
        name="c_out",
    )(xs, of, ob, p1, head_gain, w, gains, tab)


def _router_kernel(x_ref, gains_ref, tab_ref, wr_ref, h_ref, r_ref, *, per):
    i = pl.program_id(0)
    h = _norm_mod_tile(x_ref, gains_ref[2:3, :], tab_ref, 3, 4, i, per, TM)
    h_ref[...] = h
    logits = lax.dot_general(wr_ref[...], h, (((1,), (1,)), ((), ())), precision=HIGHEST,
                             preferred_element_type=F32)
    eid = lax.broadcasted_iota(jnp.int32, logits.shape, 0).astype(F32)
    m1 = jnp.max(logits, axis=0, keepdims=True)
    i1 = jnp.min(jnp.where(logits == m1, eid, float(N_EXP)), axis=0, keepdims=True)
    rest = jnp.where(eid == i1, -jnp.inf, logits)
    m2 = jnp.max(rest, axis=0, keepdims=True)
    i2 = jnp.min(jnp.where(rest == m2, eid, float(N_EXP)), axis=0, keepdims=True)
    e2 = jnp.exp(m2 - m1)
    g1 = 1.0 / (1.0 + e2)
    g2 = e2 * g1
    rid = lax.broadcasted_iota(jnp.int32, (128, TM), 0)
    packed = jnp.where(rid == 0, i1, jnp.where(rid == 1, i2, jnp.where(rid == 2, g1, jnp.where(rid == 3, g2, 0.0))))
    r_ref[...] = packed.T


def _router(xs, gains, tab, wr_t, per):
    rows = xs.shape[0]
    return pl.pallas_call(
        functools.partial(_router_kernel, per=per),
        out_shape=(jax.ShapeDtypeStruct((rows, D), F32), jax.ShapeDtypeStruct((rows, 128), F32)),
        grid=(rows // TM,),
        in_specs=[pl.BlockSpec((TM, D), lambda i: (i, 0)), _const_spec((4, D)), _const_spec((6, 8, D)),
                  _const_spec((N_EXP, D))],
        out_specs=(pl.BlockSpec((TM, D), lambda i: (i, 0)), pl.BlockSpec((TM, 128), lambda i: (i, 0))),
        compiler_params=_cparams(("parallel",)),
        name="moe_router",
    )(xs, gains, tab, wr_t)


def _row_copy(src_hbm, src_row, dst_vmem, dst_row, sem):
    return pltpu.make_async_copy(src_hbm.at[pl.ds(src_row, 1)], dst_vmem.at[pl.ds(dst_row, 1)], sem)


def _expert_kernel(be_ref, nact_ref, tok_ref, h_hbm, wgu_ref, wdn_ref, o_ref, xbuf, sem):
    b = pl.program_id(0)

    @pl.when(b < nact_ref[0])
    def _():
        def issue(r, carry):
            _row_copy(h_hbm, tok_ref[0, 0, r], xbuf, r, sem).start()
            return carry

        lax.fori_loop(0, MOE_BLOCK, issue, 0)

        def wait(r, carry):
            _row_copy(h_hbm, 0, xbuf, r, sem).wait()
            return carry

        lax.fori_loop(0, MOE_BLOCK, wait, 0)
        o_ref[...] = _swiglu(xbuf[...].astype(BF16), wgu_ref, wdn_ref, (0,))

    @pl.when(b >= nact_ref[0])
    def _():
        o_ref[...] = jnp.zeros_like(o_ref)


def _experts(block_expert, n_active, slot_tok, h, wgu, wdn):
    n_blocks = block_expert.shape[0]
    grid_spec = pltpu.PrefetchScalarGridSpec(
        num_scalar_prefetch=2,
        grid=(n_blocks,),
        in_specs=[
            pl.BlockSpec((1, 1, MOE_BLOCK), lambda b, be, na: (b, 0, 0), memory_space=pltpu.SMEM),
            pl.BlockSpec(memory_space=pl.ANY),
            pl.BlockSpec((1, D, 2 * D_FF), lambda b, be, na: (be[b], 0, 0)),
            pl.BlockSpec((1, D_FF, D), lambda b, be, na: (be[b], 0, 0)),
        ],
        out_specs=pl.BlockSpec((MOE_BLOCK, D), lambda b, be, na: (b, 0)),
        scratch_shapes=[pltpu.VMEM((MOE_BLOCK, D), F32), pltpu.SemaphoreType.DMA(())],
    )
    return pl.pallas_call(
        _expert_kernel,
        out_shape=jax.ShapeDtypeStruct((n_blocks * MOE_BLOCK, D), F32),
        grid_spec=grid_spec,
        compiler_params=_cparams(("arbitrary",)),
        name="moe_experts",
    )(block_expert, n_active, slot_tok.reshape(n_blocks, 1, MOE_BLOCK), h, wgu, wdn)


def _combine_kernel(pos_ref, ys_hbm, r_ref, x_ref, gains_ref, tab_ref, o_ref, buf, sem, *, per, lat_only):
    if lat_only:
        tile = pl.program_id(0) * per + 1 + pl.program_id(1)
    else:
        tile = pl.program_id(0)

    def issue(r, carry):
        _row_copy(ys_hbm, pos_ref[0, 0, r], buf, r, sem).start()
        return carry

    lax.fori_loop(0, 2 * HALF, issue, 0)

    def wait(r, carry):
        _row_copy(ys_hbm, 0, buf, r, sem).wait()
        return carry

    lax.fori_loop(0, 2 * HALF, wait, 0)
    gates = r_ref[...]
    y = gates[:, 2:3] * buf[0:HALF, :] + gates[:, 3:4] * buf[HALF:2 * HALF, :]
    idx = _mod_index(tile, per)
    gate = tab_ref[5, pl.ds(idx, 1), :]
    o_ref[...] = x_ref[...] + gate * _rms(y, gains_ref[3:4, :])


def _combine_latents(pos, ys, rout, xs, gains, tab, batch, s_len, per):
    nl = s_len // HALF
    tile = lambda b, j: b * per + 1 + j
    return pl.pallas_call(
        functools.partial(_combine_kernel, per=per, lat_only=True),
        out_shape=jax.ShapeDtypeStruct((batch * s_len, D), F32),
        grid=(batch, nl),
        in_specs=[
            pl.BlockSpec((1, 1, 2 * HALF), lambda b, j: (tile(b, j), 0, 0), memory_space=pltpu.SMEM),
            pl.BlockSpec(memory_space=pl.ANY),
            pl.BlockSpec((HALF, 128), lambda b, j: (tile(b, j), 0)),
            pl.BlockSpec((HALF, D), lambda b, j: (tile(b, j), 0)),
            _const_spec((4, D)), _const_spec((6, 8, D)),
        ],
        out_specs=pl.BlockSpec((HALF, D), lambda b, j: (b * nl + j, 0)),
        scratch_shapes=[pltpu.VMEM((2 * HALF, D), F32), pltpu.SemaphoreType.DMA(())],
        compiler_params=_cparams(("arbitrary", "arbitrary")),
        name="moe_combine",
    )(pos, ys, rout, xs, gains, tab)


def _moe_plan(rout):
    n_tok = rout.shape[0]
    n_flat = 2 * n_tok
    flat_e = rout[:, 0:2].astype(jnp.int32).reshape(-1)
    onehot = (flat_e[:, None] == jnp.arange(N_EXP, dtype=jnp.int32)[None, :]).astype(jnp.int32)
    csum = jnp.cumsum(onehot, axis=0)
    counts = csum[-1]
    rank = jnp.sum(csum * onehot, axis=1) - 1
    padded = (counts + MOE_BLOCK - 1) // MOE_BLOCK * MOE_BLOCK
    pend = jnp.cumsum(padded)
    pstart = pend - padded
    start = jnp.cumsum(counts) - counts
    dest = pstart[flat_e] + rank
    n_blocks = -(-n_flat // MOE_BLOCK) + N_EXP
    block_expert = jnp.minimum(
        jnp.searchsorted(pend, jnp.arange(n_blocks, dtype=jnp.int32) * MOE_BLOCK, side='right'), N_EXP - 1
    ).astype(jnp.int32)
    n_active = (pend[-1] // MOE_BLOCK).astype(jnp.int32).reshape(1)
    order = jnp.argsort(flat_e, stable=True).astype(jnp.int32)
    tok_sorted = order // 2
    slot = jnp.arange(n_blocks * MOE_BLOCK, dtype=jnp.int32)
    e_slot = block_expert[slot // MOE_BLOCK]
    off = slot - pstart[e_slot]
    valid = (off >= 0) & (off < counts[e_slot])
    src = jnp.clip(start[e_slot] + off, 0, n_flat - 1)
    slot_tok = jnp.where(valid, tok_sorted[src], 0).astype(jnp.int32)
    dest2 = dest.reshape(n_tok, 2).astype(jnp.int32)
    pos = jnp.concatenate([dest2[:, 0].reshape(-1, 1, HALF), dest2[:, 1].reshape(-1, 1, HALF)], axis=2)
    return block_expert, n_active, slot_tok, pos


def _rot_half_cols(w):
    half = A_ROPE // 2
    g = w.reshape(w.shape[0], -1, A_ROPE)
    return jnp.concatenate([-g[..., half:], g[..., :half]], axis=-1).reshape(w.shape)


def _pad_groups(w, width, to):
    k = w.shape[0]
    g = w.reshape(k, -1, width)
    return jnp.pad(g, ((0, 0), (0, 0), (0, to - width))).reshape(k, -1)


def _layer0_inproj_weight(w_in):
    splits = np.cumsum([A_QLORA, A_KVLORA, A_ROPE, M_HEADS * M_QK, M_HEADS * M_QK, M_HEADS * M_V, M_HEADS * M_V])
    cq, ckv, kr, mq, mk, mv, og, gt = jnp.split(w_in, [int(s) for s in splits], axis=1)
    pad64 = lambda a: jnp.pad(a, ((0, 0), (0, 64)))
    cols = [cq, ckv, pad64(kr), mv, _pad_groups(mq, M_QK, M_QKP), _pad_groups(mk, M_QK, M_QKP), og,
            jnp.pad(gt, ((0, 0), (0, 128 - gt.shape[1]))), pad64(_rot_half_cols(kr))]
    w = jnp.concatenate(cols, axis=1)
    assert w.shape[1] == P0_N
    return w.astype(BF16)


def _rope_tables(batch, s_len):
    t = jnp.arange(s_len)
    row = (t // GRID_W).astype(F32)
    col = (t % GRID_W).astype(F32)
    n_freq = A_ROPE // 4
    inv_freq = ROPE_THETA ** (-jnp.arange(n_freq, dtype=F32) / n_freq)
    ang = jnp.concatenate([row[:, None] * inv_freq, col[:, None] * inv_freq], axis=-1)
    ang = jnp.concatenate([ang, ang], axis=-1)

    def table(vals, fill):
        lat = jnp.pad(vals, ((0, 0), (0, 128 - A_ROPE)), constant_values=fill)
        one = jnp.concatenate([jnp.full((CTX, 128), fill, F32), lat], axis=0)
        return jnp.tile(one, (batch, 1))

    return table(jnp.cos(ang), 1.0), table(jnp.sin(ang), 0.0)


def kernel(x, c, ctx, c_ctx, w_mod, b_mod, norm_gains, w_in_ab, mla_q_gain, mla_w_uq, mla_kv_gain, mla_w_ukv,
           ml_conv_w, ml_conv_b, ml_gate_b, ml_head_gain, w_out_ab, ffn_w_gu, ffn_w_dn, w_in_c, hg_lb_logits,
           hg_head_gain, w_out_c, moe_router, moe_w_gu, moe_w_dn):
    batch, s_len, d = x.shape
    assert d == D and ctx.shape[1] == CTX and batch == 2
    assert s_len % 512 == 0 and (batch * (CTX + s_len)) % TM == 0
    t_len = CTX + s_len
    per = t_len // HALF

    xs = jnp.concatenate([ctx, x], axis=1).reshape(batch * t_len, D)
    cc = jnp.concatenate([c, c_ctx[None, :], jnp.zeros((8 - batch - 1, D), F32)], axis=0)
    tabs = _mod_table(cc, w_mod, b_mod)

    gains = norm_gains[0]
    tab = tabs[0]
    p0 = _inproj(xs, gains, tab, _layer0_inproj_weight(w_in_ab[0]), per, P0_N // 2)

    cos, sin = _rope_tables(batch, s_len)
    w_uq = mla_w_uq[0].reshape(A_QLORA, A_HEADS, A_NOPE + A_ROPE)
    wq_nope = w_uq[:, :, :A_NOPE].reshape(A_QLORA, A_HEADS * A_NOPE)
    wq_rope = w_uq[:, :, A_NOPE:].reshape(A_QLORA, A_HEADS * A_ROPE)
    w_ukv = mla_w_ukv[0].reshape(A_KVLORA, A_HEADS, A_NOPE + A_V)
    q, k, v = _mla_prep(
        p0, cos, sin, mla_q_gain[0][None, :], mla_kv_gain[0][None, :],
        wq_nope.astype(BF16),
        _pad_groups(wq_rope, A_ROPE, 128).astype(BF16),
        _pad_groups(_rot_half_cols(wq_rope), A_ROPE, 128).astype(BF16),
        w_ukv[:, :, :A_NOPE].reshape(A_KVLORA, -1).astype(BF16),
        w_ukv[:, :, A_NOPE:].reshape(A_KVLORA, -1).astype(BF16),
        batch, t_len)
    attn = _attention(q, k, v).reshape(batch * t_len, A_HEADS * A_V)

    conv_w = jnp.concatenate([_pad_groups(ml_conv_w[0][:, :M_HEADS * M_QK], M_QK, M_QKP),
                              _pad_groups(ml_conv_w[0][:, M_HEADS * M_QK:], M_QK, M_QKP)], axis=1)
    conv_b = jnp.concatenate([_pad_groups(ml_conv_b[0][None, :M_HEADS * M_QK], M_QK, M_QKP),
                              _pad_groups(ml_conv_b[0][None, M_HEADS * M_QK:], M_QK, M_QKP)], axis=1)
    post = jnp.concatenate([jnp.ones((1, 512), F32), jnp.full((1, 512), M_QK ** -0.5, F32)], axis=1)
    qk = _mlstm_conv(p0, conv_w, conv_b, post, t_len)
    gate_bias = jnp.pad(ml_gate_b[0][None, :], ((0, 0), (0, 128 - 4 * M_HEADS)))
    hf, hb = _mlstm_scan(qk, p0, gate_bias, batch, t_len)

    w_out = w_out_ab[0].astype(BF16)
    xs = _ab_out(xs, attn, hf, hb, p0, ml_head_gain[0][None, :], w_out[:A_HEADS * A_V], w_out[A_HEADS * A_V:],
                 gains, tab, per)
    xs = _ffn(xs, gains, tab, ffn_w_gu[0].astype(BF16), ffn_w_dn[0].astype(BF16), per)

    gains = norm_gains[1]
    tab = tabs[1]
    lbs = jnp.cumsum(jax.nn.softmax(hg_lb_logits.astype(F32), axis=0), axis=0)
    lb_row = (lbs - lbs[0])[1][None, :]
    p1 = _inproj(xs, gains, tab, w_in_c[0].astype(BF16), per, 1024)
    of, ob = _hgrn_scan(p1, lb_row, batch, t_len)
    xs = _c_out(xs, of, ob, p1, hg_head_gain[0][None, :], w_out_c[0].astype(BF16), gains, tab, per)

    h, rout = _router(xs, gains, tab, moe_router[0].T, per)
    block_expert, n_active, slot_tok, pos = _moe_plan(rout)
    ys = _experts(block_expert, n_active, slot_tok, h, moe_w_gu[0].astype(BF16), moe_w_dn[0].astype(BF16))
    out = _combine_latents(pos, ys, rout, xs, gains, tab, batch, s_len, per)
    return out.reshape(batch, s_len, D)
```

```python
import functools

import jax
import jax.numpy as jnp
import numpy as np
from jax import lax
from jax.experimental import pallas as pl
from jax.experimental.pallas import tpu as pltpu

F32 = jnp.float32
BF16 = jnp.bfloat16
HIGHEST = lax.Precision.HIGHEST
LOG2E = 1.4426950408889634

D = 1024
CTX = 256
EPS = 1e-6
HALF = 256
TM = 512
VMEM_LIMIT = 56 * 1024 * 1024

A_HEADS = 4
A_QLORA = 256
A_KVLORA = 128
A_NOPE = 128
A_ROPE = 64
A_V = 128
A_HD = 256
A_SCALE = (A_NOPE + A_ROPE) ** -0.5
ROPE_THETA = 10000.0
GRID_W = 64
A_QSCALE = A_SCALE * LOG2E
A_TQ = 256
A_TK = 4096
A_RS = 32

M_HEADS = 4
M_QK = 64
M_V = 128
M_CHUNK = 128
M_QKP = 128

G_HEADS = 8
G_DK = 128
G_DV = 128
G_CHUNK = 64
G_SUB = 8

D_FF = 2816
N_EXP = 8
MOE_BLOCK = 512
FF_CHUNK = 256

P0_CQ = 0
P0_CKV = 256
P0_KR = 384
P0_MV = 512
P0_MQ = 1024
P0_MK = 1536
P0_OG = 2048
P0_GATE = 2560
P0_KRR = 2688
P0_N = 2816


def _cparams(sem):
    return pltpu.CompilerParams(dimension_semantics=sem, vmem_limit_bytes=VMEM_LIMIT)


def _const_spec(shape):
    nd = len(shape)
    return pl.BlockSpec(shape, lambda *_: (0,) * nd, pipeline_mode=pl.Buffered(1))


def _rms(x, gain_row):
    ms = jnp.mean(x * x, axis=-1, keepdims=True)
    return x * lax.rsqrt(ms + EPS) * gain_row


def _mod_index(half_idx, per):
    return jnp.where(half_idx % per == 0, 2, half_idx // per)


def _silu(x):
    return x * _sigmoid(x)


def _sigmoid(x):
    return 1.0 / (1.0 + jnp.exp(-x))


def _norm_mod_rows(x, gain_row, tab_ref, k_shift, k_scale, idx):
    shift = tab_ref[k_shift, pl.ds(idx, 1), :]
    scale = tab_ref[k_scale, pl.ds(idx, 1), :]
    return _rms(x, gain_row) * (1.0 + scale) + shift


def _norm_mod_tile(x_ref, gain_row, tab_ref, k_shift, k_scale, tile, per, rows):
    parts = []
    for s in range(rows // HALF):
        idx = _mod_index(tile * (rows // HALF) + s, per)
        parts.append(_norm_mod_rows(x_ref[s * HALF:(s + 1) * HALF, :], gain_row, tab_ref, k_shift, k_scale, idx))
    return parts[0] if len(parts) == 1 else jnp.concatenate(parts, axis=0)


def _gated_residual_tile(x_ref, y, gain_row, tab_ref, k_gate, tile, per, rows, o_ref):
    yn = _rms(y, gain_row)
    for s in range(rows // HALF):
        idx = _mod_index(tile * (rows // HALF) + s, per)
        gate = tab_ref[k_gate, pl.ds(idx, 1), :]
        sl = slice(s * HALF, (s + 1) * HALF)
        o_ref[sl, :] = x_ref[sl, :] + gate * yn[sl, :]


def _mod_kernel(c_ref, w_ref, b_ref, o_ref):
    a = _silu(c_ref[...])
    o_ref[0, 0] = jnp.dot(a, w_ref[0], precision=HIGHEST, preferred_element_type=F32) + b_ref[0, 0]


def _mod_table(cc, w_mod, b_mod):
    depth = w_mod.shape[0]
    return pl.pallas_call(
        _mod_kernel,
        out_shape=jax.ShapeDtypeStruct((depth, 6, 8, D), F32),
        grid=(depth, 6),
        in_specs=[
            pl.BlockSpec((8, D), lambda l, k: (0, 0)),
            pl.BlockSpec((1, D, D), lambda l, k: (l, 0, k)),
            pl.BlockSpec((1, 1, 1, D), lambda l, k: (l, k, 0, 0)),
        ],
        out_specs=pl.BlockSpec((1, 1, 8, D), lambda l, k: (l, k, 0, 0)),
        compiler_params=_cparams(("arbitrary", "arbitrary")),
        name="mod_table",
    )(cc, w_mod, b_mod.reshape(depth, 6, 1, D))


def _inproj_kernel(x_ref, gains_ref, tab_ref, w_ref, o_ref, *, per, nchunk):
    i = pl.program_id(0)
    h = _norm_mod_tile(x_ref, gains_ref[0:1, :], tab_ref, 0, 1, i, per, TM).astype(BF16)
    n = w_ref.shape[1]
    for n0 in range(0, n, nchunk):
        o_ref[:, n0:n0 + nchunk] = jnp.dot(h, w_ref[:, n0:n0 + nchunk], preferred_element_type=F32)


def _inproj(xs, gains, tab, w, per, nchunk):
    rows = xs.shape[0]
    n = w.shape[1]
    return pl.pallas_call(
        functools.partial(_inproj_kernel, per=per, nchunk=nchunk),
        out_shape=jax.ShapeDtypeStruct((rows, n), F32),
        grid=(rows // TM,),
        in_specs=[
            pl.BlockSpec((TM, D), lambda i: (i, 0)),
            _const_spec((4, D)),
            _const_spec((6, 8, D)),
            _const_spec((D, n)),
        ],
        out_specs=pl.BlockSpec((TM, n), lambda i: (i, 0)),
        compiler_params=_cparams(("parallel",)),
        name="inproj",
    )(xs, gains, tab, w)


def _mla_prep_kernel(cq_ref, ckr_ref, krr_ref, cos_ref, sin_ref, qg_ref, kvg_ref, wqn_ref, wqr_ref, wqrr_ref,
                     wk_ref, wv_ref, q_ref, k_ref, v_ref):
    qn = _rms(cq_ref[...], qg_ref[...]).astype(BF16)
    ckr = ckr_ref[...]
    kvn = _rms(ckr[:, :A_KVLORA], kvg_ref[...]).astype(BF16)
    cos = cos_ref[...]
    sin = sin_ref[...]
    q_nope = jnp.dot(qn, wqn_ref[...], preferred_element_type=F32)
    q_r = jnp.dot(qn, wqr_ref[...], preferred_element_type=F32)
    q_rr = jnp.dot(qn, wqrr_ref[...], preferred_element_type=F32)
    k_nope = jnp.dot(kvn, wk_ref[...], preferred_element_type=F32)
    vv = jnp.dot(kvn, wv_ref[...], preferred_element_type=F32)
    k_rope = (ckr[:, A_KVLORA:] * cos + krr_ref[...] * sin).astype(BF16)
    lane = lax.broadcasted_iota(jnp.int32, (HALF, A_V), 1)
    ones_col = jnp.where(lane == 0, 1.0, 0.0).astype(BF16)
    for h in range(A_HEADS):
        sl = slice(h * 128, (h + 1) * 128)
        q_rope = q_r[:, sl] * cos + q_rr[:, sl] * sin
        q_ref[0, h, :, 0:128] = (q_nope[:, sl] * A_QSCALE).astype(BF16)
        q_ref[0, h, :, 128:256] = (q_rope * A_QSCALE).astype(BF16)
        k_ref[0, h, :, 0:128] = k_nope[:, sl].astype(BF16)
        k_ref[0, h, :, 128:256] = k_rope
        v_ref[0, h, :, 0:A_V] = vv[:, sl].astype(BF16)
        v_ref[0, h, :, A_V:2 * A_V] = ones_col


def _mla_prep(p0, cos, sin, q_gain, kv_gain, wqn, wqr, wqrr, wk, wv, batch, t_len):
    rows = p0.shape[0]
    per = t_len // HALF
    qk_shape = jax.ShapeDtypeStruct((batch, A_HEADS, t_len, A_HD), BF16)
    v_shape = jax.ShapeDtypeStruct((batch, A_HEADS, t_len, 2 * A_V), BF16)
    out_map = lambda i: (i // per, 0, i % per, 0)
    return pl.pallas_call(
        _mla_prep_kernel,
        out_shape=(qk_shape, qk_shape, v_shape),
        grid=(rows // HALF,),
        in_specs=[
            pl.BlockSpec((HALF, 256), lambda i: (i, P0_CQ // 256)),
            pl.BlockSpec((HALF, 256), lambda i: (i, P0_CKV // 256)),
            pl.BlockSpec((HALF, 128), lambda i: (i, P0_KRR // 128)),
            pl.BlockSpec((HALF, 128), lambda i: (i, 0)),
            pl.BlockSpec((HALF, 128), lambda i: (i, 0)),
            _const_spec((1, A_QLORA)),
            _const_spec((1, A_KVLORA)),
            _const_spec((A_QLORA, 512)),
            _const_spec((A_QLORA, 512)),
            _const_spec((A_QLORA, 512)),
            _const_spec((A_KVLORA, 512)),
            _const_spec((A_KVLORA, 512)),
        ],
        out_specs=(
            pl.BlockSpec((1, A_HEADS, HALF, A_HD), out_map),
            pl.BlockSpec((1, A_HEADS, HALF, A_HD), out_map),
            pl.BlockSpec((1, A_HEADS, HALF, 2 * A_V), out_map),
        ),
        compiler_params=_cparams(("parallel",)),
        name="mla_prep",
    )(p0, p0, p0, cos, sin, q_gain, kv_gain, wqn, wqr, wqrr, wk, wv)


def _attn_kernel(q_ref, k_ref, v_ref, o_ref, s_ref, p_ref, m_ref, acc_ref, *, n_lat_chunks, tk):
    qi = pl.program_id(2)
    q = q_ref[0, 0]

    def scores(start, width):
        return lax.dot_general(q, k_ref[0, 0, pl.ds(start, width), :], (((1,), (1,)), ((), ())),
                               preferred_element_type=F32)

    def softmax_step(slot, start, width, first):
        if not first:
            m_all = m_ref[...]
        m_parts, a_parts = [], []
        for r0 in range(0, A_TQ, A_RS):
            rows = slice(r0, r0 + A_RS)
            s = s_ref[slot, rows, 0:width]
            s_max = jnp.max(s, axis=-1, keepdims=True)
            if first:
                m_new = s_max
            else:
                m_prev = m_all[rows]
                m_new = jnp.maximum(m_prev, s_max)
                a_parts.append(jnp.exp2(m_prev - m_new))
            m_parts.append(m_new)
            p_ref[rows, 0:width] = jnp.exp2(s - m_new).astype(BF16)
        m_ref[...] = jnp.concatenate(m_parts, axis=0)
        pv = jnp.dot(p_ref[:, 0:width], v_ref[0, 0, pl.ds(start, width), :], preferred_element_type=F32)
        acc_ref[...] = pv if first else jnp.concatenate(a_parts, axis=0) * acc_ref[...] + pv

    def kstart(c):
        return pl.multiple_of(CTX + c * tk, 256)

    s_ref[0, :, 0:CTX] = scores(0, CTX)

    @pl.when(qi == 0)
    def _():
        softmax_step(0, 0, CTX, True)

    @pl.when(qi > 0)
    def _():
        s_ref[1] = scores(CTX, tk)
        softmax_step(0, 0, CTX, True)

        def body(c2, carry):
            c = 2 * c2
            s_ref[0] = scores(kstart(c + 1), tk)
            softmax_step(1, kstart(c), tk, False)
            s_ref[1] = scores(kstart(c + 2), tk)
            softmax_step(0, kstart(c + 1), tk, False)
            return carry

        lax.fori_loop(0, n_lat_chunks // 2 - 1, body, 0)
        c = n_lat_chunks - 2
        s_ref[0] = scores(kstart(c + 1), tk)
        softmax_step(1, kstart(c), tk, False)
        softmax_step(0, kstart(c + 1), tk, False)

    acc = acc_ref[...]
    o_ref[0] = acc[:, 0:A_V] / acc[:, A_V:A_V + 1]


def _attention(q, k, v):
    batch, heads, t_len, _ = q.shape
    tk = min(A_TK, (t_len - CTX) // 2)
    n_lat_chunks = (t_len - CTX) // tk
    assert n_lat_chunks % 2 == 0 and n_lat_chunks * tk == t_len - CTX and tk % 256 == 0
    return pl.pallas_call(
        functools.partial(_attn_kernel, n_lat_chunks=n_lat_chunks, tk=tk),
        out_shape=jax.ShapeDtypeStruct((batch, t_len, heads * A_V), F32),
        grid=(batch, heads, t_len // A_TQ),
        in_specs=[
            pl.BlockSpec((1, 1, A_TQ, A_HD), lambda b, h, i: (b, h, i, 0)),
            pl.BlockSpec((1, 1, t_len, A_HD), lambda b, h, i: (b, h, 0, 0)),
            pl.BlockSpec((1, 1, t_len, 2 * A_V), lambda b, h, i: (b, h, 0, 0)),
        ],
        out_specs=pl.BlockSpec((1, A_TQ, A_V), lambda b, h, i: (b, i, h)),
        scratch_shapes=[
            pltpu.VMEM((2, A_TQ, tk), F32),
            pltpu.VMEM((A_TQ, tk), BF16),
            pltpu.VMEM((A_TQ, 1), F32),
            pltpu.VMEM((A_TQ, 2 * A_V), F32),
        ],
        compiler_params=_cparams(("parallel", "parallel", "arbitrary")),
        name="mla_attention",
    )(q, k, v)


def _conv_kernel(x_ref, prev_ref, next_ref, w_ref, b_ref, post_ref, o_ref, *, t_len):
    i = pl.program_id(0)
    x = x_ref[...]
    row = lax.broadcasted_iota(jnp.int32, (TM, 1), 0)
    pos = (i * TM + row) % t_len
    x_prev = jnp.where(row == 0, prev_ref[7:8, :], pltpu.roll(x, 1, axis=0))
    x_next = jnp.where(row == TM - 1, next_ref[0:1, :], pltpu.roll(x, TM - 1, axis=0))
    seq_start = (pos == 0) | (pos == CTX)
    seq_end = (pos == CTX - 1) | (pos == t_len - 1)
    x_prev = jnp.where(seq_start, 0.0, x_prev)
    x_next = jnp.where(seq_end, 0.0, x_next)
    y = w_ref[0:1, :] * x_prev + w_ref[1:2, :] * x + w_ref[2:3, :] * x_next + b_ref[...]
    o_ref[...] = _silu(y) * post_ref[...]


def _mlstm_conv(p0, conv_w, conv_b, post, t_len):
    rows = p0.shape[0]
    n8 = rows // 8
    cb = P0_MQ // 1024
    return pl.pallas_call(
        functools.partial(_conv_kernel, t_len=t_len),
        out_shape=jax.ShapeDtypeStruct((rows, 1024), F32),
        grid=(rows // TM,),
        in_specs=[
            pl.BlockSpec((TM, 1024), lambda i: (i, cb)),
            pl.BlockSpec((8, 1024), lambda i: (jnp.maximum(i * (TM // 8) - 1, 0), cb)),
            pl.BlockSpec((8, 1024), lambda i: (jnp.minimum((i + 1) * (TM // 8), n8 - 1), cb)),
            _const_spec((3, 1024)),
            _const_spec((1, 1024)),
            _const_spec((1, 1024)),
        ],
        out_specs=pl.BlockSpec((TM, 1024), lambda i: (i, 0)),
        compiler_params=_cparams(("parallel",)),
        name="mlstm_conv",
    )(p0, p0, p0, conv_w, conv_b, post)


def _log_sigmoid(x):
    return jnp.minimum(x, 0.0) - jnp.log(1.0 + jnp.exp(-jnp.abs(x)))


def _mlstm_kernel(qkf_ref, qkb_ref, vf_ref, vb_ref, gf_ref, gb_ref, gbias_ref, hf_ref, hb_ref, c_ref, m_ref):
    j = pl.program_id(1)
    L = M_CHUNK

    @pl.when(j == 0)
    def _():
        c_ref[...] = jnp.zeros_like(c_ref)
        m_ref[...] = jnp.zeros_like(m_ref)

    r_i = lax.broadcasted_iota(jnp.int32, (L, L), 0)
    c_i = lax.broadcasted_iota(jnp.int32, (L, L), 1)
    row = lax.broadcasted_iota(jnp.int32, (L, 128), 0)
    lane = lax.broadcasted_iota(jnp.int32, (L, 128), 1)

    for d, (qk_ref, v_ref, g_ref, h_ref) in enumerate(((qkf_ref, vf_ref, gf_ref, hf_ref),
                                                        (qkb_ref, vb_ref, gb_ref, hb_ref))):
        rev = d == 1
        keep = (c_i >= r_i) if rev else (c_i <= r_i)
        tri = jnp.where(keep, LOG2E, 0.0).astype(F32)
        pre = g_ref[...] + gbias_ref[...]
        csum = jnp.dot(tri, _log_sigmoid(pre), precision=HIGHEST, preferred_element_type=F32)
        g_run = pltpu.roll(csum, 124, axis=1)
        u = pre * LOG2E - g_run
        c_run = u
        k = 1
        while k < L:
            if rev:
                shifted = jnp.where(row >= L - k, -jnp.inf, pltpu.roll(c_run, L - k, axis=0))
            else:
                shifted = jnp.where(row < k, -jnp.inf, pltpu.roll(c_run, k, axis=0))
            c_run = jnp.maximum(c_run, shifted)
            k *= 2
        last = 0 if rev else L - 1
        m_prev = m_ref[d]
        m_run = jnp.maximum(c_run, m_prev)
        g_end = g_run[last:last + 1, :]
        m_new = g_end + jnp.maximum(m_prev, c_run[last:last + 1, :])
        w_inter_all = jnp.exp2(m_prev - m_run)
        floor_all = jnp.exp2(-(g_run + m_run))
        w_in_all = jnp.exp2(g_end + u - m_new)
        decay_all = jnp.exp2(g_end + m_prev - m_new)
        u_t = u.T
        m_ref[d] = m_new
        for h in range(M_HEADS):
            ci = 8 * d + h
            qh = qk_ref[:, h * 128:(h + 1) * 128].astype(BF16)
            kh32 = qk_ref[:, 512 + h * 128:512 + (h + 1) * 128]
            sc = lax.dot_general(qh, kh32.astype(BF16), (((1,), (1,)), ((), ())), preferred_element_type=F32)
            s = sc * jnp.exp2(jnp.where(keep, u_t[ci:ci + 1, :] - m_run[:, ci:ci + 1], -jnp.inf))
            ones_col = jnp.where(lane == ci, 1.0, 0.0).astype(BF16)
            v_ext = jnp.concatenate([v_ref[:, h * 128:(h + 1) * 128].astype(BF16), ones_col], axis=1)
            c_st = c_ref[4 * d + h]
            r1 = jnp.dot(s.astype(BF16), v_ext, preferred_element_type=F32)
            r2 = jnp.dot(qh, c_st.astype(BF16), preferred_element_type=F32)
            den = r1[:, 128:256] + w_inter_all * r2[:, 128:256]
            inv = 1.0 / jnp.maximum(jnp.abs(den), floor_all)
            h_ref[:, h * 128:(h + 1) * 128] = ((r1[:, 0:128] + w_inter_all[:, ci:ci + 1] * r2[:, 0:128])
                                               * inv[:, ci:ci + 1])
            kw = (kh32 * w_in_all[:, ci:ci + 1]).astype(BF16)
            upd = lax.dot_general(kw, v_ext, (((0,), (0,)), ((), ())), preferred_element_type=F32)
            c_ref[4 * d + h] = decay_all[:, ci:ci + 1] * c_st + upd


def _mlstm_scan(qk, p0, gate_bias, batch, t_len):
    rows = qk.shape[0]
    nch = t_len // M_CHUNK
    nctx = CTX // M_CHUNK
    fwd = lambda b, j: b * nch + j
    bwd = lambda b, j: b * nch + jnp.where(j < nctx, nctx - 1 - j, nch + nctx - 1 - j)
    out = jax.ShapeDtypeStruct((rows, M_HEADS * M_V), F32)
    return pl.pallas_call(
        _mlstm_kernel,
        out_shape=(out, out),
        grid=(batch, nch),
        in_specs=[
            pl.BlockSpec((M_CHUNK, 1024), lambda b, j: (fwd(b, j), 0)),
            pl.BlockSpec((M_CHUNK, 1024), lambda b, j: (bwd(b, j), 0)),
            pl.BlockSpec((M_CHUNK, 512), lambda b, j: (fwd(b, j), P0_MV // 512)),
            pl.BlockSpec((M_CHUNK, 512), lambda b, j: (bwd(b, j), P0_MV // 512)),
            pl.BlockSpec((M_CHUNK, 128), lambda b, j: (fwd(b, j), P0_GATE // 128)),
            pl.BlockSpec((M_CHUNK, 128), lambda b, j: (bwd(b, j), P0_GATE // 128)),
            _const_spec((1, 128)),
        ],
        out_specs=(
            pl.BlockSpec((M_CHUNK, 512), lambda b, j: (fwd(b, j), 0)),
            pl.BlockSpec((M_CHUNK, 512), lambda b, j: (bwd(b, j), 0)),
        ),
        scratch_shapes=[
            pltpu.VMEM((2 * M_HEADS, M_QKP, 256), F32),
            pltpu.VMEM((2, 1, 128), F32),
        ],
        compiler_params=_cparams(("parallel", "arbitrary")),
        name="mlstm_scan",
    )(qk, qk, p0, p0, p0, p0, gate_bias)


def _head_rms(x, gain_row, n_heads, width):
    parts = []
    for h in range(n_heads):
        sl = slice(h * width, (h + 1) * width)
        parts.append(_rms(x[:, sl], gain_row[:, sl]))
    return jnp.concatenate(parts, axis=1)


def _ab_out_kernel(x_ref, a_ref, hf_ref, hb_ref, og_ref, hg_ref, wa_ref, wm_ref, gains_ref, tab_ref, o_ref, *, per):
    i = pl.program_id(0)
    m = _head_rms(hf_ref[...] + hb_ref[...], hg_ref[...], M_HEADS, M_V) * _sigmoid(og_ref[...])
    y = (jnp.dot(a_ref[...].astype(BF16), wa_ref[...], preferred_element_type=F32)
         + jnp.dot(m.astype(BF16), wm_ref[...], preferred_element_type=F32))
    _gated_residual_tile(x_ref, y, gains_ref[1:2, :], tab_ref, 2, i, per, TM, o_ref)


def _ab_out(xs, a, hf, hb, p0, head_gain, wa, wm, gains, tab, per):
    rows = xs.shape[0]
    row_spec = lambda w, cb=0: pl.BlockSpec((TM, w), lambda i: (i, cb))
    return pl.pallas_call(
        functools.partial(_ab_out_kernel, per=per),
        out_shape=jax.ShapeDtypeStruct((rows, D), F32),
        grid=(rows // TM,),
        in_specs=[
            row_spec(D), row_spec(512), row_spec(512), row_spec(512), row_spec(512, P0_OG // 512),
            _const_spec((1, 512)), _const_spec((512, D)), _const_spec((512, D)),
            _const_spec((4, D)), _const_spec((6, 8, D)),
        ],
        out_specs=row_spec(D),
        compiler_params=_cparams(("parallel",)),
        name="ab_out",
    )(xs, a, hf, hb, p0, head_gain, wa, wm, gains, tab)


def _swiglu(h, wgu_ref, wdn_ref, lead):
    acc = None
    for c0 in range(0, D_FF, FF_CHUNK):
        g = jnp.dot(h, wgu_ref[lead + (slice(None), slice(c0, c0 + FF_CHUNK))], preferred_element_type=F32)
        u = jnp.dot(h, wgu_ref[lead + (slice(None), slice(D_FF + c0, D_FF + c0 + FF_CHUNK))],
                    preferred_element_type=F32)
        a = (_silu(g) * u).astype(BF16)
        part = jnp.dot(a, wdn_ref[lead + (slice(c0, c0 + FF_CHUNK), slice(None))], preferred_element_type=F32)
        acc = part if acc is None else acc + part
    return acc


def _ffn_kernel(x_ref, gains_ref, tab_ref, wgu_ref, wdn_ref, o_ref, *, per):
    i = pl.program_id(0)
    h = _norm_mod_tile(x_ref, gains_ref[2:3, :], tab_ref, 3, 4, i, per, TM).astype(BF16)
    y = _swiglu(h, wgu_ref, wdn_ref, ())
    _gated_residual_tile(x_ref, y, gains_ref[3:4, :], tab_ref, 5, i, per, TM, o_ref)


def _ffn(xs, gains, tab, wgu, wdn, per):
    rows = xs.shape[0]
    return pl.pallas_call(
        functools.partial(_ffn_kernel, per=per),
        out_shape=jax.ShapeDtypeStruct((rows, D), F32),
        grid=(rows // TM,),
        in_specs=[
            pl.BlockSpec((TM, D), lambda i: (i, 0)),
            _const_spec((4, D)), _const_spec((6, 8, D)),
            _const_spec((D, 2 * D_FF)), _const_spec((D_FF, D)),
        ],
        out_specs=pl.BlockSpec((TM, D), lambda i: (i, 0)),
        compiler_params=_cparams(("parallel",)),
        name="ffn",
    )(xs, gains, tab, wgu, wdn)


def _hgrn_stream(q, v, g_ref, w_ref, sl, st, gsum, neg, lmask, rev):
    c = G_SUB
    ns = G_CHUNK // c
    blk = lambda ref, i: ref[c * i:c * (i + 1), sl]
    bcast = lambda ref, r: jnp.broadcast_to(ref[r:r + 1, sl], (c, 128))
    qb = lambda i: q[c * i:c * (i + 1)]
    bound = [bcast(g_ref, c * j if rev else c * j + c - 1) for j in range(ns)]
    khat = jnp.concatenate([jnp.exp2(bound[j] - blk(w_ref, j)) for j in range(ns)], axis=0)
    pairs = [(i, j) for i in range(ns) for j in range(ns) if (i < j if rev else i > j)]
    qst = jnp.concatenate([qb(i) * jnp.exp2(blk(g_ref, i) - bound[j]) for (i, j) in pairs], axis=0)
    cross = lax.dot_general(qst.astype(BF16), khat.astype(BF16), (((1,), (1,)), ((), ())),
                            preferred_element_type=F32)
    z_rows = []
    for i in range(ns):
        gi, qi = blk(g_ref, i), qb(i)
        z_rows.append(jnp.concatenate(
            [qi * jnp.exp2(gi + neg[s] - bcast(w_ref, c * i + s)) for s in range(c)], axis=1))
    diag = jnp.dot(jnp.concatenate(z_rows, axis=0).astype(BF16), gsum, preferred_element_type=F32)
    a_rows = []
    for i in range(ns):
        acc = diag[c * i:c * (i + 1)] * lmask[i]
        for p, (ii, j) in enumerate(pairs):
            if ii == i:
                acc = acc + cross[c * p:c * (p + 1)] * lmask[j]
        a_rows.append(acc)
    a = jnp.concatenate(a_rows, axis=0).astype(BF16)
    g_end = g_ref[0:1, sl] if rev else g_ref[G_CHUNK - 1:G_CHUNK, sl]
    o = (jnp.dot(a, v.astype(BF16), preferred_element_type=F32)
         + lax.dot_general((q * jnp.exp2(g_ref[:, sl])).astype(BF16), st.astype(BF16), (((1,), (1,)), ((), ())),
                           preferred_element_type=F32))
    kw = jnp.exp2(g_end - w_ref[:, sl]).astype(BF16)
    st_new = st * jnp.exp2(g_end) + lax.dot_general(v.astype(BF16), kw, (((0,), (0,)), ((), ())),
                                                     preferred_element_type=F32)
    return o, st_new


def _hgrn_kernel(qf_ref, zf_ref, vf_ref, qb_ref, zb_ref, vb_ref, lb_ref, of_ref, ob_ref, st_ref, g_ref, w_ref):
    j = pl.program_id(1)
    L = G_CHUNK

    @pl.when(j == 0)
    def _():
        st_ref[...] = jnp.zeros_like(st_ref)

    r_i = lax.broadcasted_iota(jnp.int32, (L, L), 0)
    c_i = lax.broadcasted_iota(jnp.int32, (L, L), 1)
    lb = lb_ref[...]
    log_1mlb = jnp.log(1.0 - lb)
    gs_r = lax.broadcasted_iota(jnp.int32, (G_SUB * 128, L), 0)
    gs_c = lax.broadcasted_iota(jnp.int32, (G_SUB * 128, L), 1)
    gsum = jnp.where(gs_r // 128 == gs_c % G_SUB, 1.0, 0.0).astype(BF16)
    sub = lax.broadcasted_iota(jnp.int32, (G_SUB, 128), 0)
    lane = lax.broadcasted_iota(jnp.int32, (G_SUB, L), 1)
    lmask = [jnp.where(lane // G_SUB == jj, 1.0, 0.0) for jj in range(L // G_SUB)]
    for d, (q_ref, z_ref, v_ref, o_ref) in enumerate(((qf_ref, zf_ref, vf_ref, of_ref),
                                                       (qb_ref, zb_ref, vb_ref, ob_ref))):
        rev = d == 1
        neg = [jnp.where((sub <= s) if rev else (sub >= s), 0.0, -jnp.inf) for s in range(G_SUB)]
        tri = jnp.where((c_i >= r_i) if rev else (c_i <= r_i), LOG2E, 0.0).astype(F32)
        z = z_ref[...]
        e = jnp.exp(-jnp.abs(z))
        r = 1.0 / (1.0 + e)
        sig_pos = jnp.where(z >= 0, r, e * r)
        log_f = jnp.log(lb + (1.0 - lb) * sig_pos)
        log_k = log_1mlb - jnp.maximum(z, 0.0) - jnp.log(1.0 + e)
        g_all = jnp.dot(tri, log_f, precision=HIGHEST, preferred_element_type=F32)
        g_ref[d] = g_all
        w_ref[d] = g_all - LOG2E * log_k
        q_all = _silu(q_ref[...])
        for h in range(G_HEADS):
            sl = slice(h * 128, (h + 1) * 128)
            o, st_new = _hgrn_stream(q_all[:, sl], v_ref[:, sl], g_ref.at[d], w_ref.at[d], sl, st_ref[d * G_HEADS + h],
                                     gsum, neg, lmask, rev)
            o_ref[:, sl] = o
            st_ref[d * G_HEADS + h] = st_new


def _hgrn_scan(p1, lb_row, batch, t_len):
    rows = p1.shape[0]
    nch = t_len // G_CHUNK
    nctx = CTX // G_CHUNK
    fwd = lambda b, j: b * nch + j
    bwd = lambda b, j: b * nch + jnp.where(j < nctx, nctx - 1 - j, nch + nctx - 1 - j)
    spec = lambda order, cb: pl.BlockSpec((G_CHUNK, 1024), lambda b, j: (order(b, j), cb))
    out = jax.ShapeDtypeStruct((rows, G_HEADS * G_DV), F32)
    return pl.pallas_call(
        _hgrn_kernel,
        out_shape=(out, out),
        grid=(batch, nch),
        in_specs=[spec(fwd, 0), spec(fwd, 1), spec(fwd, 3), spec(bwd, 0), spec(bwd, 2), spec(bwd, 3),
                  _const_spec((1, 1024))],
        out_specs=(spec(fwd, 0), spec(bwd, 0)),
        scratch_shapes=[pltpu.VMEM((2 * G_HEADS, G_DV, G_DK), F32),
                        pltpu.VMEM((2, G_CHUNK, 1024), F32),
                        pltpu.VMEM((2, G_CHUNK, 1024), F32)],
        compiler_params=_cparams(("parallel", "arbitrary")),
        name="hgrn_scan",
    )(p1, p1, p1, p1, p1, p1, lb_row)


def _c_out_kernel(x_ref, of_ref, ob_ref, hg_ref, gain_ref, w_ref, gains_ref, tab_ref, o_ref, *, per):
    i = pl.program_id(0)
    o = _head_rms(of_ref[...] + ob_ref[...], gain_ref[...], G_HEADS, G_DV) * _silu(hg_ref[...])
    y = jnp.dot(o.astype(BF16), w_ref[...], preferred_element_type=F32)
    _gated_residual_tile(x_ref, y, gains_ref[1:2, :], tab_ref, 2, i, per, TM, o_ref)


def _c_out(xs, of, ob, p1, head_gain, w, gains, tab, per):
    rows = xs.shape[0]
    row_spec = lambda cb=0: pl.BlockSpec((TM, D), lambda i: (i, cb))
    return pl.pallas_call(
        functools.partial(_c_out_kernel, per=per),
        out_shape=jax.ShapeDtypeStruct((rows, D), F32),
        grid=(rows // TM,),
        in_specs=[row_spec(), row_spec(), row_spec(), row_spec(4), _const_spec((1, D)), _const_spec((D, D)),
                  _const_spec((4, D)), _const_spec((6, 8, D))],
        out_specs=row_spec(),
        compiler_params=_cparams(("parallel",)),
        name="c_out",
    )(xs, of, ob, p1, head_gain, w, gains, tab)


def _router_kernel(x_ref, gains_ref, tab_ref, wr_ref, h_ref, r_ref, *, per):
    i = pl.program_id(0)
    h = _norm_mod_tile(x_ref, gains_ref[2:3, :], tab_ref, 3, 4, i, per, TM)
    for j in range(D // 128):
        h_ref[pl.ds(j, TM, stride=D // 128), :] = h[:, j * 128:(j + 1) * 128]
    logits = lax.dot_general(wr_ref[...], h, (((1,), (1,)), ((), ())), precision=HIGHEST,
                             preferred_element_type=F32)
    eid = lax.broadcasted_iota(jnp.int32, logits.shape, 0).astype(F32)
    m1 = jnp.max(logits, axis=0, keepdims=True)
    i1 = jnp.min(jnp.where(logits == m1, eid, float(N_EXP)), axis=0, keepdims=True)
    rest = jnp.where(eid == i1, -jnp.inf, logits)
    m2 = jnp.max(rest, axis=0, keepdims=True)
    i2 = jnp.min(jnp.where(rest == m2, eid, float(N_EXP)), axis=0, keepdims=True)
    e2 = jnp.exp(m2 - m1)
    g1 = 1.0 / (1.0 + e2)
    g2 = e2 * g1
    rid = lax.broadcasted_iota(jnp.int32, (128, TM), 0)
    packed = jnp.where(rid == 0, i1, jnp.where(rid == 1, i2, jnp.where(rid == 2, g1, jnp.where(rid == 3, g2, 0.0))))
    r_ref[...] = packed.T


def _router(xs, gains, tab, wr_t, per):
    rows = xs.shape[0]
    return pl.pallas_call(
        functools.partial(_router_kernel, per=per),
        out_shape=(jax.ShapeDtypeStruct((rows * (D // 128), 128), F32), jax.ShapeDtypeStruct((rows, 128), F32)),
        grid=(rows // TM,),
        in_specs=[pl.BlockSpec((TM, D), lambda i: (i, 0)), _const_spec((4, D)), _const_spec((6, 8, D)),
                  _const_spec((N_EXP, D))],
        out_specs=(pl.BlockSpec((TM * (D // 128), 128), lambda i: (i, 0)), pl.BlockSpec((TM, 128), lambda i: (i, 0))),
        compiler_params=_cparams(("parallel",)),
        name="moe_router",
    )(xs, gains, tab, wr_t)


def _tile_rows(ref, index):
    return jnp.concatenate([ref[index(j)] for j in range(D // 128)], axis=1)


def _expert_kernel(be_ref, nact_ref, tok0_ref, nxt_ref, dst_ref, dstl_ref, h_hbm, wgu_ref, wdn_ref, yg_hbm,
                   xbuf, ybuf, gsem, ssem):
    b = pl.program_id(0)
    nb = pl.num_programs(0)
    slot = b % 2
    other = 1 - slot

    def gather_rows(idx_ref, dst_slot, r0, r1):
        for r in range(r0, r1):
            src = h_hbm.at[pl.ds(pl.multiple_of(idx_ref[0, 0, r], 8), 8)]
            pltpu.make_async_copy(src, xbuf.at[dst_slot, pl.ds(8 * r, 8)], gsem.at[dst_slot]).start()

    def scatter_rows(idx_ref, src_slot, r0, r1):
        for r in range(r0, r1):
            dst = yg_hbm.at[pl.ds(pl.multiple_of(idx_ref[0, 0, r], 8), 8)]
            pltpu.make_async_copy(ybuf.at[src_slot, pl.ds(8 * r, 8)], dst, ssem.at[src_slot]).start()

    def wait_gather(s):
        pltpu.make_async_copy(h_hbm.at[pl.ds(0, MOE_BLOCK * (D // 128))], xbuf.at[s], gsem.at[s]).wait()

    def wait_scatter(s):
        pltpu.make_async_copy(ybuf.at[s], yg_hbm.at[pl.ds(0, MOE_BLOCK * (D // 128))], ssem.at[s]).wait()

    @pl.when(b == 0)
    def _():
        ybuf[...] = jnp.zeros_like(ybuf)
        gather_rows(tok0_ref, 0, 0, MOE_BLOCK)

    wait_gather(slot)

    @pl.when(b >= 1)
    def _():
        wait_scatter(slot)

    n_chunks = D_FF // FF_CHUNK
    per = -(-MOE_BLOCK // n_chunks)

    @pl.when(b < nact_ref[0])
    def _():
        x = _tile_rows(xbuf, lambda j: (slot, pl.ds(j, MOE_BLOCK, stride=D // 128), slice(None))).astype(BF16)
        acc = None
        for ci in range(n_chunks):
            c0 = ci * FF_CHUNK
            g = jnp.dot(x, wgu_ref[0, :, c0:c0 + FF_CHUNK], preferred_element_type=F32)
            u = jnp.dot(x, wgu_ref[0, :, D_FF + c0:D_FF + c0 + FF_CHUNK], preferred_element_type=F32)
            a = (_silu(g) * u).astype(BF16)
            part = jnp.dot(a, wdn_ref[0, c0:c0 + FF_CHUNK, :], preferred_element_type=F32)
            acc = part if acc is None else acc + part
            r0, r1 = ci * per, min((ci + 1) * per, MOE_BLOCK)
            gather_rows(nxt_ref, other, r0, r1)
            scatter_rows(dst_ref, other, r0, r1)
        for j in range(D // 128):
            ybuf[slot, pl.ds(j, MOE_BLOCK, stride=D // 128), :] = acc[:, j * 128:(j + 1) * 128]

    @pl.when(b >= nact_ref[0])
    def _():
        gather_rows(nxt_ref, other, 0, MOE_BLOCK)
        scatter_rows(dst_ref, other, 0, MOE_BLOCK)

    @pl.when(b == nb - 1)
    def _():
        wait_gather(other)
        wait_scatter(other)
        scatter_rows(dstl_ref, slot, 0, MOE_BLOCK)
        wait_scatter(slot)


def _experts(block_expert, n_active, tok3, dst3, h3, wgu, wdn, n_rows_out):
    n_blocks = block_expert.shape[0]
    smem = lambda f: pl.BlockSpec((1, 1, MOE_BLOCK), f, memory_space=pltpu.SMEM)
    grid_spec = pltpu.PrefetchScalarGridSpec(
        num_scalar_prefetch=2,
        grid=(n_blocks,),
        in_specs=[
            smem(lambda b, be, na: (0, 0, 0)),
            smem(lambda b, be, na: (jnp.minimum(b + 1, n_blocks - 1), 0, 0)),
            smem(lambda b, be, na: (b, 0, 0)),
            smem(lambda b, be, na: (n_blocks, 0, 0)),
            pl.BlockSpec(memory_space=pl.ANY),
            pl.BlockSpec((1, D, 2 * D_FF), lambda b, be, na: (be[b], 0, 0)),
            pl.BlockSpec((1, D_FF, D), lambda b, be, na: (be[b], 0, 0)),
        ],
        out_specs=pl.BlockSpec(memory_space=pl.ANY),
        scratch_shapes=[pltpu.VMEM((2, MOE_BLOCK * (D // 128), 128), F32), pltpu.VMEM((2, MOE_BLOCK * (D // 128), 128), F32),
                        pltpu.SemaphoreType.DMA((2,)), pltpu.SemaphoreType.DMA((2,))],
    )
    return pl.pallas_call(
        _expert_kernel,
        out_shape=jax.ShapeDtypeStruct((n_rows_out * (D // 128), 128), F32),
        grid_spec=grid_spec,
        compiler_params=_cparams(("arbitrary",)),
        name="moe_experts",
    )(block_expert, n_active, tok3, tok3, dst3, dst3, h3, wgu, wdn)


def _combine_kernel(yg_ref, r_ref, x_ref, gains_ref, tab_ref, o_ref, *, per):
    tile = pl.program_id(0) * per + 1 + pl.program_id(1)
    gates = r_ref[...]
    rows = lambda k: _tile_rows(yg_ref, lambda j: (pl.ds(k * (D // 128) + j, HALF, stride=2 * (D // 128)), slice(None)))
    y = gates[:, 2:3] * rows(0) + gates[:, 3:4] * rows(1)
    idx = _mod_index(tile, per)
    gate = tab_ref[5, pl.ds(idx, 1), :]
    o_ref[...] = x_ref[...] + gate * _rms(y, gains_ref[3:4, :])


def _combine_latents(yg, rout, xs, gains, tab, batch, s_len, per):
    nl = s_len // HALF
    tile = lambda b, j: b * per + 1 + j
    return pl.pallas_call(
        functools.partial(_combine_kernel, per=per),
        out_shape=jax.ShapeDtypeStruct((batch * s_len, D), F32),
        grid=(batch, nl),
        in_specs=[
            pl.BlockSpec((HALF * 2 * (D // 128), 128), lambda b, j: (tile(b, j), 0)),
            pl.BlockSpec((HALF, 128), lambda b, j: (tile(b, j), 0)),
            pl.BlockSpec((HALF, D), lambda b, j: (tile(b, j), 0)),
            _const_spec((4, D)), _const_spec((6, 8, D)),
        ],
        out_specs=pl.BlockSpec((HALF, D), lambda b, j: (b * nl + j, 0)),
        compiler_params=_cparams(("parallel", "parallel")),
        name="moe_combine",
    )(yg, rout, xs, gains, tab)


def _moe_plan(rout):
    n_tok = rout.shape[0]
    n_flat = 2 * n_tok
    flat_e = rout[:, 0:2].astype(jnp.int32).reshape(-1)
    onehot = (flat_e[:, None] == jnp.arange(N_EXP, dtype=jnp.int32)[None, :]).astype(jnp.int32)
    counts = jnp.sum(onehot, axis=0)
    padded = (counts + MOE_BLOCK - 1) // MOE_BLOCK * MOE_BLOCK
    pend = jnp.cumsum(padded)
    pstart = pend - padded
    start = jnp.cumsum(counts) - counts
    n_blocks = -(-n_flat // MOE_BLOCK) + N_EXP
    blk = jnp.arange(n_blocks, dtype=jnp.int32)
    block_expert = jnp.minimum(jnp.searchsorted(pend, blk * MOE_BLOCK, side='right'), N_EXP - 1).astype(jnp.int32)
    n_active = (pend[-1] // MOE_BLOCK).astype(jnp.int32).reshape(1)
    order = jnp.argsort(flat_e, stable=True).astype(jnp.int32)
    base = start[block_expert] + blk * MOE_BLOCK - pstart[block_expert]
    n_valid = jnp.clip(start[block_expert] + counts[block_expert] - base, 0, MOE_BLOCK)
    order_pad = jnp.concatenate([order, jnp.zeros((MOE_BLOCK,), jnp.int32)])
    rows = jax.vmap(lambda s: lax.dynamic_slice(order_pad, (s,), (MOE_BLOCK,)))(jnp.clip(base, 0, n_flat))
    lane = jnp.arange(MOE_BLOCK, dtype=jnp.int32)[None, :]
    valid = lane < n_valid[:, None]
    tok3 = (jnp.where(valid, rows // 2, 0) * (D // 128)).reshape(n_blocks, 1, MOE_BLOCK)
    dummy = n_flat + lane
    dst = jnp.where(valid, rows, dummy)
    dst3 = (jnp.concatenate([dummy, dst], axis=0) * (D // 128)).reshape(n_blocks + 1, 1, MOE_BLOCK)
    return block_expert, n_active, tok3, dst3, n_flat + MOE_BLOCK


def _rot_half_cols(w):
    half = A_ROPE // 2
    g = w.reshape(w.shape[0], -1, A_ROPE)
    return jnp.concatenate([-g[..., half:], g[..., :half]], axis=-1).reshape(w.shape)


def _pad_groups(w, width, to):
    k = w.shape[0]
    g = w.reshape(k, -1, width)
    return jnp.pad(g, ((0, 0), (0, 0), (0, to - width))).reshape(k, -1)


def _layer0_inproj_weight(w_in):
    splits = np.cumsum([A_QLORA, A_KVLORA, A_ROPE, M_HEADS * M_QK, M_HEADS * M_QK, M_HEADS * M_V, M_HEADS * M_V])
    cq, ckv, kr, mq, mk, mv, og, gt = jnp.split(w_in, [int(s) for s in splits], axis=1)
    pad64 = lambda a: jnp.pad(a, ((0, 0), (0, 64)))
    cols = [cq, ckv, pad64(kr), mv, _pad_groups(mq, M_QK, M_QKP), _pad_groups(mk, M_QK, M_QKP), og,
            jnp.pad(gt, ((0, 0), (0, 128 - gt.shape[1]))), pad64(_rot_half_cols(kr))]
    w = jnp.concatenate(cols, axis=1)
    assert w.shape[1] == P0_N
    return w.astype(BF16)


def _rope_tables(batch, s_len):
    t = jnp.arange(s_len)
    row = (t // GRID_W).astype(F32)
    col = (t % GRID_W).astype(F32)
    n_freq = A_ROPE // 4
    inv_freq = ROPE_THETA ** (-jnp.arange(n_freq, dtype=F32) / n_freq)
    ang = jnp.concatenate([row[:, None] * inv_freq, col[:, None] * inv_freq], axis=-1)
    ang = jnp.concatenate([ang, ang], axis=-1)

    def table(vals, fill):
        lat = jnp.pad(vals, ((0, 0), (0, 128 - A_ROPE)), constant_values=fill)
        one = jnp.concatenate([jnp.full((CTX, 128), fill, F32), lat], axis=0)
        return jnp.tile(one, (batch, 1))

    return table(jnp.cos(ang), 1.0), table(jnp.sin(ang), 0.0)


def kernel(x, c, ctx, c_ctx, w_mod, b_mod, norm_gains, w_in_ab, mla_q_gain, mla_w_uq, mla_kv_gain, mla_w_ukv,
           ml_conv_w, ml_conv_b, ml_gate_b, ml_head_gain, w_out_ab, ffn_w_gu, ffn_w_dn, w_in_c, hg_lb_logits,
           hg_head_gain, w_out_c, moe_router, moe_w_gu, moe_w_dn):
    batch, s_len, d = x.shape
    assert d == D and ctx.shape[1] == CTX and batch == 2
    assert s_len % 512 == 0 and (batch * (CTX + s_len)) % TM == 0
    t_len = CTX + s_len
    per = t_len // HALF

    xs = jnp.concatenate([ctx, x], axis=1).reshape(batch * t_len, D)
    cc = jnp.concatenate([c, c_ctx[None, :], jnp.zeros((8 - batch - 1, D), F32)], axis=0)
    tabs = _mod_table(cc, w_mod, b_mod)

    gains = norm_gains[0]
    tab = tabs[0]
    p0 = _inproj(xs, gains, tab, _layer0_inproj_weight(w_in_ab[0]), per, P0_N // 2)

    cos, sin = _rope_tables(batch, s_len)
    w_uq = mla_w_uq[0].reshape(A_QLORA, A_HEADS, A_NOPE + A_ROPE)
    wq_nope = w_uq[:, :, :A_NOPE].reshape(A_QLORA, A_HEADS * A_NOPE)
    wq_rope = w_uq[:, :, A_NOPE:].reshape(A_QLORA, A_HEADS * A_ROPE)
    w_ukv = mla_w_ukv[0].reshape(A_KVLORA, A_HEADS, A_NOPE + A_V)
    q, k, v = _mla_prep(
        p0, cos, sin, mla_q_gain[0][None, :], mla_kv_gain[0][None, :],
        wq_nope.astype(BF16),
        _pad_groups(wq_rope, A_ROPE, 128).astype(BF16),
        _pad_groups(_rot_half_cols(wq_rope), A_ROPE, 128).astype(BF16),
        w_ukv[:, :, :A_NOPE].reshape(A_KVLORA, -1).astype(BF16),
        w_ukv[:, :, A_NOPE:].reshape(A_KVLORA, -1).astype(BF16),
        batch, t_len)
    attn = _attention(q, k, v).reshape(batch * t_len, A_HEADS * A_V)

    conv_w = jnp.concatenate([_pad_groups(ml_conv_w[0][:, :M_HEADS * M_QK], M_QK, M_QKP),
                              _pad_groups(ml_conv_w[0][:, M_HEADS * M_QK:], M_QK, M_QKP)], axis=1)
    conv_b = jnp.concatenate([_pad_groups(ml_conv_b[0][None, :M_HEADS * M_QK], M_QK, M_QKP),
                              _pad_groups(ml_conv_b[0][None, M_HEADS * M_QK:], M_QK, M_QKP)], axis=1)
    post = jnp.concatenate([jnp.ones((1, 512), F32), jnp.full((1, 512), M_QK ** -0.5, F32)], axis=1)
    qk = _mlstm_conv(p0, conv_w, conv_b, post, t_len)
    gate_bias = jnp.pad(ml_gate_b[0][None, :], ((0, 0), (0, 128 - 4 * M_HEADS)))
    hf, hb = _mlstm_scan(qk, p0, gate_bias, batch, t_len)

    w_out = w_out_ab[0].astype(BF16)
    xs = _ab_out(xs, attn, hf, hb, p0, ml_head_gain[0][None, :], w_out[:A_HEADS * A_V], w_out[A_HEADS * A_V:],
                 gains, tab, per)
    xs = _ffn(xs, gains, tab, ffn_w_gu[0].astype(BF16), ffn_w_dn[0].astype(BF16), per)

    gains = norm_gains[1]
    tab = tabs[1]
    lbs = jnp.cumsum(jax.nn.softmax(hg_lb_logits.astype(F32), axis=0), axis=0)
    lb_row = (lbs - lbs[0])[1][None, :]
    p1 = _inproj(xs, gains, tab, w_in_c[0].astype(BF16), per, 1024)
    of, ob = _hgrn_scan(p1, lb_row, batch, t_len)
    xs = _c_out(xs, of, ob, p1, hg_head_gain[0][None, :], w_out_c[0].astype(BF16), gains, tab, per)

    h, rout = _router(xs, gains, tab, moe_router[0].T, per)
    block_expert, n_active, tok3, dst3, n_rows_out = _moe_plan(rout)
    yg = _experts(block_expert, n_active, tok3, dst3, h, moe_w_gu[0].astype(BF16), moe_w_dn[0].astype(BF16), n_rows_out)
    out = _combine_latents(yg, rout, xs, gains, tab, batch, s_len, per)
    return out.reshape(batch, s_len, D)
```

```python
import functools

import jax
import jax.numpy as jnp
import numpy as np
from jax import lax
from jax.experimental import pallas as pl
from jax.experimental.pallas import tpu as pltpu

F32 = jnp.float32
BF16 = jnp.bfloat16
HIGHEST = lax.Precision.HIGHEST
LOG2E = 1.4426950408889634

D = 1024
CTX = 256
EPS = 1e-6
HALF = 256
TM = 512
VMEM_LIMIT = 56 * 1024 * 1024

A_HEADS = 4
A_QLORA = 256
A_KVLORA = 128
A_NOPE = 128
A_ROPE = 64
A_V = 128
A_HD = 256
A_SCALE = (A_NOPE + A_ROPE) ** -0.5
ROPE_THETA = 10000.0
GRID_W = 64
A_QSCALE = A_SCALE * LOG2E
A_TQ = 256
A_TK = 4096
A_RS = 32

M_HEADS = 4
M_QK = 64
M_V = 128
M_CHUNK = 128
M_QKP = 128

G_HEADS = 8
G_DK = 128
G_DV = 128
G_CHUNK = 64
G_SUB = 8

D_FF = 2816
N_EXP = 8
MOE_BLOCK = 512
FF_CHUNK = 256

P0_CQ = 0
P0_CKV = 256
P0_KR = 384
P0_MV = 512
P0_MQ = 1024
P0_MK = 1536
P0_OG = 2048
P0_GATE = 2560
P0_KRR = 2688
P0_N = 2816


def _cparams(sem):
    return pltpu.CompilerParams(dimension_semantics=sem, vmem_limit_bytes=VMEM_LIMIT)


def _const_spec(shape):
    nd = len(shape)
    return pl.BlockSpec(shape, lambda *_: (0,) * nd, pipeline_mode=pl.Buffered(1))


def _rms(x, gain_row):
    ms = jnp.mean(x * x, axis=-1, keepdims=True)
    return x * lax.rsqrt(ms + EPS) * gain_row


def _mod_index(half_idx, per):
    return jnp.where(half_idx % per == 0, 2, half_idx // per)


def _silu(x):
    return x * _sigmoid(x)


def _sigmoid(x):
    return 1.0 / (1.0 + jnp.exp(-x))


def _norm_mod_rows(x, gain_row, tab_ref, k_shift, k_scale, idx):
    shift = tab_ref[k_shift, pl.ds(idx, 1), :]
    scale = tab_ref[k_scale, pl.ds(idx, 1), :]
    return _rms(x, gain_row) * (1.0 + scale) + shift


def _norm_mod_tile(x_ref, gain_row, tab_ref, k_shift, k_scale, tile, per, rows):
    parts = []
    for s in range(rows // HALF):
        idx = _mod_index(tile * (rows // HALF) + s, per)
        parts.append(_norm_mod_rows(x_ref[s * HALF:(s + 1) * HALF, :], gain_row, tab_ref, k_shift, k_scale, idx))
    return parts[0] if len(parts) == 1 else jnp.concatenate(parts, axis=0)


def _gated_residual_tile(x_ref, y, gain_row, tab_ref, k_gate, tile, per, rows, o_ref):
    yn = _rms(y, gain_row)
    for s in range(rows // HALF):
        idx = _mod_index(tile * (rows // HALF) + s, per)
        gate = tab_ref[k_gate, pl.ds(idx, 1), :]
        sl = slice(s * HALF, (s + 1) * HALF)
        o_ref[sl, :] = x_ref[sl, :] + gate * yn[sl, :]


def _mod_kernel(c_ref, w_ref, b_ref, o_ref):
    a = _silu(c_ref[...])
    o_ref[0, 0] = jnp.dot(a, w_ref[0], precision=HIGHEST, preferred_element_type=F32) + b_ref[0, 0]


def _mod_table(cc, w_mod, b_mod):
    depth = w_mod.shape[0]
    return pl.pallas_call(
        _mod_kernel,
        out_shape=jax.ShapeDtypeStruct((depth, 6, 8, D), F32),
        grid=(depth, 6),
        in_specs=[
            pl.BlockSpec((8, D), lambda l, k: (0, 0)),
            pl.BlockSpec((1, D, D), lambda l, k: (l, 0, k)),
            pl.BlockSpec((1, 1, 1, D), lambda l, k: (l, k, 0, 0)),
        ],
        out_specs=pl.BlockSpec((1, 1, 8, D), lambda l, k: (l, k, 0, 0)),
        compiler_params=_cparams(("arbitrary", "arbitrary")),
        name="mod_table",
    )(cc, w_mod, b_mod.reshape(depth, 6, 1, D))


def _inproj_kernel(x_ref, gains_ref, tab_ref, w_ref, o_ref, *, per, nchunk):
    i = pl.program_id(0)
    h = _norm_mod_tile(x_ref, gains_ref[0:1, :], tab_ref, 0, 1, i, per, TM).astype(BF16)
    n = w_ref.shape[1]
    for n0 in range(0, n, nchunk):
        o_ref[:, n0:n0 + nchunk] = jnp.dot(h, w_ref[:, n0:n0 + nchunk], preferred_element_type=F32)


def _inproj(xs, gains, tab, w, per, nchunk):
    rows = xs.shape[0]
    n = w.shape[1]
    return pl.pallas_call(
        functools.partial(_inproj_kernel, per=per, nchunk=nchunk),
        out_shape=jax.ShapeDtypeStruct((rows, n), F32),
        grid=(rows // TM,),
        in_specs=[
            pl.BlockSpec((TM, D), lambda i: (i, 0)),
            _const_spec((4, D)),
            _const_spec((6, 8, D)),
            _const_spec((D, n)),
        ],
        out_specs=pl.BlockSpec((TM, n), lambda i: (i, 0)),
        compiler_params=_cparams(("parallel",)),
        name="inproj",
    )(xs, gains, tab, w)


def _mla_prep_kernel(cq_ref, ckr_ref, krr_ref, cos_ref, sin_ref, qg_ref, kvg_ref, wqn_ref, wqr_ref, wqrr_ref,
                     wk_ref, wv_ref, q_ref, k_ref, v_ref):
    qn = _rms(cq_ref[...], qg_ref[...]).astype(BF16)
    ckr = ckr_ref[...]
    kvn = _rms(ckr[:, :A_KVLORA], kvg_ref[...]).astype(BF16)
    cos = cos_ref[...]
    sin = sin_ref[...]
    q_nope = jnp.dot(qn, wqn_ref[...], preferred_element_type=F32)
    q_r = jnp.dot(qn, wqr_ref[...], preferred_element_type=F32)
    q_rr = jnp.dot(qn, wqrr_ref[...], preferred_element_type=F32)
    k_nope = jnp.dot(kvn, wk_ref[...], preferred_element_type=F32)
    vv = jnp.dot(kvn, wv_ref[...], preferred_element_type=F32)
    k_rope = (ckr[:, A_KVLORA:] * cos + krr_ref[...] * sin).astype(BF16)
    lane = lax.broadcasted_iota(jnp.int32, (HALF, A_V), 1)
    ones_col = jnp.where(lane == 0, 1.0, 0.0).astype(BF16)
    for h in range(A_HEADS):
        sl = slice(h * 128, (h + 1) * 128)
        q_rope = q_r[:, sl] * cos + q_rr[:, sl] * sin
        q_ref[0, h, :, 0:128] = (q_nope[:, sl] * A_QSCALE).astype(BF16)
        q_ref[0, h, :, 128:256] = (q_rope * A_QSCALE).astype(BF16)
        k_ref[0, h, :, 0:128] = k_nope[:, sl].astype(BF16)
        k_ref[0, h, :, 128:256] = k_rope
        v_ref[0, h, :, 0:A_V] = vv[:, sl].astype(BF16)
        v_ref[0, h, :, A_V:2 * A_V] = ones_col


def _mla_prep(p0, cos, sin, q_gain, kv_gain, wqn, wqr, wqrr, wk, wv, batch, t_len):
    rows = p0.shape[0]
    per = t_len // HALF
    qk_shape = jax.ShapeDtypeStruct((batch, A_HEADS, t_len, A_HD), BF16)
    v_shape = jax.ShapeDtypeStruct((batch, A_HEADS, t_len, 2 * A_V), BF16)
    out_map = lambda i: (i // per, 0, i % per, 0)
    return pl.pallas_call(
        _mla_prep_kernel,
        out_shape=(qk_shape, qk_shape, v_shape),
        grid=(rows // HALF,),
        in_specs=[
            pl.BlockSpec((HALF, 256), lambda i: (i, P0_CQ // 256)),
            pl.BlockSpec((HALF, 256), lambda i: (i, P0_CKV // 256)),
            pl.BlockSpec((HALF, 128), lambda i: (i, P0_KRR // 128)),
            pl.BlockSpec((HALF, 128), lambda i: (i, 0)),
            pl.BlockSpec((HALF, 128), lambda i: (i, 0)),
            _const_spec((1, A_QLORA)),
            _const_spec((1, A_KVLORA)),
            _const_spec((A_QLORA, 512)),
            _const_spec((A_QLORA, 512)),
            _const_spec((A_QLORA, 512)),
            _const_spec((A_KVLORA, 512)),
            _const_spec((A_KVLORA, 512)),
        ],
        out_specs=(
            pl.BlockSpec((1, A_HEADS, HALF, A_HD), out_map),
            pl.BlockSpec((1, A_HEADS, HALF, A_HD), out_map),
            pl.BlockSpec((1, A_HEADS, HALF, 2 * A_V), out_map),
        ),
        compiler_params=_cparams(("parallel",)),
        name="mla_prep",
    )(p0, p0, p0, cos, sin, q_gain, kv_gain, wqn, wqr, wqrr, wk, wv)


def _attn_kernel(q_ref, k_ref, v_ref, o_ref, s_ref, p_ref, m_ref, acc_ref, *, n_lat_chunks, tk):
    qi = pl.program_id(2)
    q = q_ref[0, 0]

    def scores(start, width):
        return lax.dot_general(q, k_ref[0, 0, pl.ds(start, width), :], (((1,), (1,)), ((), ())),
                               preferred_element_type=F32)

    def softmax_step(slot, start, width, first):
        if not first:
            m_all = m_ref[...]
        m_parts, a_parts = [], []
        for r0 in range(0, A_TQ, A_RS):
            rows = slice(r0, r0 + A_RS)
            s = s_ref[slot, rows, 0:width]
            s_max = jnp.max(s, axis=-1, keepdims=True)
            if first:
                m_new = s_max
            else:
                m_prev = m_all[rows]
                m_new = jnp.maximum(m_prev, s_max)
                a_parts.append(jnp.exp2(m_prev - m_new))
            m_parts.append(m_new)
            p_ref[rows, 0:width] = jnp.exp2(s - m_new).astype(BF16)
        m_ref[...] = jnp.concatenate(m_parts, axis=0)
        pv = jnp.dot(p_ref[:, 0:width], v_ref[0, 0, pl.ds(start, width), :], preferred_element_type=F32)
        acc_ref[...] = pv if first else jnp.concatenate(a_parts, axis=0) * acc_ref[...] + pv

    def kstart(c):
        return pl.multiple_of(CTX + c * tk, 256)

    s_ref[0, :, 0:CTX] = scores(0, CTX)

    @pl.when(qi == 0)
    def _():
        softmax_step(0, 0, CTX, True)

    @pl.when(qi > 0)
    def _():
        s_ref[1] = scores(CTX, tk)
        softmax_step(0, 0, CTX, True)

        def body(c2, carry):
            c = 2 * c2
            s_ref[0] = scores(kstart(c + 1), tk)
            softmax_step(1, kstart(c), tk, False)
            s_ref[1] = scores(kstart(c + 2), tk)
            softmax_step(0, kstart(c + 1), tk, False)
            return carry

        lax.fori_loop(0, n_lat_chunks // 2 - 1, body, 0)
        c = n_lat_chunks - 2
        s_ref[0] = scores(kstart(c + 1), tk)
        softmax_step(1, kstart(c), tk, False)
        softmax_step(0, kstart(c + 1), tk, False)

    acc = acc_ref[...]
    o_ref[0] = acc[:, 0:A_V] / acc[:, A_V:A_V + 1]


def _attention(q, k, v):
    batch, heads, t_len, _ = q.shape
    tk = min(A_TK, (t_len - CTX) // 2)
    n_lat_chunks = (t_len - CTX) // tk
    assert n_lat_chunks % 2 == 0 and n_lat_chunks * tk == t_len - CTX and tk % 256 == 0
    return pl.pallas_call(
        functools.partial(_attn_kernel, n_lat_chunks=n_lat_chunks, tk=tk),
        out_shape=jax.ShapeDtypeStruct((batch, t_len, heads * A_V), F32),
        grid=(batch, heads, t_len // A_TQ),
        in_specs=[
            pl.BlockSpec((1, 1, A_TQ, A_HD), lambda b, h, i: (b, h, i, 0)),
            pl.BlockSpec((1, 1, t_len, A_HD), lambda b, h, i: (b, h, 0, 0)),
            pl.BlockSpec((1, 1, t_len, 2 * A_V), lambda b, h, i: (b, h, 0, 0)),
        ],
        out_specs=pl.BlockSpec((1, A_TQ, A_V), lambda b, h, i: (b, i, h)),
        scratch_shapes=[
            pltpu.VMEM((2, A_TQ, tk), F32),
            pltpu.VMEM((A_TQ, tk), BF16),
            pltpu.VMEM((A_TQ, 1), F32),
            pltpu.VMEM((A_TQ, 2 * A_V), F32),
        ],
        compiler_params=_cparams(("parallel", "parallel", "arbitrary")),
        name="mla_attention",
    )(q, k, v)


def _conv_kernel(x_ref, prev_ref, next_ref, w_ref, b_ref, post_ref, o_ref, *, t_len):
    i = pl.program_id(0)
    x = x_ref[...]
    row = lax.broadcasted_iota(jnp.int32, (TM, 1), 0)
    pos = (i * TM + row) % t_len
    x_prev = jnp.where(row == 0, prev_ref[7:8, :], pltpu.roll(x, 1, axis=0))
    x_next = jnp.where(row == TM - 1, next_ref[0:1, :], pltpu.roll(x, TM - 1, axis=0))
    seq_start = (pos == 0) | (pos == CTX)
    seq_end = (pos == CTX - 1) | (pos == t_len - 1)
    x_prev = jnp.where(seq_start, 0.0, x_prev)
    x_next = jnp.where(seq_end, 0.0, x_next)
    y = w_ref[0:1, :] * x_prev + w_ref[1:2, :] * x + w_ref[2:3, :] * x_next + b_ref[...]
    o_ref[...] = _silu(y) * post_ref[...]


def _mlstm_conv(p0, conv_w, conv_b, post, t_len):
    rows = p0.shape[0]
    n8 = rows // 8
    cb = P0_MQ // 1024
    return pl.pallas_call(
        functools.partial(_conv_kernel, t_len=t_len),
        out_shape=jax.ShapeDtypeStruct((rows, 1024), F32),
        grid=(rows // TM,),
        in_specs=[
            pl.BlockSpec((TM, 1024), lambda i: (i, cb)),
            pl.BlockSpec((8, 1024), lambda i: (jnp.maximum(i * (TM // 8) - 1, 0), cb)),
            pl.BlockSpec((8, 1024), lambda i: (jnp.minimum((i + 1) * (TM // 8), n8 - 1), cb)),
            _const_spec((3, 1024)),
            _const_spec((1, 1024)),
            _const_spec((1, 1024)),
        ],
        out_specs=pl.BlockSpec((TM, 1024), lambda i: (i, 0)),
        compiler_params=_cparams(("parallel",)),
        name="mlstm_conv",
    )(p0, p0, p0, conv_w, conv_b, post)


def _log_sigmoid(x):
    return jnp.minimum(x, 0.0) - jnp.log(1.0 + jnp.exp(-jnp.abs(x)))


def _mlstm_kernel(qkf_ref, qkb_ref, vf_ref, vb_ref, gf_ref, gb_ref, gbias_ref, hf_ref, hb_ref, c_ref, m_ref):
    j = pl.program_id(1)
    L = M_CHUNK

    @pl.when(j == 0)
    def _():
        c_ref[...] = jnp.zeros_like(c_ref)
        m_ref[...] = jnp.zeros_like(m_ref)

    r_i = lax.broadcasted_iota(jnp.int32, (L, L), 0)
    c_i = lax.broadcasted_iota(jnp.int32, (L, L), 1)
    row = lax.broadcasted_iota(jnp.int32, (L, 128), 0)
    lane = lax.broadcasted_iota(jnp.int32, (L, 128), 1)

    for d, (qk_ref, v_ref, g_ref, h_ref) in enumerate(((qkf_ref, vf_ref, gf_ref, hf_ref),
                                                        (qkb_ref, vb_ref, gb_ref, hb_ref))):
        rev = d == 1
        keep = (c_i >= r_i) if rev else (c_i <= r_i)
        tri = jnp.where(keep, LOG2E, 0.0).astype(F32)
        pre = g_ref[...] + gbias_ref[...]
        csum = jnp.dot(tri, _log_sigmoid(pre), precision=HIGHEST, preferred_element_type=F32)
        g_run = pltpu.roll(csum, 124, axis=1)
        u = pre * LOG2E - g_run
        c_run = u
        k = 1
        while k < L:
            if rev:
                shifted = jnp.where(row >= L - k, -jnp.inf, pltpu.roll(c_run, L - k, axis=0))
            else:
                shifted = jnp.where(row < k, -jnp.inf, pltpu.roll(c_run, k, axis=0))
            c_run = jnp.maximum(c_run, shifted)
            k *= 2
        last = 0 if rev else L - 1
        m_prev = m_ref[d]
        m_run = jnp.maximum(c_run, m_prev)
        g_end = g_run[last:last + 1, :]
        m_new = g_end + jnp.maximum(m_prev, c_run[last:last + 1, :])
        w_inter_all = jnp.exp2(m_prev - m_run)
        floor_all = jnp.exp2(-(g_run + m_run))
        w_in_all = jnp.exp2(g_end + u - m_new)
        decay_all = jnp.exp2(g_end + m_prev - m_new)
        u_t = u.T
        m_ref[d] = m_new
        for h in range(M_HEADS):
            ci = 8 * d + h
            qh = qk_ref[:, h * 128:(h + 1) * 128].astype(BF16)
            kh32 = qk_ref[:, 512 + h * 128:512 + (h + 1) * 128]
            sc = lax.dot_general(qh, kh32.astype(BF16), (((1,), (1,)), ((), ())), preferred_element_type=F32)
            s = sc * jnp.exp2(jnp.where(keep, u_t[ci:ci + 1, :] - m_run[:, ci:ci + 1], -jnp.inf))
            ones_col = jnp.where(lane == ci, 1.0, 0.0).astype(BF16)
            v_ext = jnp.concatenate([v_ref[:, h * 128:(h + 1) * 128].astype(BF16), ones_col], axis=1)
            c_st = c_ref[4 * d + h]
            r1 = jnp.dot(s.astype(BF16), v_ext, preferred_element_type=F32)
            r2 = jnp.dot(qh, c_st.astype(BF16), preferred_element_type=F32)
            den = r1[:, 128:256] + w_inter_all * r2[:, 128:256]
            inv = 1.0 / jnp.maximum(jnp.abs(den), floor_all)
            h_ref[:, h * 128:(h + 1) * 128] = ((r1[:, 0:128] + w_inter_all[:, ci:ci + 1] * r2[:, 0:128])
                                               * inv[:, ci:ci + 1])
            kw = (kh32 * w_in_all[:, ci:ci + 1]).astype(BF16)
            upd = lax.dot_general(kw, v_ext, (((0,), (0,)), ((), ())), preferred_element_type=F32)
            c_ref[4 * d + h] = decay_all[:, ci:ci + 1] * c_st + upd


def _mlstm_scan(qk, p0, gate_bias, batch, t_len):
    rows = qk.shape[0]
    nch = t_len // M_CHUNK
    nctx = CTX // M_CHUNK
    fwd = lambda b, j: b * nch + j
    bwd = lambda b, j: b * nch + jnp.where(j < nctx, nctx - 1 - j, nch + nctx - 1 - j)
    out = jax.ShapeDtypeStruct((rows, M_HEADS * M_V), F32)
    return pl.pallas_call(
        _mlstm_kernel,
        out_shape=(out, out),
        grid=(batch, nch),
        in_specs=[
            pl.BlockSpec((M_CHUNK, 1024), lambda b, j: (fwd(b, j), 0)),
            pl.BlockSpec((M_CHUNK, 1024), lambda b, j: (bwd(b, j), 0)),
            pl.BlockSpec((M_CHUNK, 512), lambda b, j: (fwd(b, j), P0_MV // 512)),
            pl.BlockSpec((M_CHUNK, 512), lambda b, j: (bwd(b, j), P0_MV // 512)),
            pl.BlockSpec((M_CHUNK, 128), lambda b, j: (fwd(b, j), P0_GATE // 128)),
            pl.BlockSpec((M_CHUNK, 128), lambda b, j: (bwd(b, j), P0_GATE // 128)),
            _const_spec((1, 128)),
        ],
        out_specs=(
            pl.BlockSpec((M_CHUNK, 512), lambda b, j: (fwd(b, j), 0)),
            pl.BlockSpec((M_CHUNK, 512), lambda b, j: (bwd(b, j), 0)),
        ),
        scratch_shapes=[
            pltpu.VMEM((2 * M_HEADS, M_QKP, 256), F32),
            pltpu.VMEM((2, 1, 128), F32),
        ],
        compiler_params=_cparams(("parallel", "arbitrary")),
        name="mlstm_scan",
    )(qk, qk, p0, p0, p0, p0, gate_bias)


def _head_rms(x, gain_row, n_heads, width):
    parts = []
    for h in range(n_heads):
        sl = slice(h * width, (h + 1) * width)
        parts.append(_rms(x[:, sl], gain_row[:, sl]))
    return jnp.concatenate(parts, axis=1)


def _ab_out_kernel(x_ref, a_ref, hf_ref, hb_ref, og_ref, hg_ref, wa_ref, wm_ref, gains_ref, tab_ref, o_ref, *, per):
    i = pl.program_id(0)
    m = _head_rms(hf_ref[...] + hb_ref[...], hg_ref[...], M_HEADS, M_V) * _sigmoid(og_ref[...])
    y = (jnp.dot(a_ref[...].astype(BF16), wa_ref[...], preferred_element_type=F32)
         + jnp.dot(m.astype(BF16), wm_ref[...], preferred_element_type=F32))
    _gated_residual_tile(x_ref, y, gains_ref[1:2, :], tab_ref, 2, i, per, TM, o_ref)


def _ab_out(xs, a, hf, hb, p0, head_gain, wa, wm, gains, tab, per):
    rows = xs.shape[0]
    row_spec = lambda w, cb=0: pl.BlockSpec((TM, w), lambda i: (i, cb))
    return pl.pallas_call(
        functools.partial(_ab_out_kernel, per=per),
        out_shape=jax.ShapeDtypeStruct((rows, D), F32),
        grid=(rows // TM,),
        in_specs=[
            row_spec(D), row_spec(512), row_spec(512), row_spec(512), row_spec(512, P0_OG // 512),
            _const_spec((1, 512)), _const_spec((512, D)), _const_spec((512, D)),
            _const_spec((4, D)), _const_spec((6, 8, D)),
        ],
        out_specs=row_spec(D),
        compiler_params=_cparams(("parallel",)),
        name="ab_out",
    )(xs, a, hf, hb, p0, head_gain, wa, wm, gains, tab)


def _swiglu(h, wgu_ref, wdn_ref, lead):
    acc = None
    for c0 in range(0, D_FF, FF_CHUNK):
        g = jnp.dot(h, wgu_ref[lead + (slice(None), slice(c0, c0 + FF_CHUNK))], preferred_element_type=F32)
        u = jnp.dot(h, wgu_ref[lead + (slice(None), slice(D_FF + c0, D_FF + c0 + FF_CHUNK))],
                    preferred_element_type=F32)
        a = (_silu(g) * u).astype(BF16)
        part = jnp.dot(a, wdn_ref[lead + (slice(c0, c0 + FF_CHUNK), slice(None))], preferred_element_type=F32)
        acc = part if acc is None else acc + part
    return acc


def _ffn_kernel(x_ref, gains_ref, tab_ref, wgu_ref, wdn_ref, o_ref, *, per):
    i = pl.program_id(0)
    h = _norm_mod_tile(x_ref, gains_ref[2:3, :], tab_ref, 3, 4, i, per, TM).astype(BF16)
    y = _swiglu(h, wgu_ref, wdn_ref, ())
    _gated_residual_tile(x_ref, y, gains_ref[3:4, :], tab_ref, 5, i, per, TM, o_ref)


def _ffn(xs, gains, tab, wgu, wdn, per):
    rows = xs.shape[0]
    return pl.pallas_call(
        functools.partial(_ffn_kernel, per=per),
        out_shape=jax.ShapeDtypeStruct((rows, D), F32),
        grid=(rows // TM,),
        in_specs=[
            pl.BlockSpec((TM, D), lambda i: (i, 0)),
            _const_spec((4, D)), _const_spec((6, 8, D)),
            _const_spec((D, 2 * D_FF)), _const_spec((D_FF, D)),
        ],
        out_specs=pl.BlockSpec((TM, D), lambda i: (i, 0)),
        compiler_params=_cparams(("parallel",)),
        name="ffn",
    )(xs, gains, tab, wgu, wdn)


def _hgrn_stream(q, v, g_ref, w_ref, sl, st, gsum, neg, lmask, rev):
    c = G_SUB
    ns = G_CHUNK // c
    blk = lambda ref, i: ref[c * i:c * (i + 1), sl]
    bcast = lambda ref, r: jnp.broadcast_to(ref[r:r + 1, sl], (c, 128))
    qb = lambda i: q[c * i:c * (i + 1)]
    bound = [bcast(g_ref, c * j if rev else c * j + c - 1) for j in range(ns)]
    khat = jnp.concatenate([jnp.exp2(bound[j] - blk(w_ref, j)) for j in range(ns)], axis=0)
    pairs = [(i, j) for i in range(ns) for j in range(ns) if (i < j if rev else i > j)]
    qst = jnp.concatenate([qb(i) * jnp.exp2(blk(g_ref, i) - bound[j]) for (i, j) in pairs], axis=0)
    cross = lax.dot_general(qst.astype(BF16), khat.astype(BF16), (((1,), (1,)), ((), ())),
                            preferred_element_type=F32)
    z_rows = []
    for i in range(ns):
        gi, qi = blk(g_ref, i), qb(i)
        z_rows.append(jnp.concatenate(
            [qi * jnp.exp2(gi + neg[s] - bcast(w_ref, c * i + s)) for s in range(c)], axis=1))
    diag = jnp.dot(jnp.concatenate(z_rows, axis=0).astype(BF16), gsum, preferred_element_type=F32)
    a_rows = []
    for i in range(ns):
        acc = diag[c * i:c * (i + 1)] * lmask[i]
        for p, (ii, j) in enumerate(pairs):
            if ii == i:
                acc = acc + cross[c * p:c * (p + 1)] * lmask[j]
        a_rows.append(acc)
    a = jnp.concatenate(a_rows, axis=0).astype(BF16)
    g_end = g_ref[0:1, sl] if rev else g_ref[G_CHUNK - 1:G_CHUNK, sl]
    o = (jnp.dot(a, v.astype(BF16), preferred_element_type=F32)
         + lax.dot_general((q * jnp.exp2(g_ref[:, sl])).astype(BF16), st.astype(BF16), (((1,), (1,)), ((), ())),
                           preferred_element_type=F32))
    kw = jnp.exp2(g_end - w_ref[:, sl]).astype(BF16)
    st_new = st * jnp.exp2(g_end) + lax.dot_general(v.astype(BF16), kw, (((0,), (0,)), ((), ())),
                                                     preferred_element_type=F32)
    return o, st_new


def _hgrn_kernel(qf_ref, zf_ref, vf_ref, qb_ref, zb_ref, vb_ref, lb_ref, of_ref, ob_ref, st_ref, g_ref, w_ref):
    j = pl.program_id(1)
    L = G_CHUNK

    @pl.when(j == 0)
    def _():
        st_ref[...] = jnp.zeros_like(st_ref)

    r_i = lax.broadcasted_iota(jnp.int32, (L, L), 0)
    c_i = lax.broadcasted_iota(jnp.int32, (L, L), 1)
    lb = lb_ref[...]
    log_1mlb = jnp.log(1.0 - lb)
    gs_r = lax.broadcasted_iota(jnp.int32, (G_SUB * 128, L), 0)
    gs_c = lax.broadcasted_iota(jnp.int32, (G_SUB * 128, L), 1)
    gsum = jnp.where(gs_r // 128 == gs_c % G_SUB, 1.0, 0.0).astype(BF16)
    sub = lax.broadcasted_iota(jnp.int32, (G_SUB, 128), 0)
    lane = lax.broadcasted_iota(jnp.int32, (G_SUB, L), 1)
    lmask = [jnp.where(lane // G_SUB == jj, 1.0, 0.0) for jj in range(L // G_SUB)]
    for d, (q_ref, z_ref, v_ref, o_ref) in enumerate(((qf_ref, zf_ref, vf_ref, of_ref),
                                                       (qb_ref, zb_ref, vb_ref, ob_ref))):
        rev = d == 1
        neg = [jnp.where((sub <= s) if rev else (sub >= s), 0.0, -jnp.inf) for s in range(G_SUB)]
        tri = jnp.where((c_i >= r_i) if rev else (c_i <= r_i), LOG2E, 0.0).astype(F32)
        z = z_ref[...]
        e = jnp.exp(-jnp.abs(z))
        r = 1.0 / (1.0 + e)
        sig_pos = jnp.where(z >= 0, r, e * r)
        log_f = jnp.log(lb + (1.0 - lb) * sig_pos)
        log_k = log_1mlb - jnp.maximum(z, 0.0) - jnp.log(1.0 + e)
        g_all = jnp.dot(tri, log_f, precision=HIGHEST, preferred_element_type=F32)
        g_ref[d] = g_all
        w_ref[d] = g_all - LOG2E * log_k
        q_all = _silu(q_ref[...])
        for h in range(G_HEADS):
            sl = slice(h * 128, (h + 1) * 128)
            o, st_new = _hgrn_stream(q_all[:, sl], v_ref[:, sl], g_ref.at[d], w_ref.at[d], sl, st_ref[d * G_HEADS + h],
                                     gsum, neg, lmask, rev)
            o_ref[:, sl] = o
            st_ref[d * G_HEADS + h] = st_new


def _hgrn_scan(p1, lb_row, batch, t_len):
    rows = p1.shape[0]
    nch = t_len // G_CHUNK
    nctx = CTX // G_CHUNK
    fwd = lambda b, j: b * nch + j
    bwd = lambda b, j: b * nch + jnp.where(j < nctx, nctx - 1 - j, nch + nctx - 1 - j)
    spec = lambda order, cb: pl.BlockSpec((G_CHUNK, 1024), lambda b, j: (order(b, j), cb))
    out = jax.ShapeDtypeStruct((rows, G_HEADS * G_DV), F32)
    return pl.pallas_call(
        _hgrn_kernel,
        out_shape=(out, out),
        grid=(batch, nch),
        in_specs=[spec(fwd, 0), spec(fwd, 1), spec(fwd, 3), spec(bwd, 0), spec(bwd, 2), spec(bwd, 3),
                  _const_spec((1, 1024))],
        out_specs=(spec(fwd, 0), spec(bwd, 0)),
        scratch_shapes=[pltpu.VMEM((2 * G_HEADS, G_DV, G_DK), F32),
                        pltpu.VMEM((2, G_CHUNK, 1024), F32),
                        pltpu.VMEM((2, G_CHUNK, 1024), F32)],
        compiler_params=_cparams(("parallel", "arbitrary")),
        name="hgrn_scan",
    )(p1, p1, p1, p1, p1, p1, lb_row)


def _c_out_kernel(x_ref, of_ref, ob_ref, hg_ref, gain_ref, w_ref, gains_ref, tab_ref, o_ref, *, per):
    i = pl.program_id(0)
    o = _head_rms(of_ref[...] + ob_ref[...], gain_ref[...], G_HEADS, G_DV) * _silu(hg_ref[...])
    y = jnp.dot(o.astype(BF16), w_ref[...], preferred_element_type=F32)
    _gated_residual_tile(x_ref, y, gains_ref[1:2, :], tab_ref, 2, i, per, TM, o_ref)


def _c_out(xs, of, ob, p1, head_gain, w, gains, tab, per):
    rows = xs.shape[0]
    row_spec = lambda cb=0: pl.BlockSpec((TM, D), lambda i: (i, cb))
    return pl.pallas_call(
        functools.partial(_c_out_kernel, per=per),
        out_shape=jax.ShapeDtypeStruct((rows, D), F32),
        grid=(rows // TM,),
        in_specs=[row_spec(), row_spec(), row_spec(), row_spec(4), _const_spec((1, D)), _const_spec((D, D)),
                  _const_spec((4, D)), _const_spec((6, 8, D))],
        out_specs=row_spec(),
        compiler_params=_cparams(("parallel",)),
        name="c_out",
    )(xs, of, ob, p1, head_gain, w, gains, tab)


def _router_kernel(x_ref, gains_ref, tab_ref, wr_ref, h_ref, r_ref, *, per):
    i = pl.program_id(0)
    h = _norm_mod_tile(x_ref, gains_ref[2:3, :], tab_ref, 3, 4, i, per, TM)
    for j in range(D // 128):
        h_ref[pl.ds(j, TM, stride=D // 128), :] = h[:, j * 128:(j + 1) * 128]
    logits = lax.dot_general(wr_ref[...], h, (((1,), (1,)), ((), ())), precision=HIGHEST,
                             preferred_element_type=F32)
    eid = lax.broadcasted_iota(jnp.int32, logits.shape, 0).astype(F32)
    m1 = jnp.max(logits, axis=0, keepdims=True)
    i1 = jnp.min(jnp.where(logits == m1, eid, float(N_EXP)), axis=0, keepdims=True)
    rest = jnp.where(eid == i1, -jnp.inf, logits)
    m2 = jnp.max(rest, axis=0, keepdims=True)
    i2 = jnp.min(jnp.where(rest == m2, eid, float(N_EXP)), axis=0, keepdims=True)
    e2 = jnp.exp(m2 - m1)
    g1 = 1.0 / (1.0 + e2)
    g2 = e2 * g1
    rid = lax.broadcasted_iota(jnp.int32, (128, TM), 0)
    packed = jnp.where(rid == 0, i1, jnp.where(rid == 1, i2, jnp.where(rid == 2, g1, jnp.where(rid == 3, g2, 0.0))))
    r_ref[...] = packed.T


def _router(xs, gains, tab, wr_t, per):
    rows = xs.shape[0]
    return pl.pallas_call(
        functools.partial(_router_kernel, per=per),
        out_shape=(jax.ShapeDtypeStruct((rows * (D // 128), 128), F32), jax.ShapeDtypeStruct((rows, 128), F32)),
        grid=(rows // TM,),
        in_specs=[pl.BlockSpec((TM, D), lambda i: (i, 0)), _const_spec((4, D)), _const_spec((6, 8, D)),
                  _const_spec((N_EXP, D))],
        out_specs=(pl.BlockSpec((TM * (D // 128), 128), lambda i: (i, 0)), pl.BlockSpec((TM, 128), lambda i: (i, 0))),
        compiler_params=_cparams(("parallel",)),
        name="moe_router",
    )(xs, gains, tab, wr_t)


def _tile_rows(ref, index):
    return jnp.concatenate([ref[index(j)] for j in range(D // 128)], axis=1)


def _expert_kernel(be_ref, nact_ref, tok0_ref, nxt_ref, dst_ref, dstl_ref, h_hbm, wgu_ref, wdn_ref, yg_hbm,
                   xbuf, ybuf, gsem, ssem):
    b = pl.program_id(0)
    nb = pl.num_programs(0)
    slot = b % 2
    other = 1 - slot

    def gather_rows(idx_ref, dst_slot, r0, r1):
        for r in range(r0, r1):
            src = h_hbm.at[pl.ds(pl.multiple_of(idx_ref[0, 0, r], 8), 8)]
            pltpu.make_async_copy(src, xbuf.at[dst_slot, pl.ds(8 * r, 8)], gsem.at[dst_slot]).start(priority=1)

    def scatter_rows(idx_ref, src_slot, r0, r1):
        for r in range(r0, r1):
            dst = yg_hbm.at[pl.ds(pl.multiple_of(idx_ref[0, 0, r], 8), 8)]
            pltpu.make_async_copy(ybuf.at[src_slot, pl.ds(8 * r, 8)], dst, ssem.at[src_slot]).start()

    def wait_gather(s):
        pltpu.make_async_copy(h_hbm.at[pl.ds(0, MOE_BLOCK * (D // 128))], xbuf.at[s], gsem.at[s]).wait()

    def wait_scatter(s):
        pltpu.make_async_copy(ybuf.at[s], yg_hbm.at[pl.ds(0, MOE_BLOCK * (D // 128))], ssem.at[s]).wait()

    @pl.when(b == 0)
    def _():
        ybuf[...] = jnp.zeros_like(ybuf)
        gather_rows(tok0_ref, 0, 0, MOE_BLOCK)

    wait_gather(slot)

    @pl.when(b >= 1)
    def _():
        wait_scatter(slot)

    n_chunks = D_FF // FF_CHUNK
    per = -(-MOE_BLOCK // n_chunks)

    @pl.when(b < nact_ref[0])
    def _():
        x = _tile_rows(xbuf, lambda j: (slot, pl.ds(j, MOE_BLOCK, stride=D // 128), slice(None))).astype(BF16)
        acc = None
        for ci in range(n_chunks):
            c0 = ci * FF_CHUNK
            g = jnp.dot(x, wgu_ref[0, :, c0:c0 + FF_CHUNK], preferred_element_type=F32)
            u = jnp.dot(x, wgu_ref[0, :, D_FF + c0:D_FF + c0 + FF_CHUNK], preferred_element_type=F32)
            a = (_silu(g) * u).astype(BF16)
            part = jnp.dot(a, wdn_ref[0, c0:c0 + FF_CHUNK, :], preferred_element_type=F32)
            acc = part if acc is None else acc + part
            r0, r1 = ci * per, min((ci + 1) * per, MOE_BLOCK)
            gather_rows(nxt_ref, other, r0, r1)
            scatter_rows(dst_ref, other, r0, r1)
        for j in range(D // 128):
            ybuf[slot, pl.ds(j, MOE_BLOCK, stride=D // 128), :] = acc[:, j * 128:(j + 1) * 128]

    @pl.when(b >= nact_ref[0])
    def _():
        gather_rows(nxt_ref, other, 0, MOE_BLOCK)
        scatter_rows(dst_ref, other, 0, MOE_BLOCK)

    @pl.when(b == nb - 1)
    def _():
        wait_gather(other)
        wait_scatter(other)
        scatter_rows(dstl_ref, slot, 0, MOE_BLOCK)
        wait_scatter(slot)


def _experts(block_expert, n_active, tok3, dst3, h3, wgu, wdn, n_rows_out):
    n_blocks = block_expert.shape[0]
    smem = lambda f: pl.BlockSpec((1, 1, MOE_BLOCK), f, memory_space=pltpu.SMEM)
    grid_spec = pltpu.PrefetchScalarGridSpec(
        num_scalar_prefetch=2,
        grid=(n_blocks,),
        in_specs=[
            smem(lambda b, be, na: (0, 0, 0)),
            smem(lambda b, be, na: (jnp.minimum(b + 1, n_blocks - 1), 0, 0)),
            smem(lambda b, be, na: (b, 0, 0)),
            smem(lambda b, be, na: (n_blocks, 0, 0)),
            pl.BlockSpec(memory_space=pl.ANY),
            pl.BlockSpec((1, D, 2 * D_FF), lambda b, be, na: (be[b], 0, 0)),
            pl.BlockSpec((1, D_FF, D), lambda b, be, na: (be[b], 0, 0)),
        ],
        out_specs=pl.BlockSpec(memory_space=pl.ANY),
        scratch_shapes=[pltpu.VMEM((2, MOE_BLOCK * (D // 128), 128), F32), pltpu.VMEM((2, MOE_BLOCK * (D // 128), 128), F32),
                        pltpu.SemaphoreType.DMA((2,)), pltpu.SemaphoreType.DMA((2,))],
    )
    return pl.pallas_call(
        _expert_kernel,
        out_shape=jax.ShapeDtypeStruct((n_rows_out * (D // 128), 128), F32),
        grid_spec=grid_spec,
        compiler_params=_cparams(("arbitrary",)),
        name="moe_experts",
    )(block_expert, n_active, tok3, tok3, dst3, dst3, h3, wgu, wdn)


def _combine_kernel(yg_ref, r_ref, x_ref, gains_ref, tab_ref, o_ref, *, per):
    tile = pl.program_id(0) * per + 1 + pl.program_id(1)
    gates = r_ref[...]
    rows = lambda k: _tile_rows(yg_ref, lambda j: (pl.ds(k * (D // 128) + j, HALF, stride=2 * (D // 128)), slice(None)))
    y = gates[:, 2:3] * rows(0) + gates[:, 3:4] * rows(1)
    idx = _mod_index(tile, per)
    gate = tab_ref[5, pl.ds(idx, 1), :]
    o_ref[...] = x_ref[...] + gate * _rms(y, gains_ref[3:4, :])


def _combine_latents(yg, rout, xs, gains, tab, batch, s_len, per):
    nl = s_len // HALF
    tile = lambda b, j: b * per + 1 + j
    return pl.pallas_call(
        functools.partial(_combine_kernel, per=per),
        out_shape=jax.ShapeDtypeStruct((batch * s_len, D), F32),
        grid=(batch, nl),
        in_specs=[
            pl.BlockSpec((HALF * 2 * (D // 128), 128), lambda b, j: (tile(b, j), 0)),
            pl.BlockSpec((HALF, 128), lambda b, j: (tile(b, j), 0)),
            pl.BlockSpec((HALF, D), lambda b, j: (tile(b, j), 0)),
            _const_spec((4, D)), _const_spec((6, 8, D)),
        ],
        out_specs=pl.BlockSpec((HALF, D), lambda b, j: (b * nl + j, 0)),
        compiler_params=_cparams(("parallel", "parallel")),
        name="moe_combine",
    )(yg, rout, xs, gains, tab)


def _moe_plan(rout):
    n_tok = rout.shape[0]
    n_flat = 2 * n_tok
    n_blocks = -(-n_flat // MOE_BLOCK) + N_EXP
    assert n_blocks * MOE_BLOCK == n_flat + N_EXP * MOE_BLOCK and n_flat < 2 ** 19
    flat_e = rout[:, 0:2].astype(jnp.int32).reshape(-1)
    experts = jnp.arange(N_EXP, dtype=jnp.int32)
    counts = jnp.sum((flat_e[:, None] == experts[None, :]).astype(jnp.int32), axis=0)
    padded = (counts + MOE_BLOCK - 1) // MOE_BLOCK * MOE_BLOCK
    pend = jnp.cumsum(padded)
    blk = jnp.arange(n_blocks, dtype=jnp.int32)
    block_expert = jnp.minimum(jnp.searchsorted(pend, blk * MOE_BLOCK, side='right'), N_EXP - 1).astype(jnp.int32)
    n_active = (pend[-1] // MOE_BLOCK).astype(jnp.int32).reshape(1)
    flat = jnp.arange(n_flat, dtype=jnp.int32)
    pad_i = jnp.arange(MOE_BLOCK, dtype=jnp.int32)[None, :]
    pad_key = jnp.where(pad_i < (padded - counts)[:, None], experts[:, None] * 2 ** 20 + 2 ** 19 + pad_i,
                        jnp.iinfo(jnp.int32).max)
    keys = jnp.concatenate([flat_e * 2 ** 20 + flat, pad_key.reshape(-1)])
    vals = jnp.concatenate([flat, jnp.full((N_EXP * MOE_BLOCK,), -1, jnp.int32)])
    _, rows = lax.sort((keys, vals), num_keys=1)
    rows = rows.reshape(n_blocks, MOE_BLOCK)
    valid = rows >= 0
    tok3 = (jnp.where(valid, rows // 2, 0) * (D // 128)).reshape(n_blocks, 1, MOE_BLOCK)
    dummy = n_flat + pad_i
    dst = jnp.where(valid, rows, dummy)
    dst3 = (jnp.concatenate([dummy, dst], axis=0) * (D // 128)).reshape(n_blocks + 1, 1, MOE_BLOCK)
    return block_expert, n_active, tok3, dst3, n_flat + MOE_BLOCK


def _rot_half_cols(w):
    half = A_ROPE // 2
    g = w.reshape(w.shape[0], -1, A_ROPE)
    return jnp.concatenate([-g[..., half:], g[..., :half]], axis=-1).reshape(w.shape)


def _pad_groups(w, width, to):
    k = w.shape[0]
    g = w.reshape(k, -1, width)
    return jnp.pad(g, ((0, 0), (0, 0), (0, to - width))).reshape(k, -1)


def _layer0_inproj_weight(w_in):
    splits = np.cumsum([A_QLORA, A_KVLORA, A_ROPE, M_HEADS * M_QK, M_HEADS * M_QK, M_HEADS * M_V, M_HEADS * M_V])
    cq, ckv, kr, mq, mk, mv, og, gt = jnp.split(w_in, [int(s) for s in splits], axis=1)
    pad64 = lambda a: jnp.pad(a, ((0, 0), (0, 64)))
    cols = [cq, ckv, pad64(kr), mv, _pad_groups(mq, M_QK, M_QKP), _pad_groups(mk, M_QK, M_QKP), og,
            jnp.pad(gt, ((0, 0), (0, 128 - gt.shape[1]))), pad64(_rot_half_cols(kr))]
    w = jnp.concatenate(cols, axis=1)
    assert w.shape[1] == P0_N
    return w.astype(BF16)


def _rope_tables(batch, s_len):
    t = jnp.arange(s_len)
    row = (t // GRID_W).astype(F32)
    col = (t % GRID_W).astype(F32)
    n_freq = A_ROPE // 4
    inv_freq = ROPE_THETA ** (-jnp.arange(n_freq, dtype=F32) / n_freq)
    ang = jnp.concatenate([row[:, None] * inv_freq, col[:, None] * inv_freq], axis=-1)
    ang = jnp.concatenate([ang, ang], axis=-1)

    def table(vals, fill):
        lat = jnp.pad(vals, ((0, 0), (0, 128 - A_ROPE)), constant_values=fill)
        one = jnp.concatenate([jnp.full((CTX, 128), fill, F32), lat], axis=0)
        return jnp.tile(one, (batch, 1))

    return table(jnp.cos(ang), 1.0), table(jnp.sin(ang), 0.0)


def kernel(x, c, ctx, c_ctx, w_mod, b_mod, norm_gains, w_in_ab, mla_q_gain, mla_w_uq, mla_kv_gain, mla_w_ukv,
           ml_conv_w, ml_conv_b, ml_gate_b, ml_head_gain, w_out_ab, ffn_w_gu, ffn_w_dn, w_in_c, hg_lb_logits,
           hg_head_gain, w_out_c, moe_router, moe_w_gu, moe_w_dn):
    batch, s_len, d = x.shape
    assert d == D and ctx.shape[1] == CTX and batch == 2
    assert s_len % 512 == 0 and (batch * (CTX + s_len)) % TM == 0
    t_len = CTX + s_len
    per = t_len // HALF

    xs = jnp.concatenate([ctx, x], axis=1).reshape(batch * t_len, D)
    cc = jnp.concatenate([c, c_ctx[None, :], jnp.zeros((8 - batch - 1, D), F32)], axis=0)
    tabs = _mod_table(cc, w_mod, b_mod)

    gains = norm_gains[0]
    tab = tabs[0]
    p0 = _inproj(xs, gains, tab, _layer0_inproj_weight(w_in_ab[0]), per, P0_N // 2)

    cos, sin = _rope_tables(batch, s_len)
    w_uq = mla_w_uq[0].reshape(A_QLORA, A_HEADS, A_NOPE + A_ROPE)
    wq_nope = w_uq[:, :, :A_NOPE].reshape(A_QLORA, A_HEADS * A_NOPE)
    wq_rope = w_uq[:, :, A_NOPE:].reshape(A_QLORA, A_HEADS * A_ROPE)
    w_ukv = mla_w_ukv[0].reshape(A_KVLORA, A_HEADS, A_NOPE + A_V)
    q, k, v = _mla_prep(
        p0, cos, sin, mla_q_gain[0][None, :], mla_kv_gain[0][None, :],
        wq_nope.astype(BF16),
        _pad_groups(wq_rope, A_ROPE, 128).astype(BF16),
        _pad_groups(_rot_half_cols(wq_rope), A_ROPE, 128).astype(BF16),
        w_ukv[:, :, :A_NOPE].reshape(A_KVLORA, -1).astype(BF16),
        w_ukv[:, :, A_NOPE:].reshape(A_KVLORA, -1).astype(BF16),
        batch, t_len)
    attn = _attention(q, k, v).reshape(batch * t_len, A_HEADS * A_V)

    conv_w = jnp.concatenate([_pad_groups(ml_conv_w[0][:, :M_HEADS * M_QK], M_QK, M_QKP),
                              _pad_groups(ml_conv_w[0][:, M_HEADS * M_QK:], M_QK, M_QKP)], axis=1)
    conv_b = jnp.concatenate([_pad_groups(ml_conv_b[0][None, :M_HEADS * M_QK], M_QK, M_QKP),
                              _pad_groups(ml_conv_b[0][None, M_HEADS * M_QK:], M_QK, M_QKP)], axis=1)
    post = jnp.concatenate([jnp.ones((1, 512), F32), jnp.full((1, 512), M_QK ** -0.5, F32)], axis=1)
    qk = _mlstm_conv(p0, conv_w, conv_b, post, t_len)
    gate_bias = jnp.pad(ml_gate_b[0][None, :], ((0, 0), (0, 128 - 4 * M_HEADS)))
    hf, hb = _mlstm_scan(qk, p0, gate_bias, batch, t_len)

    w_out = w_out_ab[0].astype(BF16)
    xs = _ab_out(xs, attn, hf, hb, p0, ml_head_gain[0][None, :], w_out[:A_HEADS * A_V], w_out[A_HEADS * A_V:],
                 gains, tab, per)
    xs = _ffn(xs, gains, tab, ffn_w_gu[0].astype(BF16), ffn_w_dn[0].astype(BF16), per)

    gains = norm_gains[1]
    tab = tabs[1]
    lbs = jnp.cumsum(jax.nn.softmax(hg_lb_logits.astype(F32), axis=0), axis=0)
    lb_row = (lbs - lbs[0])[1][None, :]
    p1 = _inproj(xs, gains, tab, w_in_c[0].astype(BF16), per, 1024)
    of, ob = _hgrn_scan(p1, lb_row, batch, t_len)
    xs = _c_out(xs, of, ob, p1, hg_head_gain[0][None, :], w_out_c[0].astype(BF16), gains, tab, per)

    h, rout = _router(xs, gains, tab, moe_router[0].T, per)
    block_expert, n_active, tok3, dst3, n_rows_out = _moe_plan(rout)
    yg = _experts(block_expert, n_active, tok3, dst3, h, moe_w_gu[0].astype(BF16), moe_w_dn[0].astype(BF16), n_rows_out)
    out = _combine_latents(yg, rout, xs, gains, tab, batch, s_len, per)
    return out.reshape(batch, s_len, D)
```

```python
import functools

import jax
import jax.numpy as jnp
import numpy as np
from jax import lax
from jax.experimental import pallas as pl
from jax.experimental.pallas import tpu as pltpu

F32 = jnp.float32
BF16 = jnp.bfloat16
HIGHEST = lax.Precision.HIGHEST
LOG2E = 1.4426950408889634

D = 1024
CTX = 256
EPS = 1e-6
HALF = 256
TM = 512
VMEM_LIMIT = 56 * 1024 * 1024

A_HEADS = 4
A_QLORA = 256
A_KVLORA = 128
A_NOPE = 128
A_ROPE = 64
A_V = 128
A_HD = 256
A_SCALE = (A_NOPE + A_ROPE) ** -0.5
ROPE_THETA = 10000.0
GRID_W = 64
A_QSCALE = A_SCALE * LOG2E
A_TQ = 256
A_TK = 4096
A_RS = 32

M_HEADS = 4
M_QK = 64
M_V = 128
M_CHUNK = 128
M_QKP = 128

G_HEADS = 8
G_DK = 128
G_DV = 128
G_CHUNK = 64
G_SUB = 8

D_FF = 2816
N_EXP = 8
MOE_BLOCK = 512
FF_CHUNK = 256

P0_CQ = 0
P0_CKV = 256
P0_KR = 384
P0_MV = 512
P0_MQ = 1024
P0_MK = 1536
P0_OG = 2048
P0_GATE = 2560
P0_KRR = 2688
P0_N = 2816


def _cparams(sem):
    return pltpu.CompilerParams(dimension_semantics=sem, vmem_limit_bytes=VMEM_LIMIT)


def _const_spec(shape):
    nd = len(shape)
    return pl.BlockSpec(shape, lambda *_: (0,) * nd, pipeline_mode=pl.Buffered(1))


def _rms(x, gain_row):
    ms = jnp.mean(x * x, axis=-1, keepdims=True)
    return x * lax.rsqrt(ms + EPS) * gain_row


def _mod_index(half_idx, per):
    return jnp.where(half_idx % per == 0, 2, half_idx // per)


def _silu(x):
    return x * _sigmoid(x)


def _sigmoid(x):
    return 1.0 / (1.0 + jnp.exp(-x))


def _norm_mod_rows(x, gain_row, tab_ref, k_shift, k_scale, idx):
    shift = tab_ref[k_shift, pl.ds(idx, 1), :]
    scale = tab_ref[k_scale, pl.ds(idx, 1), :]
    return _rms(x, gain_row) * (1.0 + scale) + shift


def _norm_mod_tile(x_ref, gain_row, tab_ref, k_shift, k_scale, tile, per, rows):
    parts = []
    for s in range(rows // HALF):
        idx = _mod_index(tile * (rows // HALF) + s, per)
        parts.append(_norm_mod_rows(x_ref[s * HALF:(s + 1) * HALF, :], gain_row, tab_ref, k_shift, k_scale, idx))
    return parts[0] if len(parts) == 1 else jnp.concatenate(parts, axis=0)


def _gated_residual_tile(x_ref, y, gain_row, tab_ref, k_gate, tile, per, rows, o_ref):
    yn = _rms(y, gain_row)
    for s in range(rows // HALF):
        idx = _mod_index(tile * (rows // HALF) + s, per)
        gate = tab_ref[k_gate, pl.ds(idx, 1), :]
        sl = slice(s * HALF, (s + 1) * HALF)
        o_ref[sl, :] = x_ref[sl, :] + gate * yn[sl, :]


def _mod_kernel(c_ref, w_ref, b_ref, o_ref):
    a = _silu(c_ref[...])
    o_ref[0, 0] = jnp.dot(a, w_ref[0], precision=HIGHEST, preferred_element_type=F32) + b_ref[0, 0]


def _mod_table(cc, w_mod, b_mod):
    depth = w_mod.shape[0]
    return pl.pallas_call(
        _mod_kernel,
        out_shape=jax.ShapeDtypeStruct((depth, 6, 8, D), F32),
        grid=(depth, 6),
        in_specs=[
            pl.BlockSpec((8, D), lambda l, k: (0, 0)),
            pl.BlockSpec((1, D, D), lambda l, k: (l, 0, k)),
            pl.BlockSpec((1, 1, 1, D), lambda l, k: (l, k, 0, 0)),
        ],
        out_specs=pl.BlockSpec((1, 1, 8, D), lambda l, k: (l, k, 0, 0)),
        compiler_params=_cparams(("arbitrary", "arbitrary")),
        name="mod_table",
    )(cc, w_mod, b_mod.reshape(depth, 6, 1, D))


def _inproj_kernel(x_ref, gains_ref, tab_ref, w_ref, o_ref, *, per, nchunk):
    i = pl.program_id(0)
    h = _norm_mod_tile(x_ref, gains_ref[0:1, :], tab_ref, 0, 1, i, per, TM).astype(BF16)
    n = w_ref.shape[1]
    for n0 in range(0, n, nchunk):
        o_ref[:, n0:n0 + nchunk] = jnp.dot(h, w_ref[:, n0:n0 + nchunk], preferred_element_type=F32)


def _inproj(xs, gains, tab, w, per, nchunk):
    rows = xs.shape[0]
    n = w.shape[1]
    return pl.pallas_call(
        functools.partial(_inproj_kernel, per=per, nchunk=nchunk),
        out_shape=jax.ShapeDtypeStruct((rows, n), F32),
        grid=(rows // TM,),
        in_specs=[
            pl.BlockSpec((TM, D), lambda i: (i, 0)),
            _const_spec((4, D)),
            _const_spec((6, 8, D)),
            _const_spec((D, n)),
        ],
        out_specs=pl.BlockSpec((TM, n), lambda i: (i, 0)),
        compiler_params=_cparams(("parallel",)),
        name="inproj",
    )(xs, gains, tab, w)


def _mla_prep_kernel(cq_ref, ckr_ref, krr_ref, cos_ref, sin_ref, qg_ref, kvg_ref, wqn_ref, wqr_ref, wqrr_ref,
                     wk_ref, wv_ref, q_ref, k_ref, v_ref):
    qn = _rms(cq_ref[...], qg_ref[...]).astype(BF16)
    ckr = ckr_ref[...]
    kvn = _rms(ckr[:, :A_KVLORA], kvg_ref[...]).astype(BF16)
    cos = cos_ref[...]
    sin = sin_ref[...]
    q_nope = jnp.dot(qn, wqn_ref[...], preferred_element_type=F32)
    q_r = jnp.dot(qn, wqr_ref[...], preferred_element_type=F32)
    q_rr = jnp.dot(qn, wqrr_ref[...], preferred_element_type=F32)
    k_nope = jnp.dot(kvn, wk_ref[...], preferred_element_type=F32)
    vv = jnp.dot(kvn, wv_ref[...], preferred_element_type=F32)
    k_rope = (ckr[:, A_KVLORA:] * cos + krr_ref[...] * sin).astype(BF16)
    lane = lax.broadcasted_iota(jnp.int32, (HALF, A_V), 1)
    ones_col = jnp.where(lane == 0, 1.0, 0.0).astype(BF16)
    for h in range(A_HEADS):
        sl = slice(h * 128, (h + 1) * 128)
        q_rope = q_r[:, sl] * cos + q_rr[:, sl] * sin
        q_ref[0, h, :, 0:128] = (q_nope[:, sl] * A_QSCALE).astype(BF16)
        q_ref[0, h, :, 128:256] = (q_rope * A_QSCALE).astype(BF16)
        k_ref[0, h, :, 0:128] = k_nope[:, sl].astype(BF16)
        k_ref[0, h, :, 128:256] = k_rope
        v_ref[0, h, :, 0:A_V] = vv[:, sl].astype(BF16)
        v_ref[0, h, :, A_V:2 * A_V] = ones_col


def _mla_prep(p0, cos, sin, q_gain, kv_gain, wqn, wqr, wqrr, wk, wv, batch, t_len):
    rows = p0.shape[0]
    per = t_len // HALF
    qk_shape = jax.ShapeDtypeStruct((batch, A_HEADS, t_len, A_HD), BF16)
    v_shape = jax.ShapeDtypeStruct((batch, A_HEADS, t_len, 2 * A_V), BF16)
    out_map = lambda i: (i // per, 0, i % per, 0)
    return pl.pallas_call(
        _mla_prep_kernel,
        out_shape=(qk_shape, qk_shape, v_shape),
        grid=(rows // HALF,),
        in_specs=[
            pl.BlockSpec((HALF, 256), lambda i: (i, P0_CQ // 256)),
            pl.BlockSpec((HALF, 256), lambda i: (i, P0_CKV // 256)),
            pl.BlockSpec((HALF, 128), lambda i: (i, P0_KRR // 128)),
            pl.BlockSpec((HALF, 128), lambda i: (i, 0)),
            pl.BlockSpec((HALF, 128), lambda i: (i, 0)),
            _const_spec((1, A_QLORA)),
            _const_spec((1, A_KVLORA)),
            _const_spec((A_QLORA, 512)),
            _const_spec((A_QLORA, 512)),
            _const_spec((A_QLORA, 512)),
            _const_spec((A_KVLORA, 512)),
            _const_spec((A_KVLORA, 512)),
        ],
        out_specs=(
            pl.BlockSpec((1, A_HEADS, HALF, A_HD), out_map),
            pl.BlockSpec((1, A_HEADS, HALF, A_HD), out_map),
            pl.BlockSpec((1, A_HEADS, HALF, 2 * A_V), out_map),
        ),
        compiler_params=_cparams(("parallel",)),
        name="mla_prep",
    )(p0, p0, p0, cos, sin, q_gain, kv_gain, wqn, wqr, wqrr, wk, wv)


def _attn_kernel(q_ref, qn_ref, k_ref, v_ref, o_ref, s_ref, c_ref, p_ref, m_ref, acc_ref, *, n_lat_chunks, tk):
    qi = pl.program_id(2)
    q = q_ref[0, 0]
    qn = qn_ref[0, 0]

    def scores(qq, start, width):
        return lax.dot_general(qq, k_ref[0, 0, pl.ds(start, width), :], (((1,), (1,)), ((), ())),
                               preferred_element_type=F32)

    def softmax_step(s_view, start, width, first):
        if not first:
            m_all = m_ref[...]
        m_parts, a_parts = [], []
        for r0 in range(0, A_TQ, A_RS):
            rows = slice(r0, r0 + A_RS)
            s = s_view[rows, :]
            s_max = jnp.max(s, axis=-1, keepdims=True)
            if first:
                m_new = s_max
            else:
                m_prev = m_all[rows]
                m_new = jnp.maximum(m_prev, s_max)
                a_parts.append(jnp.exp2(m_prev - m_new))
            m_parts.append(m_new)
            p_ref[rows, 0:width] = jnp.exp2(s - m_new).astype(BF16)
        m_ref[...] = jnp.concatenate(m_parts, axis=0)
        pv = jnp.dot(p_ref[:, 0:width], v_ref[0, 0, pl.ds(start, width), :], preferred_element_type=F32)
        acc_ref[...] = pv if first else jnp.concatenate(a_parts, axis=0) * acc_ref[...] + pv

    def kstart(c):
        return pl.multiple_of(CTX + c * tk, 256)

    def finish():
        acc = acc_ref[...]
        o_ref[0] = acc[:, 0:A_V] / acc[:, A_V:A_V + 1]

    @pl.when(qi == 0)
    def _():
        c_ref[...] = scores(q, 0, CTX)
        s_ref[1] = scores(qn, CTX, tk)
        softmax_step(c_ref, 0, CTX, True)
        finish()
        c_ref[...] = scores(qn, 0, CTX)

    @pl.when(qi > 0)
    def _():
        s_ref[0] = scores(q, kstart(1), tk)
        softmax_step(c_ref, 0, CTX, True)
        softmax_step(s_ref.at[1], kstart(0), tk, False)

        def body(c2, carry):
            c = 2 * c2 + 1
            s_ref[1] = scores(q, kstart(c + 1), tk)
            softmax_step(s_ref.at[0], kstart(c), tk, False)
            s_ref[0] = scores(q, kstart(c + 2), tk)
            softmax_step(s_ref.at[1], kstart(c + 1), tk, False)
            return carry

        lax.fori_loop(0, n_lat_chunks // 2 - 1, body, 0)
        s_ref[1] = scores(qn, CTX, tk)
        softmax_step(s_ref.at[0], kstart(n_lat_chunks - 1), tk, False)
        c_ref[...] = scores(qn, 0, CTX)
        finish()


def _attention(q, k, v):
    batch, heads, t_len, _ = q.shape
    tk = min(A_TK, (t_len - CTX) // 2)
    n_lat_chunks = (t_len - CTX) // tk
    n_q = t_len // A_TQ
    assert n_lat_chunks % 2 == 0 and n_lat_chunks * tk == t_len - CTX and tk % 256 == 0
    return pl.pallas_call(
        functools.partial(_attn_kernel, n_lat_chunks=n_lat_chunks, tk=tk),
        out_shape=jax.ShapeDtypeStruct((batch, t_len, heads * A_V), F32),
        grid=(batch, heads, n_q),
        in_specs=[
            pl.BlockSpec((1, 1, A_TQ, A_HD), lambda b, h, i: (b, h, i, 0)),
            pl.BlockSpec((1, 1, A_TQ, A_HD), lambda b, h, i: (b, h, jnp.minimum(i + 1, n_q - 1), 0)),
            pl.BlockSpec((1, 1, t_len, A_HD), lambda b, h, i: (b, h, 0, 0)),
            pl.BlockSpec((1, 1, t_len, 2 * A_V), lambda b, h, i: (b, h, 0, 0)),
        ],
        out_specs=pl.BlockSpec((1, A_TQ, A_V), lambda b, h, i: (b, i, h)),
        scratch_shapes=[
            pltpu.VMEM((2, A_TQ, tk), F32),
            pltpu.VMEM((A_TQ, CTX), F32),
            pltpu.VMEM((A_TQ, tk), BF16),
            pltpu.VMEM((A_TQ, 1), F32),
            pltpu.VMEM((A_TQ, 2 * A_V), F32),
        ],
        compiler_params=_cparams(("parallel", "parallel", "arbitrary")),
        name="mla_attention",
    )(q, q, k, v)


def _conv_kernel(x_ref, prev_ref, next_ref, w_ref, b_ref, post_ref, o_ref, *, t_len):
    i = pl.program_id(0)
    x = x_ref[...]
    row = lax.broadcasted_iota(jnp.int32, (TM, 1), 0)
    pos = (i * TM + row) % t_len
    x_prev = jnp.where(row == 0, prev_ref[7:8, :], pltpu.roll(x, 1, axis=0))
    x_next = jnp.where(row == TM - 1, next_ref[0:1, :], pltpu.roll(x, TM - 1, axis=0))
    seq_start = (pos == 0) | (pos == CTX)
    seq_end = (pos == CTX - 1) | (pos == t_len - 1)
    x_prev = jnp.where(seq_start, 0.0, x_prev)
    x_next = jnp.where(seq_end, 0.0, x_next)
    y = w_ref[0:1, :] * x_prev + w_ref[1:2, :] * x + w_ref[2:3, :] * x_next + b_ref[...]
    o_ref[...] = _silu(y) * post_ref[...]


def _mlstm_conv(p0, conv_w, conv_b, post, t_len):
    rows = p0.shape[0]
    n8 = rows // 8
    cb = P0_MQ // 1024
    return pl.pallas_call(
        functools.partial(_conv_kernel, t_len=t_len),
        out_shape=jax.ShapeDtypeStruct((rows, 1024), F32),
        grid=(rows // TM,),
        in_specs=[
            pl.BlockSpec((TM, 1024), lambda i: (i, cb)),
            pl.BlockSpec((8, 1024), lambda i: (jnp.maximum(i * (TM // 8) - 1, 0), cb)),
            pl.BlockSpec((8, 1024), lambda i: (jnp.minimum((i + 1) * (TM // 8), n8 - 1), cb)),
            _const_spec((3, 1024)),
            _const_spec((1, 1024)),
            _const_spec((1, 1024)),
        ],
        out_specs=pl.BlockSpec((TM, 1024), lambda i: (i, 0)),
        compiler_params=_cparams(("parallel",)),
        name="mlstm_conv",
    )(p0, p0, p0, conv_w, conv_b, post)


def _log_sigmoid(x):
    return jnp.minimum(x, 0.0) - jnp.log(1.0 + jnp.exp(-jnp.abs(x)))


def _mlstm_kernel(qkf_ref, qkb_ref, vf_ref, vb_ref, gf_ref, gb_ref, gbias_ref, hf_ref, hb_ref, c_ref, m_ref):
    j = pl.program_id(1)
    L = M_CHUNK

    @pl.when(j == 0)
    def _():
        c_ref[...] = jnp.zeros_like(c_ref)
        m_ref[...] = jnp.zeros_like(m_ref)

    r_i = lax.broadcasted_iota(jnp.int32, (L, L), 0)
    c_i = lax.broadcasted_iota(jnp.int32, (L, L), 1)
    row = lax.broadcasted_iota(jnp.int32, (L, 128), 0)
    lane = lax.broadcasted_iota(jnp.int32, (L, 128), 1)

    for d, (qk_ref, v_ref, g_ref, h_ref) in enumerate(((qkf_ref, vf_ref, gf_ref, hf_ref),
                                                        (qkb_ref, vb_ref, gb_ref, hb_ref))):
        rev = d == 1
        keep = (c_i >= r_i) if rev else (c_i <= r_i)
        tri = jnp.where(keep, LOG2E, 0.0).astype(F32)
        pre = g_ref[...] + gbias_ref[...]
        csum = jnp.dot(tri, _log_sigmoid(pre), precision=HIGHEST, preferred_element_type=F32)
        g_run = pltpu.roll(csum, 124, axis=1)
        u = pre * LOG2E - g_run
        c_run = u
        k = 1
        while k < L:
            if rev:
                shifted = jnp.where(row >= L - k, -jnp.inf, pltpu.roll(c_run, L - k, axis=0))
            else:
                shifted = jnp.where(row < k, -jnp.inf, pltpu.roll(c_run, k, axis=0))
            c_run = jnp.maximum(c_run, shifted)
            k *= 2
        last = 0 if rev else L - 1
        m_prev = m_ref[d]
        m_run = jnp.maximum(c_run, m_prev)
        g_end = g_run[last:last + 1, :]
        m_new = g_end + jnp.maximum(m_prev, c_run[last:last + 1, :])
        w_inter_all = jnp.exp2(m_prev - m_run)
        floor_all = jnp.exp2(-(g_run + m_run))
        w_in_all = jnp.exp2(g_end + u - m_new)
        decay_all = jnp.exp2(g_end + m_prev - m_new)
        u_t = u.T
        m_ref[d] = m_new
        for h in range(M_HEADS):
            ci = 8 * d + h
            qh = qk_ref[:, h * 128:(h + 1) * 128].astype(BF16)
            kh32 = qk_ref[:, 512 + h * 128:512 + (h + 1) * 128]
            sc = lax.dot_general(qh, kh32.astype(BF16), (((1,), (1,)), ((), ())), preferred_element_type=F32)
            s = sc * jnp.exp2(jnp.where(keep, u_t[ci:ci + 1, :] - m_run[:, ci:ci + 1], -jnp.inf))
            ones_col = jnp.where(lane == ci, 1.0, 0.0).astype(BF16)
            v_ext = jnp.concatenate([v_ref[:, h * 128:(h + 1) * 128].astype(BF16), ones_col], axis=1)
            c_st = c_ref[4 * d + h]
            r1 = jnp.dot(s.astype(BF16), v_ext, preferred_element_type=F32)
            r2 = jnp.dot(qh, c_st.astype(BF16), preferred_element_type=F32)
            den = r1[:, 128:256] + w_inter_all * r2[:, 128:256]
            inv = 1.0 / jnp.maximum(jnp.abs(den), floor_all)
            h_ref[:, h * 128:(h + 1) * 128] = ((r1[:, 0:128] + w_inter_all[:, ci:ci + 1] * r2[:, 0:128])
                                               * inv[:, ci:ci + 1])
            kw = (kh32 * w_in_all[:, ci:ci + 1]).astype(BF16)
            upd = lax.dot_general(kw, v_ext, (((0,), (0,)), ((), ())), preferred_element_type=F32)
            c_ref[4 * d + h] = decay_all[:, ci:ci + 1] * c_st + upd


def _mlstm_scan(qk, p0, gate_bias, batch, t_len):
    rows = qk.shape[0]
    nch = t_len // M_CHUNK
    nctx = CTX // M_CHUNK
    fwd = lambda b, j: b * nch + j
    bwd = lambda b, j: b * nch + jnp.where(j < nctx, nctx - 1 - j, nch + nctx - 1 - j)
    out = jax.ShapeDtypeStruct((rows, M_HEADS * M_V), F32)
    return pl.pallas_call(
        _mlstm_kernel,
        out_shape=(out, out),
        grid=(batch, nch),
        in_specs=[
            pl.BlockSpec((M_CHUNK, 1024), lambda b, j: (fwd(b, j), 0)),
            pl.BlockSpec((M_CHUNK, 1024), lambda b, j: (bwd(b, j), 0)),
            pl.BlockSpec((M_CHUNK, 512), lambda b, j: (fwd(b, j), P0_MV // 512)),
            pl.BlockSpec((M_CHUNK, 512), lambda b, j: (bwd(b, j), P0_MV // 512)),
            pl.BlockSpec((M_CHUNK, 128), lambda b, j: (fwd(b, j), P0_GATE // 128)),
            pl.BlockSpec((M_CHUNK, 128), lambda b, j: (bwd(b, j), P0_GATE // 128)),
            _const_spec((1, 128)),
        ],
        out_specs=(
            pl.BlockSpec((M_CHUNK, 512), lambda b, j: (fwd(b, j), 0)),
            pl.BlockSpec((M_CHUNK, 512), lambda b, j: (bwd(b, j), 0)),
        ),
        scratch_shapes=[
            pltpu.VMEM((2 * M_HEADS, M_QKP, 256), F32),
            pltpu.VMEM((2, 1, 128), F32),
        ],
        compiler_params=_cparams(("parallel", "arbitrary")),
        name="mlstm_scan",
    )(qk, qk, p0, p0, p0, p0, gate_bias)


def _head_rms(x, gain_row, n_heads, width):
    parts = []
    for h in range(n_heads):
        sl = slice(h * width, (h + 1) * width)
        parts.append(_rms(x[:, sl], gain_row[:, sl]))
    return jnp.concatenate(parts, axis=1)


def _ab_out_kernel(x_ref, a_ref, hf_ref, hb_ref, og_ref, hg_ref, wa_ref, wm_ref, gains_ref, tab_ref, o_ref, *, per):
    i = pl.program_id(0)
    m = _head_rms(hf_ref[...] + hb_ref[...], hg_ref[...], M_HEADS, M_V) * _sigmoid(og_ref[...])
    y = (jnp.dot(a_ref[...].astype(BF16), wa_ref[...], preferred_element_type=F32)
         + jnp.dot(m.astype(BF16), wm_ref[...], preferred_element_type=F32))
    _gated_residual_tile(x_ref, y, gains_ref[1:2, :], tab_ref, 2, i, per, TM, o_ref)


def _ab_out(xs, a, hf, hb, p0, head_gain, wa, wm, gains, tab, per):
    rows = xs.shape[0]
    row_spec = lambda w, cb=0: pl.BlockSpec((TM, w), lambda i: (i, cb))
    return pl.pallas_call(
        functools.partial(_ab_out_kernel, per=per),
        out_shape=jax.ShapeDtypeStruct((rows, D), F32),
        grid=(rows // TM,),
        in_specs=[
            row_spec(D), row_spec(512), row_spec(512), row_spec(512), row_spec(512, P0_OG // 512),
            _const_spec((1, 512)), _const_spec((512, D)), _const_spec((512, D)),
            _const_spec((4, D)), _const_spec((6, 8, D)),
        ],
        out_specs=row_spec(D),
        compiler_params=_cparams(("parallel",)),
        name="ab_out",
    )(xs, a, hf, hb, p0, head_gain, wa, wm, gains, tab)


def _swiglu(h, wgu_ref, wdn_ref, lead):
    acc = None
    for c0 in range(0, D_FF, FF_CHUNK):
        g = jnp.dot(h, wgu_ref[lead + (slice(None), slice(c0, c0 + FF_CHUNK))], preferred_element_type=F32)
        u = jnp.dot(h, wgu_ref[lead + (slice(None), slice(D_FF + c0, D_FF + c0 + FF_CHUNK))],
                    preferred_element_type=F32)
        a = (_silu(g) * u).astype(BF16)
        part = jnp.dot(a, wdn_ref[lead + (slice(c0, c0 + FF_CHUNK), slice(None))], preferred_element_type=F32)
        acc = part if acc is None else acc + part
    return acc


def _ffn_kernel(x_ref, gains_ref, tab_ref, wgu_ref, wdn_ref, o_ref, *, per):
    i = pl.program_id(0)
    h = _norm_mod_tile(x_ref, gains_ref[2:3, :], tab_ref, 3, 4, i, per, TM).astype(BF16)
    y = _swiglu(h, wgu_ref, wdn_ref, ())
    _gated_residual_tile(x_ref, y, gains_ref[3:4, :], tab_ref, 5, i, per, TM, o_ref)


def _ffn(xs, gains, tab, wgu, wdn, per):
    rows = xs.shape[0]
    return pl.pallas_call(
        functools.partial(_ffn_kernel, per=per),
        out_shape=jax.ShapeDtypeStruct((rows, D), F32),
        grid=(rows // TM,),
        in_specs=[
            pl.BlockSpec((TM, D), lambda i: (i, 0)),
            _const_spec((4, D)), _const_spec((6, 8, D)),
            _const_spec((D, 2 * D_FF)), _const_spec((D_FF, D)),
        ],
        out_specs=pl.BlockSpec((TM, D), lambda i: (i, 0)),
        compiler_params=_cparams(("parallel",)),
        name="ffn",
    )(xs, gains, tab, wgu, wdn)


def _hgrn_stream(q, v, g_ref, w_ref, sl, st, gsum, neg, lmask, rev):
    c = G_SUB
    ns = G_CHUNK // c
    blk = lambda ref, i: ref[c * i:c * (i + 1), sl]
    bcast = lambda ref, r: jnp.broadcast_to(ref[r:r + 1, sl], (c, 128))
    qb = lambda i: q[c * i:c * (i + 1)]
    bound = [bcast(g_ref, c * j if rev else c * j + c - 1) for j in range(ns)]
    khat = jnp.concatenate([jnp.exp2(bound[j] - blk(w_ref, j)) for j in range(ns)], axis=0)
    pairs = [(i, j) for i in range(ns) for j in range(ns) if (i < j if rev else i > j)]
    qst = jnp.concatenate([qb(i) * jnp.exp2(blk(g_ref, i) - bound[j]) for (i, j) in pairs], axis=0)
    cross = lax.dot_general(qst.astype(BF16), khat.astype(BF16), (((1,), (1,)), ((), ())),
                            preferred_element_type=F32)
    z_rows = []
    for i in range(ns):
        gi, qi = blk(g_ref, i), qb(i)
        z_rows.append(jnp.concatenate(
            [qi * jnp.exp2(gi + neg[s] - bcast(w_ref, c * i + s)) for s in range(c)], axis=1))
    diag = jnp.dot(jnp.concatenate(z_rows, axis=0).astype(BF16), gsum, preferred_element_type=F32)
    a_rows = []
    for i in range(ns):
        acc = diag[c * i:c * (i + 1)] * lmask[i]
        for p, (ii, j) in enumerate(pairs):
            if ii == i:
                acc = acc + cross[c * p:c * (p + 1)] * lmask[j]
        a_rows.append(acc)
    a = jnp.concatenate(a_rows, axis=0).astype(BF16)
    g_end = g_ref[0:1, sl] if rev else g_ref[G_CHUNK - 1:G_CHUNK, sl]
    o = (jnp.dot(a, v.astype(BF16), preferred_element_type=F32)
         + lax.dot_general((q * jnp.exp2(g_ref[:, sl])).astype(BF16), st.astype(BF16), (((1,), (1,)), ((), ())),
                           preferred_element_type=F32))
    kw = jnp.exp2(g_end - w_ref[:, sl]).astype(BF16)
    st_new = st * jnp.exp2(g_end) + lax.dot_general(v.astype(BF16), kw, (((0,), (0,)), ((), ())),
                                                     preferred_element_type=F32)
    return o, st_new


def _hgrn_kernel(qf_ref, zf_ref, vf_ref, qb_ref, zb_ref, vb_ref, lb_ref, of_ref, ob_ref, st_ref, g_ref, w_ref):
    j = pl.program_id(1)
    L = G_CHUNK

    @pl.when(j == 0)
    def _():
        st_ref[...] = jnp.zeros_like(st_ref)

    r_i = lax.broadcasted_iota(jnp.int32, (L, L), 0)
    c_i = lax.broadcasted_iota(jnp.int32, (L, L), 1)
    lb = lb_ref[...]
    log_1mlb = jnp.log(1.0 - lb)
    gs_r = lax.broadcasted_iota(jnp.int32, (G_SUB * 128, L), 0)
    gs_c = lax.broadcasted_iota(jnp.int32, (G_SUB * 128, L), 1)
    gsum = jnp.where(gs_r // 128 == gs_c % G_SUB, 1.0, 0.0).astype(BF16)
    sub = lax.broadcasted_iota(jnp.int32, (G_SUB, 128), 0)
    lane = lax.broadcasted_iota(jnp.int32, (G_SUB, L), 1)
    lmask = [jnp.where(lane // G_SUB == jj, 1.0, 0.0) for jj in range(L // G_SUB)]
    for d, (q_ref, z_ref, v_ref, o_ref) in enumerate(((qf_ref, zf_ref, vf_ref, of_ref),
                                                       (qb_ref, zb_ref, vb_ref, ob_ref))):
        rev = d == 1
        neg = [jnp.where((sub <= s) if rev else (sub >= s), 0.0, -jnp.inf) for s in range(G_SUB)]
        tri = jnp.where((c_i >= r_i) if rev else (c_i <= r_i), LOG2E, 0.0).astype(F32)
        z = z_ref[...]
        e = jnp.exp(-jnp.abs(z))
        r = 1.0 / (1.0 + e)
        sig_pos = jnp.where(z >= 0, r, e * r)
        log_f = jnp.log(lb + (1.0 - lb) * sig_pos)
        log_k = log_1mlb - jnp.maximum(z, 0.0) - jnp.log(1.0 + e)
        g_all = jnp.dot(tri, log_f, precision=HIGHEST, preferred_element_type=F32)
        g_ref[d] = g_all
        w_ref[d] = g_all - LOG2E * log_k
        q_all = _silu(q_ref[...])
        for h in range(G_HEADS):
            sl = slice(h * 128, (h + 1) * 128)
            o, st_new = _hgrn_stream(q_all[:, sl], v_ref[:, sl], g_ref.at[d], w_ref.at[d], sl, st_ref[d * G_HEADS + h],
                                     gsum, neg, lmask, rev)
            o_ref[:, sl] = o
            st_ref[d * G_HEADS + h] = st_new


def _hgrn_scan(p1, lb_row, batch, t_len):
    rows = p1.shape[0]
    nch = t_len // G_CHUNK
    nctx = CTX // G_CHUNK
    fwd = lambda b, j: b * nch + j
    bwd = lambda b, j: b * nch + jnp.where(j < nctx, nctx - 1 - j, nch + nctx - 1 - j)
    spec = lambda order, cb: pl.BlockSpec((G_CHUNK, 1024), lambda b, j: (order(b, j), cb))
    out = jax.ShapeDtypeStruct((rows, G_HEADS * G_DV), F32)
    return pl.pallas_call(
        _hgrn_kernel,
        out_shape=(out, out),
        grid=(batch, nch),
        in_specs=[spec(fwd, 0), spec(fwd, 1), spec(fwd, 3), spec(bwd, 0), spec(bwd, 2), spec(bwd, 3),
                  _const_spec((1, 1024))],
        out_specs=(spec(fwd, 0), spec(bwd, 0)),
        scratch_shapes=[pltpu.VMEM((2 * G_HEADS, G_DV, G_DK), F32),
                        pltpu.VMEM((2, G_CHUNK, 1024), F32),
                        pltpu.VMEM((2, G_CHUNK, 1024), F32)],
        compiler_params=_cparams(("parallel", "arbitrary")),
        name="hgrn_scan",
    )(p1, p1, p1, p1, p1, p1, lb_row)


def _c_out_kernel(x_ref, of_ref, ob_ref, hg_ref, gain_ref, w_ref, gains_ref, tab_ref, o_ref, *, per):
    i = pl.program_id(0)
    o = _head_rms(of_ref[...] + ob_ref[...], gain_ref[...], G_HEADS, G_DV) * _silu(hg_ref[...])
    y = jnp.dot(o.astype(BF16), w_ref[...], preferred_element_type=F32)
    _gated_residual_tile(x_ref, y, gains_ref[1:2, :], tab_ref, 2, i, per, TM, o_ref)


def _c_out(xs, of, ob, p1, head_gain, w, gains, tab, per):
    rows = xs.shape[0]
    row_spec = lambda cb=0: pl.BlockSpec((TM, D), lambda i: (i, cb))
    return pl.pallas_call(
        functools.partial(_c_out_kernel, per=per),
        out_shape=jax.ShapeDtypeStruct((rows, D), F32),
        grid=(rows // TM,),
        in_specs=[row_spec(), row_spec(), row_spec(), row_spec(4), _const_spec((1, D)), _const_spec((D, D)),
                  _const_spec((4, D)), _const_spec((6, 8, D))],
        out_specs=row_spec(),
        compiler_params=_cparams(("parallel",)),
        name="c_out",
    )(xs, of, ob, p1, head_gain, w, gains, tab)


def _router_kernel(x_ref, gains_ref, tab_ref, wr_ref, h_ref, r_ref, *, per):
    i = pl.program_id(0)
    h = _norm_mod_tile(x_ref, gains_ref[2:3, :], tab_ref, 3, 4, i, per, TM)
    for j in range(D // 128):
        h_ref[pl.ds(j, TM, stride=D // 128), :] = h[:, j * 128:(j + 1) * 128]
    logits = lax.dot_general(wr_ref[...], h, (((1,), (1,)), ((), ())), precision=HIGHEST,
                             preferred_element_type=F32)
    eid = lax.broadcasted_iota(jnp.int32, logits.shape, 0).astype(F32)
    m1 = jnp.max(logits, axis=0, keepdims=True)
    i1 = jnp.min(jnp.where(logits == m1, eid, float(N_EXP)), axis=0, keepdims=True)
    rest = jnp.where(eid == i1, -jnp.inf, logits)
    m2 = jnp.max(rest, axis=0, keepdims=True)
    i2 = jnp.min(jnp.where(rest == m2, eid, float(N_EXP)), axis=0, keepdims=True)
    e2 = jnp.exp(m2 - m1)
    g1 = 1.0 / (1.0 + e2)
    g2 = e2 * g1
    rid = lax.broadcasted_iota(jnp.int32, (128, TM), 0)
    packed = jnp.where(rid == 0, i1, jnp.where(rid == 1, i2, jnp.where(rid == 2, g1, jnp.where(rid == 3, g2, 0.0))))
    r_ref[...] = packed.T


def _router(xs, gains, tab, wr_t, per):
    rows = xs.shape[0]
    return pl.pallas_call(
        functools.partial(_router_kernel, per=per),
        out_shape=(jax.ShapeDtypeStruct((rows * (D // 128), 128), F32), jax.ShapeDtypeStruct((rows, 128), F32)),
        grid=(rows // TM,),
        in_specs=[pl.BlockSpec((TM, D), lambda i: (i, 0)), _const_spec((4, D)), _const_spec((6, 8, D)),
                  _const_spec((N_EXP, D))],
        out_specs=(pl.BlockSpec((TM * (D // 128), 128), lambda i: (i, 0)), pl.BlockSpec((TM, 128), lambda i: (i, 0))),
        compiler_params=_cparams(("parallel",)),
        name="moe_router",
    )(xs, gains, tab, wr_t)


def _tile_rows(ref, index):
    return jnp.concatenate([ref[index(j)] for j in range(D // 128)], axis=1)


def _expert_kernel(be_ref, nact_ref, tok0_ref, nxt_ref, dst_ref, dstl_ref, h_hbm, wgu_ref, wdn_ref, yg_hbm,
                   xbuf, ybuf, gsem, ssem):
    b = pl.program_id(0)
    nb = pl.num_programs(0)
    slot = b % 2
    other = 1 - slot

    def gather_rows(idx_ref, dst_slot, r0, r1):
        for r in range(r0, r1):
            src = h_hbm.at[pl.ds(pl.multiple_of(idx_ref[0, 0, r], 8), 8)]
            pltpu.make_async_copy(src, xbuf.at[dst_slot, pl.ds(8 * r, 8)], gsem.at[dst_slot]).start(priority=1)

    def scatter_rows(idx_ref, src_slot, r0, r1):
        for r in range(r0, r1):
            dst = yg_hbm.at[pl.ds(pl.multiple_of(idx_ref[0, 0, r], 8), 8)]
            pltpu.make_async_copy(ybuf.at[src_slot, pl.ds(8 * r, 8)], dst, ssem.at[src_slot]).start()

    def wait_gather(s):
        pltpu.make_async_copy(h_hbm.at[pl.ds(0, MOE_BLOCK * (D // 128))], xbuf.at[s], gsem.at[s]).wait()

    def wait_scatter(s):
        pltpu.make_async_copy(ybuf.at[s], yg_hbm.at[pl.ds(0, MOE_BLOCK * (D // 128))], ssem.at[s]).wait()

    @pl.when(b == 0)
    def _():
        ybuf[...] = jnp.zeros_like(ybuf)
        gather_rows(tok0_ref, 0, 0, MOE_BLOCK)

    wait_gather(slot)

    @pl.when(b >= 1)
    def _():
        wait_scatter(slot)

    n_chunks = D_FF // FF_CHUNK
    per_g = -(-MOE_BLOCK // (n_chunks // 2))
    per_s = -(-MOE_BLOCK // (n_chunks - n_chunks // 2))

    @pl.when(b < nact_ref[0])
    def _():
        x = _tile_rows(xbuf, lambda j: (slot, pl.ds(j, MOE_BLOCK, stride=D // 128), slice(None))).astype(BF16)
        acc = None
        for ci in range(n_chunks):
            c0 = ci * FF_CHUNK
            g = jnp.dot(x, wgu_ref[0, :, c0:c0 + FF_CHUNK], preferred_element_type=F32)
            u = jnp.dot(x, wgu_ref[0, :, D_FF + c0:D_FF + c0 + FF_CHUNK], preferred_element_type=F32)
            a = (_silu(g) * u).astype(BF16)
            part = jnp.dot(a, wdn_ref[0, c0:c0 + FF_CHUNK, :], preferred_element_type=F32)
            acc = part if acc is None else acc + part
            if ci < n_chunks // 2:
                gather_rows(nxt_ref, other, ci * per_g, min((ci + 1) * per_g, MOE_BLOCK))
            else:
                cs = ci - n_chunks // 2
                scatter_rows(dst_ref, other, cs * per_s, min((cs + 1) * per_s, MOE_BLOCK))
        for j in range(D // 128):
            ybuf[slot, pl.ds(j, MOE_BLOCK, stride=D // 128), :] = acc[:, j * 128:(j + 1) * 128]

    @pl.when(b >= nact_ref[0])
    def _():
        gather_rows(nxt_ref, other, 0, MOE_BLOCK)
        scatter_rows(dst_ref, other, 0, MOE_BLOCK)

    @pl.when(b == nb - 1)
    def _():
        wait_gather(other)
        wait_scatter(other)
        scatter_rows(dstl_ref, slot, 0, MOE_BLOCK)
        wait_scatter(slot)


def _experts(block_expert, n_active, tok3, dst3, h3, wgu, wdn, n_rows_out):
    n_blocks = block_expert.shape[0]
    smem = lambda f: pl.BlockSpec((1, 1, MOE_BLOCK), f, memory_space=pltpu.SMEM)
    grid_spec = pltpu.PrefetchScalarGridSpec(
        num_scalar_prefetch=2,
        grid=(n_blocks,),
        in_specs=[
            smem(lambda b, be, na: (0, 0, 0)),
            smem(lambda b, be, na: (jnp.minimum(b + 1, n_blocks - 1), 0, 0)),
            smem(lambda b, be, na: (b, 0, 0)),
            smem(lambda b, be, na: (n_blocks, 0, 0)),
            pl.BlockSpec(memory_space=pl.ANY),
            pl.BlockSpec((1, D, 2 * D_FF), lambda b, be, na: (be[b], 0, 0)),
            pl.BlockSpec((1, D_FF, D), lambda b, be, na: (be[b], 0, 0)),
        ],
        out_specs=pl.BlockSpec(memory_space=pl.ANY),
        scratch_shapes=[pltpu.VMEM((2, MOE_BLOCK * (D // 128), 128), F32), pltpu.VMEM((2, MOE_BLOCK * (D // 128), 128), F32),
                        pltpu.SemaphoreType.DMA((2,)), pltpu.SemaphoreType.DMA((2,))],
    )
    return pl.pallas_call(
        _expert_kernel,
        out_shape=jax.ShapeDtypeStruct((n_rows_out * (D // 128), 128), F32),
        grid_spec=grid_spec,
        compiler_params=_cparams(("arbitrary",)),
        name="moe_experts",
    )(block_expert, n_active, tok3, tok3, dst3, dst3, h3, wgu, wdn)


def _combine_kernel(yg_ref, r_ref, x_ref, gains_ref, tab_ref, o_ref, *, per):
    tile = pl.program_id(0) * per + 1 + pl.program_id(1)
    gates = r_ref[...]
    rows = lambda k: _tile_rows(yg_ref, lambda j: (pl.ds(k * (D // 128) + j, HALF, stride=2 * (D // 128)), slice(None)))
    y = gates[:, 2:3] * rows(0) + gates[:, 3:4] * rows(1)
    idx = _mod_index(tile, per)
    gate = tab_ref[5, pl.ds(idx, 1), :]
    o_ref[...] = x_ref[...] + gate * _rms(y, gains_ref[3:4, :])


def _combine_latents(yg, rout, xs, gains, tab, batch, s_len, per):
    nl = s_len // HALF
    tile = lambda b, j: b * per + 1 + j
    return pl.pallas_call(
        functools.partial(_combine_kernel, per=per),
        out_shape=jax.ShapeDtypeStruct((batch * s_len, D), F32),
        grid=(batch, nl),
        in_specs=[
            pl.BlockSpec((HALF * 2 * (D // 128), 128), lambda b, j: (tile(b, j), 0)),
            pl.BlockSpec((HALF, 128), lambda b, j: (tile(b, j), 0)),
            pl.BlockSpec((HALF, D), lambda b, j: (tile(b, j), 0)),
            _const_spec((4, D)), _const_spec((6, 8, D)),
        ],
        out_specs=pl.BlockSpec((HALF, D), lambda b, j: (b * nl + j, 0)),
        compiler_params=_cparams(("parallel", "parallel")),
        name="moe_combine",
    )(yg, rout, xs, gains, tab)


def _moe_plan(rout):
    n_tok = rout.shape[0]
    n_flat = 2 * n_tok
    n_blocks = -(-n_flat // MOE_BLOCK) + N_EXP
    assert n_blocks * MOE_BLOCK == n_flat + N_EXP * MOE_BLOCK and n_flat < 2 ** 19
    flat_e = rout[:, 0:2].astype(jnp.int32).reshape(-1)
    experts = jnp.arange(N_EXP, dtype=jnp.int32)
    counts = jnp.sum((flat_e[:, None] == experts[None, :]).astype(jnp.int32), axis=0)
    padded = (counts + MOE_BLOCK - 1) // MOE_BLOCK * MOE_BLOCK
    pend = jnp.cumsum(padded)
    blk = jnp.arange(n_blocks, dtype=jnp.int32)
    block_expert = jnp.minimum(jnp.searchsorted(pend, blk * MOE_BLOCK, side='right'), N_EXP - 1).astype(jnp.int32)
    n_active = (pend[-1] // MOE_BLOCK).astype(jnp.int32).reshape(1)
    flat = jnp.arange(n_flat, dtype=jnp.int32)
    pad_i = jnp.arange(MOE_BLOCK, dtype=jnp.int32)[None, :]
    pad_key = jnp.where(pad_i < (padded - counts)[:, None], experts[:, None] * 2 ** 20 + 2 ** 19 + pad_i,
                        jnp.iinfo(jnp.int32).max)
    keys = jnp.concatenate([flat_e * 2 ** 20 + flat, pad_key.reshape(-1)])
    vals = jnp.concatenate([flat, jnp.full((N_EXP * MOE_BLOCK,), -1, jnp.int32)])
    _, rows = lax.sort((keys, vals), num_keys=1)
    rows = rows.reshape(n_blocks, MOE_BLOCK)
    valid = rows >= 0
    tok3 = (jnp.where(valid, rows // 2, 0) * (D // 128)).reshape(n_blocks, 1, MOE_BLOCK)
    dummy = n_flat + pad_i
    dst = jnp.where(valid, rows, dummy)
    dst3 = (jnp.concatenate([dummy, dst], axis=0) * (D // 128)).reshape(n_blocks + 1, 1, MOE_BLOCK)
    return block_expert, n_active, tok3, dst3, n_flat + MOE_BLOCK


def _rot_half_cols(w):
    half = A_ROPE // 2
    g = w.reshape(w.shape[0], -1, A_ROPE)
    return jnp.concatenate([-g[..., half:], g[..., :half]], axis=-1).reshape(w.shape)


def _pad_groups(w, width, to):
    k = w.shape[0]
    g = w.reshape(k, -1, width)
    return jnp.pad(g, ((0, 0), (0, 0), (0, to - width))).reshape(k, -1)


def _layer0_inproj_weight(w_in):
    splits = np.cumsum([A_QLORA, A_KVLORA, A_ROPE, M_HEADS * M_QK, M_HEADS * M_QK, M_HEADS * M_V, M_HEADS * M_V])
    cq, ckv, kr, mq, mk, mv, og, gt = jnp.split(w_in, [int(s) for s in splits], axis=1)
    pad64 = lambda a: jnp.pad(a, ((0, 0), (0, 64)))
    cols = [cq, ckv, pad64(kr), mv, _pad_groups(mq, M_QK, M_QKP), _pad_groups(mk, M_QK, M_QKP), og,
            jnp.pad(gt, ((0, 0), (0, 128 - gt.shape[1]))), pad64(_rot_half_cols(kr))]
    w = jnp.concatenate(cols, axis=1)
    assert w.shape[1] == P0_N
    return w.astype(BF16)


def _rope_tables(batch, s_len):
    t = jnp.arange(s_len)
    row = (t // GRID_W).astype(F32)
    col = (t % GRID_W).astype(F32)
    n_freq = A_ROPE // 4
    inv_freq = ROPE_THETA ** (-jnp.arange(n_freq, dtype=F32) / n_freq)
    ang = jnp.concatenate([row[:, None] * inv_freq, col[:, None] * inv_freq], axis=-1)
    ang = jnp.concatenate([ang, ang], axis=-1)

    def table(vals, fill):
        lat = jnp.pad(vals, ((0, 0), (0, 128 - A_ROPE)), constant_values=fill)
        one = jnp.concatenate([jnp.full((CTX, 128), fill, F32), lat], axis=0)
        return jnp.tile(one, (batch, 1))

    return table(jnp.cos(ang), 1.0), table(jnp.sin(ang), 0.0)


def kernel(x, c, ctx, c_ctx, w_mod, b_mod, norm_gains, w_in_ab, mla_q_gain, mla_w_uq, mla_kv_gain, mla_w_ukv,
           ml_conv_w, ml_conv_b, ml_gate_b, ml_head_gain, w_out_ab, ffn_w_gu, ffn_w_dn, w_in_c, hg_lb_logits,
           hg_head_gain, w_out_c, moe_router, moe_w_gu, moe_w_dn):
    batch, s_len, d = x.shape
    assert d == D and ctx.shape[1] == CTX and batch == 2
    assert s_len % 512 == 0 and (batch * (CTX + s_len)) % TM == 0
    t_len = CTX + s_len
    per = t_len // HALF

    xs = jnp.concatenate([ctx, x], axis=1).reshape(batch * t_len, D)
    cc = jnp.concatenate([c, c_ctx[None, :], jnp.zeros((8 - batch - 1, D), F32)], axis=0)
    tabs = _mod_table(cc, w_mod, b_mod)

    gains = norm_gains[0]
    tab = tabs[0]
    p0 = _inproj(xs, gains, tab, _layer0_inproj_weight(w_in_ab[0]), per, P0_N // 2)

    cos, sin = _rope_tables(batch, s_len)
    w_uq = mla_w_uq[0].reshape(A_QLORA, A_HEADS, A_NOPE + A_ROPE)
    wq_nope = w_uq[:, :, :A_NOPE].reshape(A_QLORA, A_HEADS * A_NOPE)
    wq_rope = w_uq[:, :, A_NOPE:].reshape(A_QLORA, A_HEADS * A_ROPE)
    w_ukv = mla_w_ukv[0].reshape(A_KVLORA, A_HEADS, A_NOPE + A_V)
    q, k, v = _mla_prep(
        p0, cos, sin, mla_q_gain[0][None, :], mla_kv_gain[0][None, :],
        wq_nope.astype(BF16),
        _pad_groups(wq_rope, A_ROPE, 128).astype(BF16),
        _pad_groups(_rot_half_cols(wq_rope), A_ROPE, 128).astype(BF16),
        w_ukv[:, :, :A_NOPE].reshape(A_KVLORA, -1).astype(BF16),
        w_ukv[:, :, A_NOPE:].reshape(A_KVLORA, -1).astype(BF16),
        batch, t_len)
    attn = _attention(q, k, v).reshape(batch * t_len, A_HEADS * A_V)

    conv_w = jnp.concatenate([_pad_groups(ml_conv_w[0][:, :M_HEADS * M_QK], M_QK, M_QKP),
                              _pad_groups(ml_conv_w[0][:, M_HEADS * M_QK:], M_QK, M_QKP)], axis=1)
    conv_b = jnp.concatenate([_pad_groups(ml_conv_b[0][None, :M_HEADS * M_QK], M_QK, M_QKP),
                              _pad_groups(ml_conv_b[0][None, M_HEADS * M_QK:], M_QK, M_QKP)], axis=1)
    post = jnp.concatenate([jnp.ones((1, 512), F32), jnp.full((1, 512), M_QK ** -0.5, F32)], axis=1)
    qk = _mlstm_conv(p0, conv_w, conv_b, post, t_len)
    gate_bias = jnp.pad(ml_gate_b[0][None, :], ((0, 0), (0, 128 - 4 * M_HEADS)))
    hf, hb = _mlstm_scan(qk, p0, gate_bias, batch, t_len)

    w_out = w_out_ab[0].astype(BF16)
    xs = _ab_out(xs, attn, hf, hb, p0, ml_head_gain[0][None, :], w_out[:A_HEADS * A_V], w_out[A_HEADS * A_V:],
                 gains, tab, per)
    xs = _ffn(xs, gains, tab, ffn_w_gu[0].astype(BF16), ffn_w_dn[0].astype(BF16), per)

    gains = norm_gains[1]
    tab = tabs[1]
    lbs = jnp.cumsum(jax.nn.softmax(hg_lb_logits.astype(F32), axis=0), axis=0)
    lb_row = (lbs - lbs[0])[1][None, :]
    p1 = _inproj(xs, gains, tab, w_in_c[0].astype(BF16), per, 1024)
    of, ob = _hgrn_scan(p1, lb_row, batch, t_len)
    xs = _c_out(xs, of, ob, p1, hg_head_gain[0][None, :], w_out_c[0].astype(BF16), gains, tab, per)

    h, rout = _router(xs, gains, tab, moe_router[0].T, per)
    block_expert, n_active, tok3, dst3, n_rows_out = _moe_plan(rout)
    yg = _experts(block_expert, n_active, tok3, dst3, h, moe_w_gu[0].astype(BF16), moe_w_dn[0].astype(BF16), n_rows_out)
    out = _combine_latents(yg, rout, xs, gains, tab, batch, s_len, per)
    return out.reshape(batch, s_len, D)
```

```python
import functools

import jax
import jax.numpy as jnp
import numpy as np
from jax import lax
from jax.experimental import pallas as pl
from jax.experimental.pallas import tpu as pltpu

F32 = jnp.float32
BF16 = jnp.bfloat16
HIGHEST = lax.Precision.HIGHEST
LOG2E = 1.4426950408889634

D = 1024
CTX = 256
EPS = 1e-6
HALF = 256
TM = 512
VMEM_LIMIT = 56 * 1024 * 1024

A_HEADS = 4
A_QLORA = 256
A_KVLORA = 128
A_NOPE = 128
A_ROPE = 64
A_V = 128
A_HD = 256
A_SCALE = (A_NOPE + A_ROPE) ** -0.5
ROPE_THETA = 10000.0
GRID_W = 64
A_QSCALE = A_SCALE * LOG2E
A_TQ = 256
A_TK = 4096
A_RS = 32

M_HEADS = 4
M_QK = 64
M_V = 128
M_CHUNK = 128
M_QKP = 128

G_HEADS = 8
G_DK = 128
G_DV = 128
G_CHUNK = 64
G_SUB = 8

D_FF = 2816
N_EXP = 8
MOE_BLOCK = 512
FF_CHUNK = 256

P0_CQ = 0
P0_CKV = 256
P0_KR = 384
P0_MV = 512
P0_MQ = 1024
P0_MK = 1536
P0_OG = 2048
P0_GATE = 2560
P0_KRR = 2688
P0_N = 2816


def _cparams(sem):
    return pltpu.CompilerParams(dimension_semantics=sem, vmem_limit_bytes=VMEM_LIMIT)


def _const_spec(shape):
    nd = len(shape)
    return pl.BlockSpec(shape, lambda *_: (0,) * nd, pipeline_mode=pl.Buffered(1))


def _rms(x, gain_row):
    ms = jnp.mean(x * x, axis=-1, keepdims=True)
    return x * lax.rsqrt(ms + EPS) * gain_row


def _mod_index(half_idx, per):
    return jnp.where(half_idx % per == 0, 2, half_idx // per)


def _silu(x):
    return x * _sigmoid(x)


def _sigmoid(x):
    return 1.0 / (1.0 + jnp.exp(-x))


def _norm_mod_rows(x, gain_row, tab_ref, k_shift, k_scale, idx):
    shift = tab_ref[k_shift, pl.ds(idx, 1), :]
    scale = tab_ref[k_scale, pl.ds(idx, 1), :]
    return _rms(x, gain_row) * (1.0 + scale) + shift


def _norm_mod_tile(x_ref, gain_row, tab_ref, k_shift, k_scale, tile, per, rows):
    parts = []
    for s in range(rows // HALF):
        idx = _mod_index(tile * (rows // HALF) + s, per)
        parts.append(_norm_mod_rows(x_ref[s * HALF:(s + 1) * HALF, :], gain_row, tab_ref, k_shift, k_scale, idx))
    return parts[0] if len(parts) == 1 else jnp.concatenate(parts, axis=0)


def _gated_residual_tile(x_ref, y, gain_row, tab_ref, k_gate, tile, per, rows, o_ref):
    yn = _rms(y, gain_row)
    for s in range(rows // HALF):
        idx = _mod_index(tile * (rows // HALF) + s, per)
        gate = tab_ref[k_gate, pl.ds(idx, 1), :]
        sl = slice(s * HALF, (s + 1) * HALF)
        o_ref[sl, :] = x_ref[sl, :] + gate * yn[sl, :]


def _mod_kernel(c_ref, w_ref, b_ref, o_ref):
    a = _silu(c_ref[...])
    o_ref[0, 0] = jnp.dot(a, w_ref[0], precision=HIGHEST, preferred_element_type=F32) + b_ref[0, 0]


def _mod_table(cc, w_mod, b_mod):
    depth = w_mod.shape[0]
    return pl.pallas_call(
        _mod_kernel,
        out_shape=jax.ShapeDtypeStruct((depth, 6, 8, D), F32),
        grid=(depth, 6),
        in_specs=[
            pl.BlockSpec((8, D), lambda l, k: (0, 0)),
            pl.BlockSpec((1, D, D), lambda l, k: (l, 0, k)),
            pl.BlockSpec((1, 1, 1, D), lambda l, k: (l, k, 0, 0)),
        ],
        out_specs=pl.BlockSpec((1, 1, 8, D), lambda l, k: (l, k, 0, 0)),
        compiler_params=_cparams(("arbitrary", "arbitrary")),
        name="mod_table",
    )(cc, w_mod, b_mod.reshape(depth, 6, 1, D))


def _inproj_kernel(x_ref, gains_ref, tab_ref, w_ref, o_ref, *, per, nchunk):
    i = pl.program_id(0)
    h = _norm_mod_tile(x_ref, gains_ref[0:1, :], tab_ref, 0, 1, i, per, TM).astype(BF16)
    n = w_ref.shape[1]
    for n0 in range(0, n, nchunk):
        o_ref[:, n0:n0 + nchunk] = jnp.dot(h, w_ref[:, n0:n0 + nchunk], preferred_element_type=F32)


def _inproj(xs, gains, tab, w, per, nchunk):
    rows = xs.shape[0]
    n = w.shape[1]
    return pl.pallas_call(
        functools.partial(_inproj_kernel, per=per, nchunk=nchunk),
        out_shape=jax.ShapeDtypeStruct((rows, n), F32),
        grid=(rows // TM,),
        in_specs=[
            pl.BlockSpec((TM, D), lambda i: (i, 0)),
            _const_spec((4, D)),
            _const_spec((6, 8, D)),
            _const_spec((D, n)),
        ],
        out_specs=pl.BlockSpec((TM, n), lambda i: (i, 0)),
        compiler_params=_cparams(("parallel",)),
        name="inproj",
    )(xs, gains, tab, w)


def _mla_prep_kernel(cq_ref, ckr_ref, krr_ref, cos_ref, sin_ref, qg_ref, kvg_ref, wqn_ref, wqr_ref, wqrr_ref,
                     wk_ref, wv_ref, q_ref, k_ref, v_ref):
    qn = _rms(cq_ref[...], qg_ref[...]).astype(BF16)
    ckr = ckr_ref[...]
    kvn = _rms(ckr[:, :A_KVLORA], kvg_ref[...]).astype(BF16)
    cos = cos_ref[...]
    sin = sin_ref[...]
    q_nope = jnp.dot(qn, wqn_ref[...], preferred_element_type=F32)
    q_r = jnp.dot(qn, wqr_ref[...], preferred_element_type=F32)
    q_rr = jnp.dot(qn, wqrr_ref[...], preferred_element_type=F32)
    k_nope = jnp.dot(kvn, wk_ref[...], preferred_element_type=F32)
    vv = jnp.dot(kvn, wv_ref[...], preferred_element_type=F32)
    k_rope = (ckr[:, A_KVLORA:] * cos + krr_ref[...] * sin).astype(BF16)
    lane = lax.broadcasted_iota(jnp.int32, (HALF, A_V), 1)
    ones_col = jnp.where(lane == 0, 1.0, 0.0).astype(BF16)
    for h in range(A_HEADS):
        sl = slice(h * 128, (h + 1) * 128)
        q_rope = q_r[:, sl] * cos + q_rr[:, sl] * sin
        q_ref[0, h, :, 0:128] = (q_nope[:, sl] * A_QSCALE).astype(BF16)
        q_ref[0, h, :, 128:256] = (q_rope * A_QSCALE).astype(BF16)
        k_ref[0, h, :, 0:128] = k_nope[:, sl].astype(BF16)
        k_ref[0, h, :, 128:256] = k_rope
        v_ref[0, h, :, 0:A_V] = vv[:, sl].astype(BF16)
        v_ref[0, h, :, A_V:2 * A_V] = ones_col


def _mla_prep(p0, cos, sin, q_gain, kv_gain, wqn, wqr, wqrr, wk, wv, batch, t_len):
    rows = p0.shape[0]
    per = t_len // HALF
    qk_shape = jax.ShapeDtypeStruct((batch, A_HEADS, t_len, A_HD), BF16)
    v_shape = jax.ShapeDtypeStruct((batch, A_HEADS, t_len, 2 * A_V), BF16)
    out_map = lambda i: (i // per, 0, i % per, 0)
    return pl.pallas_call(
        _mla_prep_kernel,
        out_shape=(qk_shape, qk_shape, v_shape),
        grid=(rows // HALF,),
        in_specs=[
            pl.BlockSpec((HALF, 256), lambda i: (i, P0_CQ // 256)),
            pl.BlockSpec((HALF, 256), lambda i: (i, P0_CKV // 256)),
            pl.BlockSpec((HALF, 128), lambda i: (i, P0_KRR // 128)),
            pl.BlockSpec((HALF, 128), lambda i: (i, 0)),
            pl.BlockSpec((HALF, 128), lambda i: (i, 0)),
            _const_spec((1, A_QLORA)),
            _const_spec((1, A_KVLORA)),
            _const_spec((A_QLORA, 512)),
            _const_spec((A_QLORA, 512)),
            _const_spec((A_QLORA, 512)),
            _const_spec((A_KVLORA, 512)),
            _const_spec((A_KVLORA, 512)),
        ],
        out_specs=(
            pl.BlockSpec((1, A_HEADS, HALF, A_HD), out_map),
            pl.BlockSpec((1, A_HEADS, HALF, A_HD), out_map),
            pl.BlockSpec((1, A_HEADS, HALF, 2 * A_V), out_map),
        ),
        compiler_params=_cparams(("parallel",)),
        name="mla_prep",
    )(p0, p0, p0, cos, sin, q_gain, kv_gain, wqn, wqr, wqrr, wk, wv)


def _attn_kernel(q_ref, qn_ref, k_ref, v_ref, o_ref, s_ref, c_ref, p_ref, m_ref, acc_ref, *, n_lat_chunks, tk):
    qi = pl.program_id(2)
    q = q_ref[0, 0]
    qn = qn_ref[0, 0]

    def scores(qq, start, width):
        return lax.dot_general(qq, k_ref[0, 0, pl.ds(start, width), :], (((1,), (1,)), ((), ())),
                               preferred_element_type=F32)

    def softmax_step(s_view, start, width, first):
        if not first:
            m_all = m_ref[...]
        m_parts, a_parts = [], []
        for r0 in range(0, A_TQ, A_RS):
            rows = slice(r0, r0 + A_RS)
            s = s_view[rows, :]
            s_max = jnp.max(s, axis=-1, keepdims=True)
            if first:
                m_new = s_max
            else:
                m_prev = m_all[rows]
                m_new = jnp.maximum(m_prev, s_max)
                a_parts.append(jnp.exp2(m_prev - m_new))
            m_parts.append(m_new)
            p_ref[rows, 0:width] = jnp.exp2(s - m_new).astype(BF16)
        m_ref[...] = jnp.concatenate(m_parts, axis=0)
        pv = jnp.dot(p_ref[:, 0:width], v_ref[0, 0, pl.ds(start, width), :], preferred_element_type=F32)
        acc_ref[...] = pv if first else jnp.concatenate(a_parts, axis=0) * acc_ref[...] + pv

    def kstart(c):
        return pl.multiple_of(CTX + c * tk, 256)

    def finish():
        acc = acc_ref[...]
        o_ref[0] = acc[:, 0:A_V] / acc[:, A_V:A_V + 1]

    @pl.when(qi == 0)
    def _():
        c_ref[...] = scores(q, 0, CTX)
        s_ref[1] = scores(qn, CTX, tk)
        softmax_step(c_ref, 0, CTX, True)
        finish()
        c_ref[...] = scores(qn, 0, CTX)

    @pl.when(qi > 0)
    def _():
        s_ref[0] = scores(q, kstart(1), tk)
        softmax_step(c_ref, 0, CTX, True)
        softmax_step(s_ref.at[1], kstart(0), tk, False)

        def body(c2, carry):
            c = 2 * c2 + 1
            s_ref[1] = scores(q, kstart(c + 1), tk)
            softmax_step(s_ref.at[0], kstart(c), tk, False)
            s_ref[0] = scores(q, kstart(c + 2), tk)
            softmax_step(s_ref.at[1], kstart(c + 1), tk, False)
            return carry

        lax.fori_loop(0, n_lat_chunks // 2 - 1, body, 0)
        s_ref[1] = scores(qn, CTX, tk)
        softmax_step(s_ref.at[0], kstart(n_lat_chunks - 1), tk, False)
        c_ref[...] = scores(qn, 0, CTX)
        finish()


def _attention(q, k, v):
    batch, heads, t_len, _ = q.shape
    tk = min(A_TK, (t_len - CTX) // 2)
    n_lat_chunks = (t_len - CTX) // tk
    n_q = t_len // A_TQ
    assert n_lat_chunks % 2 == 0 and n_lat_chunks * tk == t_len - CTX and tk % 256 == 0
    return pl.pallas_call(
        functools.partial(_attn_kernel, n_lat_chunks=n_lat_chunks, tk=tk),
        out_shape=jax.ShapeDtypeStruct((batch, t_len, heads * A_V), F32),
        grid=(batch, heads, n_q),
        in_specs=[
            pl.BlockSpec((1, 1, A_TQ, A_HD), lambda b, h, i: (b, h, i, 0)),
            pl.BlockSpec((1, 1, A_TQ, A_HD), lambda b, h, i: (b, h, jnp.minimum(i + 1, n_q - 1), 0)),
            pl.BlockSpec((1, 1, t_len, A_HD), lambda b, h, i: (b, h, 0, 0)),
            pl.BlockSpec((1, 1, t_len, 2 * A_V), lambda b, h, i: (b, h, 0, 0)),
        ],
        out_specs=pl.BlockSpec((1, A_TQ, A_V), lambda b, h, i: (b, i, h)),
        scratch_shapes=[
            pltpu.VMEM((2, A_TQ, tk), F32),
            pltpu.VMEM((A_TQ, CTX), F32),
            pltpu.VMEM((A_TQ, tk), BF16),
            pltpu.VMEM((A_TQ, 1), F32),
            pltpu.VMEM((A_TQ, 2 * A_V), F32),
        ],
        compiler_params=_cparams(("parallel", "parallel", "arbitrary")),
        name="mla_attention",
    )(q, q, k, v)


def _conv_kernel(x_ref, prev_ref, next_ref, w_ref, b_ref, post_ref, o_ref, *, t_len):
    i = pl.program_id(0)
    x = x_ref[...]
    row = lax.broadcasted_iota(jnp.int32, (TM, 1), 0)
    pos = (i * TM + row) % t_len
    x_prev = jnp.where(row == 0, prev_ref[7:8, :], pltpu.roll(x, 1, axis=0))
    x_next = jnp.where(row == TM - 1, next_ref[0:1, :], pltpu.roll(x, TM - 1, axis=0))
    seq_start = (pos == 0) | (pos == CTX)
    seq_end = (pos == CTX - 1) | (pos == t_len - 1)
    x_prev = jnp.where(seq_start, 0.0, x_prev)
    x_next = jnp.where(seq_end, 0.0, x_next)
    y = w_ref[0:1, :] * x_prev + w_ref[1:2, :] * x + w_ref[2:3, :] * x_next + b_ref[...]
    o_ref[...] = _silu(y) * post_ref[...]


def _mlstm_conv(p0, conv_w, conv_b, post, t_len):
    rows = p0.shape[0]
    n8 = rows // 8
    cb = P0_MQ // 1024
    return pl.pallas_call(
        functools.partial(_conv_kernel, t_len=t_len),
        out_shape=jax.ShapeDtypeStruct((rows, 1024), F32),
        grid=(rows // TM,),
        in_specs=[
            pl.BlockSpec((TM, 1024), lambda i: (i, cb)),
            pl.BlockSpec((8, 1024), lambda i: (jnp.maximum(i * (TM // 8) - 1, 0), cb)),
            pl.BlockSpec((8, 1024), lambda i: (jnp.minimum((i + 1) * (TM // 8), n8 - 1), cb)),
            _const_spec((3, 1024)),
            _const_spec((1, 1024)),
            _const_spec((1, 1024)),
        ],
        out_specs=pl.BlockSpec((TM, 1024), lambda i: (i, 0)),
        compiler_params=_cparams(("parallel",)),
        name="mlstm_conv",
    )(p0, p0, p0, conv_w, conv_b, post)


def _log_sigmoid(x):
    return jnp.minimum(x, 0.0) - jnp.log(1.0 + jnp.exp(-jnp.abs(x)))


def _mlstm_kernel(qkf_ref, qkb_ref, vf_ref, vb_ref, gf_ref, gb_ref, gbias_ref, hf_ref, hb_ref, c_ref, m_ref):
    j = pl.program_id(1)
    L = M_CHUNK

    @pl.when(j == 0)
    def _():
        c_ref[...] = jnp.zeros_like(c_ref)
        m_ref[...] = jnp.zeros_like(m_ref)

    r_i = lax.broadcasted_iota(jnp.int32, (L, L), 0)
    c_i = lax.broadcasted_iota(jnp.int32, (L, L), 1)
    row = lax.broadcasted_iota(jnp.int32, (L, 128), 0)
    lane = lax.broadcasted_iota(jnp.int32, (L, 128), 1)

    for d, (qk_ref, v_ref, g_ref, h_ref) in enumerate(((qkf_ref, vf_ref, gf_ref, hf_ref),
                                                        (qkb_ref, vb_ref, gb_ref, hb_ref))):
        rev = d == 1
        keep = (c_i >= r_i) if rev else (c_i <= r_i)
        tri = jnp.where(keep, LOG2E, 0.0).astype(F32)
        pre = g_ref[...] + gbias_ref[...]
        csum = jnp.dot(tri, _log_sigmoid(pre), precision=HIGHEST, preferred_element_type=F32)
        g_run = pltpu.roll(csum, 124, axis=1)
        u = pre * LOG2E - g_run
        c_run = u
        k = 1
        while k < L:
            if rev:
                shifted = jnp.where(row >= L - k, -jnp.inf, pltpu.roll(c_run, L - k, axis=0))
            else:
                shifted = jnp.where(row < k, -jnp.inf, pltpu.roll(c_run, k, axis=0))
            c_run = jnp.maximum(c_run, shifted)
            k *= 2
        last = 0 if rev else L - 1
        m_prev = m_ref[d]
        m_run = jnp.maximum(c_run, m_prev)
        g_end = g_run[last:last + 1, :]
        m_new = g_end + jnp.maximum(m_prev, c_run[last:last + 1, :])
        w_inter_all = jnp.exp2(m_prev - m_run)
        floor_all = jnp.exp2(-(g_run + m_run))
        w_in_all = jnp.exp2(g_end + u - m_new)
        decay_all = jnp.exp2(g_end + m_prev - m_new)
        u_t = u.T
        m_ref[d] = m_new
        for h in range(M_HEADS):
            ci = 8 * d + h
            qh = qk_ref[:, h * 128:(h + 1) * 128].astype(BF16)
            kh32 = qk_ref[:, 512 + h * 128:512 + (h + 1) * 128]
            sc = lax.dot_general(qh, kh32.astype(BF16), (((1,), (1,)), ((), ())), preferred_element_type=F32)
            s = sc * jnp.exp2(jnp.where(keep, u_t[ci:ci + 1, :] - m_run[:, ci:ci + 1], -jnp.inf))
            ones_col = jnp.where(lane == ci, 1.0, 0.0).astype(BF16)
            v_ext = jnp.concatenate([v_ref[:, h * 128:(h + 1) * 128].astype(BF16), ones_col], axis=1)
            c_st = c_ref[4 * d + h]
            r1 = jnp.dot(s.astype(BF16), v_ext, preferred_element_type=F32)
            r2 = jnp.dot(qh, c_st.astype(BF16), preferred_element_type=F32)
            den = r1[:, 128:256] + w_inter_all * r2[:, 128:256]
            inv = 1.0 / jnp.maximum(jnp.abs(den), floor_all)
            h_ref[:, h * 128:(h + 1) * 128] = ((r1[:, 0:128] + w_inter_all[:, ci:ci + 1] * r2[:, 0:128])
                                               * inv[:, ci:ci + 1])
            kw = (kh32 * w_in_all[:, ci:ci + 1]).astype(BF16)
            upd = lax.dot_general(kw, v_ext, (((0,), (0,)), ((), ())), preferred_element_type=F32)
            c_ref[4 * d + h] = decay_all[:, ci:ci + 1] * c_st + upd


def _mlstm_scan(qk, p0, gate_bias, batch, t_len):
    rows = qk.shape[0]
    nch = t_len // M_CHUNK
    nctx = CTX // M_CHUNK
    fwd = lambda b, j: b * nch + j
    bwd = lambda b, j: b * nch + jnp.where(j < nctx, nctx - 1 - j, nch + nctx - 1 - j)
    out = jax.ShapeDtypeStruct((rows, M_HEADS * M_V), F32)
    return pl.pallas_call(
        _mlstm_kernel,
        out_shape=(out, out),
        grid=(batch, nch),
        in_specs=[
            pl.BlockSpec((M_CHUNK, 1024), lambda b, j: (fwd(b, j), 0)),
            pl.BlockSpec((M_CHUNK, 1024), lambda b, j: (bwd(b, j), 0)),
            pl.BlockSpec((M_CHUNK, 512), lambda b, j: (fwd(b, j), P0_MV // 512)),
            pl.BlockSpec((M_CHUNK, 512), lambda b, j: (bwd(b, j), P0_MV // 512)),
            pl.BlockSpec((M_CHUNK, 128), lambda b, j: (fwd(b, j), P0_GATE // 128)),
            pl.BlockSpec((M_CHUNK, 128), lambda b, j: (bwd(b, j), P0_GATE // 128)),
            _const_spec((1, 128)),
        ],
        out_specs=(
            pl.BlockSpec((M_CHUNK, 512), lambda b, j: (fwd(b, j), 0)),
            pl.BlockSpec((M_CHUNK, 512), lambda b, j: (bwd(b, j), 0)),
        ),
        scratch_shapes=[
            pltpu.VMEM((2 * M_HEADS, M_QKP, 256), F32),
            pltpu.VMEM((2, 1, 128), F32),
        ],
        compiler_params=_cparams(("parallel", "arbitrary")),
        name="mlstm_scan",
    )(qk, qk, p0, p0, p0, p0, gate_bias)


def _head_rms(x, gain_row, n_heads, width):
    parts = []
    for h in range(n_heads):
        sl = slice(h * width, (h + 1) * width)
        parts.append(_rms(x[:, sl], gain_row[:, sl]))
    return jnp.concatenate(parts, axis=1)


def _ab_out_kernel(x_ref, a_ref, hf_ref, hb_ref, og_ref, hg_ref, wa_ref, wm_ref, gains_ref, tab_ref, o_ref, *, per):
    i = pl.program_id(0)
    m = _head_rms(hf_ref[...] + hb_ref[...], hg_ref[...], M_HEADS, M_V) * _sigmoid(og_ref[...])
    y = (jnp.dot(a_ref[...].astype(BF16), wa_ref[...], preferred_element_type=F32)
         + jnp.dot(m.astype(BF16), wm_ref[...], preferred_element_type=F32))
    _gated_residual_tile(x_ref, y, gains_ref[1:2, :], tab_ref, 2, i, per, TM, o_ref)


def _ab_out(xs, a, hf, hb, p0, head_gain, wa, wm, gains, tab, per):
    rows = xs.shape[0]
    row_spec = lambda w, cb=0: pl.BlockSpec((TM, w), lambda i: (i, cb))
    return pl.pallas_call(
        functools.partial(_ab_out_kernel, per=per),
        out_shape=jax.ShapeDtypeStruct((rows, D), F32),
        grid=(rows // TM,),
        in_specs=[
            row_spec(D), row_spec(512), row_spec(512), row_spec(512), row_spec(512, P0_OG // 512),
            _const_spec((1, 512)), _const_spec((512, D)), _const_spec((512, D)),
            _const_spec((4, D)), _const_spec((6, 8, D)),
        ],
        out_specs=row_spec(D),
        compiler_params=_cparams(("parallel",)),
        name="ab_out",
    )(xs, a, hf, hb, p0, head_gain, wa, wm, gains, tab)


def _swiglu(h, wgu_ref, wdn_ref, lead):
    acc = None
    for c0 in range(0, D_FF, FF_CHUNK):
        g = jnp.dot(h, wgu_ref[lead + (slice(None), slice(c0, c0 + FF_CHUNK))], preferred_element_type=F32)
        u = jnp.dot(h, wgu_ref[lead + (slice(None), slice(D_FF + c0, D_FF + c0 + FF_CHUNK))],
                    preferred_element_type=F32)
        a = (_silu(g) * u).astype(BF16)
        part = jnp.dot(a, wdn_ref[lead + (slice(c0, c0 + FF_CHUNK), slice(None))], preferred_element_type=F32)
        acc = part if acc is None else acc + part
    return acc


def _ffn_kernel(x_ref, gains_ref, tab_ref, wgu_ref, wdn_ref, o_ref, *, per):
    i = pl.program_id(0)
    h = _norm_mod_tile(x_ref, gains_ref[2:3, :], tab_ref, 3, 4, i, per, TM).astype(BF16)
    y = _swiglu(h, wgu_ref, wdn_ref, ())
    _gated_residual_tile(x_ref, y, gains_ref[3:4, :], tab_ref, 5, i, per, TM, o_ref)


def _ffn(xs, gains, tab, wgu, wdn, per):
    rows = xs.shape[0]
    return pl.pallas_call(
        functools.partial(_ffn_kernel, per=per),
        out_shape=jax.ShapeDtypeStruct((rows, D), F32),
        grid=(rows // TM,),
        in_specs=[
            pl.BlockSpec((TM, D), lambda i: (i, 0)),
            _const_spec((4, D)), _const_spec((6, 8, D)),
            _const_spec((D, 2 * D_FF)), _const_spec((D_FF, D)),
        ],
        out_specs=pl.BlockSpec((TM, D), lambda i: (i, 0)),
        compiler_params=_cparams(("parallel",)),
        name="ffn",
    )(xs, gains, tab, wgu, wdn)


def _hgrn_stream(q, v, g_ref, w_ref, sl, st, gsum, neg, lmask, rev):
    c = G_SUB
    ns = G_CHUNK // c
    blk = lambda ref, i: ref[c * i:c * (i + 1), sl]
    bcast = lambda ref, r: jnp.broadcast_to(ref[r:r + 1, sl], (c, 128))
    qb = lambda i: q[c * i:c * (i + 1)]
    bound = [bcast(g_ref, c * j if rev else c * j + c - 1) for j in range(ns)]
    khat = jnp.concatenate([jnp.exp2(bound[j] - blk(w_ref, j)) for j in range(ns)], axis=0)
    pairs = [(i, j) for i in range(ns) for j in range(ns) if (i < j if rev else i > j)]
    qst = jnp.concatenate([qb(i) * jnp.exp2(blk(g_ref, i) - bound[j]) for (i, j) in pairs], axis=0)
    cross = lax.dot_general(qst.astype(BF16), khat.astype(BF16), (((1,), (1,)), ((), ())),
                            preferred_element_type=F32)
    z_rows = []
    for i in range(ns):
        gi, qi = blk(g_ref, i), qb(i)
        z_rows.append(jnp.concatenate(
            [qi * jnp.exp2(gi + neg[s] - bcast(w_ref, c * i + s)) for s in range(c)], axis=1))
    diag = jnp.dot(jnp.concatenate(z_rows, axis=0).astype(BF16), gsum, preferred_element_type=F32)
    a_rows = []
    for i in range(ns):
        acc = diag[c * i:c * (i + 1)] * lmask[i]
        for p, (ii, j) in enumerate(pairs):
            if ii == i:
                acc = acc + cross[c * p:c * (p + 1)] * lmask[j]
        a_rows.append(acc)
    a = jnp.concatenate(a_rows, axis=0).astype(BF16)
    g_end = g_ref[0:1, sl] if rev else g_ref[G_CHUNK - 1:G_CHUNK, sl]
    o = (jnp.dot(a, v.astype(BF16), preferred_element_type=F32)
         + lax.dot_general((q * jnp.exp2(g_ref[:, sl])).astype(BF16), st.astype(BF16), (((1,), (1,)), ((), ())),
                           preferred_element_type=F32))
    kw = jnp.exp2(g_end - w_ref[:, sl]).astype(BF16)
    st_new = st * jnp.exp2(g_end) + lax.dot_general(v.astype(BF16), kw, (((0,), (0,)), ((), ())),
                                                     preferred_element_type=F32)
    return o, st_new


def _hgrn_kernel(qf_ref, zf_ref, vf_ref, qb_ref, zb_ref, vb_ref, lb_ref, of_ref, ob_ref, st_ref, g_ref, w_ref):
    j = pl.program_id(1)
    L = G_CHUNK

    @pl.when(j == 0)
    def _():
        st_ref[...] = jnp.zeros_like(st_ref)

    r_i = lax.broadcasted_iota(jnp.int32, (L, L), 0)
    c_i = lax.broadcasted_iota(jnp.int32, (L, L), 1)
    lb = lb_ref[...]
    log_1mlb = jnp.log(1.0 - lb)
    gs_r = lax.broadcasted_iota(jnp.int32, (G_SUB * 128, L), 0)
    gs_c = lax.broadcasted_iota(jnp.int32, (G_SUB * 128, L), 1)
    gsum = jnp.where(gs_r // 128 == gs_c % G_SUB, 1.0, 0.0).astype(BF16)
    sub = lax.broadcasted_iota(jnp.int32, (G_SUB, 128), 0)
    lane = lax.broadcasted_iota(jnp.int32, (G_SUB, L), 1)
    lmask = [jnp.where(lane // G_SUB == jj, 1.0, 0.0) for jj in range(L // G_SUB)]
    for d, (q_ref, z_ref, v_ref, o_ref) in enumerate(((qf_ref, zf_ref, vf_ref, of_ref),
                                                       (qb_ref, zb_ref, vb_ref, ob_ref))):
        rev = d == 1
        neg = [jnp.where((sub <= s) if rev else (sub >= s), 0.0, -jnp.inf) for s in range(G_SUB)]
        tri = jnp.where((c_i >= r_i) if rev else (c_i <= r_i), LOG2E, 0.0).astype(F32)
        z = z_ref[...]
        e = jnp.exp(-jnp.abs(z))
        r = 1.0 / (1.0 + e)
        sig_pos = jnp.where(z >= 0, r, e * r)
        log_f = jnp.log(lb + (1.0 - lb) * sig_pos)
        log_k = log_1mlb - jnp.maximum(z, 0.0) - jnp.log(1.0 + e)
        g_all = jnp.dot(tri, log_f, precision=HIGHEST, preferred_element_type=F32)
        g_ref[d] = g_all
        w_ref[d] = g_all - LOG2E * log_k
        q_all = _silu(q_ref[...])
        for h in range(G_HEADS):
            sl = slice(h * 128, (h + 1) * 128)
            o, st_new = _hgrn_stream(q_all[:, sl], v_ref[:, sl], g_ref.at[d], w_ref.at[d], sl, st_ref[d * G_HEADS + h],
                                     gsum, neg, lmask, rev)
            o_ref[:, sl] = o
            st_ref[d * G_HEADS + h] = st_new


def _hgrn_scan(p1, lb_row, batch, t_len):
    rows = p1.shape[0]
    nch = t_len // G_CHUNK
    nctx = CTX // G_CHUNK
    fwd = lambda b, j: b * nch + j
    bwd = lambda b, j: b * nch + jnp.where(j < nctx, nctx - 1 - j, nch + nctx - 1 - j)
    spec = lambda order, cb: pl.BlockSpec((G_CHUNK, 1024), lambda b, j: (order(b, j), cb))
    out = jax.ShapeDtypeStruct((rows, G_HEADS * G_DV), F32)
    return pl.pallas_call(
        _hgrn_kernel,
        out_shape=(out, out),
        grid=(batch, nch),
        in_specs=[spec(fwd, 0), spec(fwd, 1), spec(fwd, 3), spec(bwd, 0), spec(bwd, 2), spec(bwd, 3),
                  _const_spec((1, 1024))],
        out_specs=(spec(fwd, 0), spec(bwd, 0)),
        scratch_shapes=[pltpu.VMEM((2 * G_HEADS, G_DV, G_DK), F32),
                        pltpu.VMEM((2, G_CHUNK, 1024), F32),
                        pltpu.VMEM((2, G_CHUNK, 1024), F32)],
        compiler_params=_cparams(("parallel", "arbitrary")),
        name="hgrn_scan",
    )(p1, p1, p1, p1, p1, p1, lb_row)


def _c_out_kernel(x_ref, of_ref, ob_ref, hg_ref, gain_ref, w_ref, gains_ref, tab_ref, o_ref, *, per):
    i = pl.program_id(0)
    o = _head_rms(of_ref[...] + ob_ref[...], gain_ref[...], G_HEADS, G_DV) * _silu(hg_ref[...])
    y = jnp.dot(o.astype(BF16), w_ref[...], preferred_element_type=F32)
    _gated_residual_tile(x_ref, y, gains_ref[1:2, :], tab_ref, 2, i, per, TM, o_ref)


def _c_out(xs, of, ob, p1, head_gain, w, gains, tab, per):
    rows = xs.shape[0]
    row_spec = lambda cb=0: pl.BlockSpec((TM, D), lambda i: (i, cb))
    return pl.pallas_call(
        functools.partial(_c_out_kernel, per=per),
        out_shape=jax.ShapeDtypeStruct((rows, D), F32),
        grid=(rows // TM,),
        in_specs=[row_spec(), row_spec(), row_spec(), row_spec(4), _const_spec((1, D)), _const_spec((D, D)),
                  _const_spec((4, D)), _const_spec((6, 8, D))],
        out_specs=row_spec(),
        compiler_params=_cparams(("parallel",)),
        name="c_out",
    )(xs, of, ob, p1, head_gain, w, gains, tab)


def _router_kernel(x_ref, gains_ref, tab_ref, wr_ref, h_ref, r_ref, *, per):
    i = pl.program_id(0)
    h = _norm_mod_tile(x_ref, gains_ref[2:3, :], tab_ref, 3, 4, i, per, TM)
    for j in range(D // 128):
        h_ref[pl.ds(j, TM, stride=D // 128), :] = h[:, j * 128:(j + 1) * 128]
    logits = lax.dot_general(wr_ref[...], h, (((1,), (1,)), ((), ())), precision=HIGHEST,
                             preferred_element_type=F32)
    eid = lax.broadcasted_iota(jnp.int32, logits.shape, 0).astype(F32)
    m1 = jnp.max(logits, axis=0, keepdims=True)
    i1 = jnp.min(jnp.where(logits == m1, eid, float(N_EXP)), axis=0, keepdims=True)
    rest = jnp.where(eid == i1, -jnp.inf, logits)
    m2 = jnp.max(rest, axis=0, keepdims=True)
    i2 = jnp.min(jnp.where(rest == m2, eid, float(N_EXP)), axis=0, keepdims=True)
    e2 = jnp.exp(m2 - m1)
    g1 = 1.0 / (1.0 + e2)
    g2 = e2 * g1
    rid = lax.broadcasted_iota(jnp.int32, (128, TM), 0)
    packed = jnp.where(rid == 0, i1, jnp.where(rid == 1, i2, jnp.where(rid == 2, g1, jnp.where(rid == 3, g2, 0.0))))
    r_ref[...] = packed.T


def _router(xs, gains, tab, wr_t, per):
    rows = xs.shape[0]
    return pl.pallas_call(
        functools.partial(_router_kernel, per=per),
        out_shape=(jax.ShapeDtypeStruct((rows * (D // 128), 128), F32), jax.ShapeDtypeStruct((rows, 128), F32)),
        grid=(rows // TM,),
        in_specs=[pl.BlockSpec((TM, D), lambda i: (i, 0)), _const_spec((4, D)), _const_spec((6, 8, D)),
                  _const_spec((N_EXP, D))],
        out_specs=(pl.BlockSpec((TM * (D // 128), 128), lambda i: (i, 0)), pl.BlockSpec((TM, 128), lambda i: (i, 0))),
        compiler_params=_cparams(("parallel",)),
        name="moe_router",
    )(xs, gains, tab, wr_t)


def _tile_rows(ref, index):
    return jnp.concatenate([ref[index(j)] for j in range(D // 128)], axis=1)


def _expert_kernel(be_ref, nact_ref, tok_ref, dstx_ref, h_hbm, wgu_ref, wdn_ref, yg_hbm,
                   xbuf, ybuf, gsem, ssem):
    b = pl.program_id(0)
    nb = pl.num_programs(0)
    slot = b % 2
    other = 1 - slot
    nxt = jnp.minimum(b + 1, nb - 1)

    def gather_rows(blk, dst_slot, r0, r1):
        for r in range(r0, r1):
            src = h_hbm.at[pl.ds(pl.multiple_of(tok_ref[blk * MOE_BLOCK + r], 8), 8)]
            pltpu.make_async_copy(src, xbuf.at[dst_slot, pl.ds(8 * r, 8)], gsem.at[dst_slot]).start(priority=1)

    def scatter_rows(blk, src_slot, r0, r1):
        for r in range(r0, r1):
            dst = yg_hbm.at[pl.ds(pl.multiple_of(dstx_ref[blk * MOE_BLOCK + r], 8), 8)]
            pltpu.make_async_copy(ybuf.at[src_slot, pl.ds(8 * r, 8)], dst, ssem.at[src_slot]).start()

    def wait_gather(s):
        pltpu.make_async_copy(h_hbm.at[pl.ds(0, MOE_BLOCK * (D // 128))], xbuf.at[s], gsem.at[s]).wait()

    def wait_scatter(s):
        pltpu.make_async_copy(ybuf.at[s], yg_hbm.at[pl.ds(0, MOE_BLOCK * (D // 128))], ssem.at[s]).wait()

    @pl.when(b == 0)
    def _():
        ybuf[...] = jnp.zeros_like(ybuf)
        gather_rows(0, 0, 0, MOE_BLOCK)

    wait_gather(slot)

    @pl.when(b >= 1)
    def _():
        wait_scatter(slot)

    n_chunks = D_FF // FF_CHUNK
    per_g = -(-MOE_BLOCK // (n_chunks // 2))
    per_s = -(-MOE_BLOCK // (n_chunks - n_chunks // 2))

    @pl.when(b < nact_ref[0])
    def _():
        x = _tile_rows(xbuf, lambda j: (slot, pl.ds(j, MOE_BLOCK, stride=D // 128), slice(None))).astype(BF16)
        acc = None
        for ci in range(n_chunks):
            c0 = ci * FF_CHUNK
            g = jnp.dot(x, wgu_ref[0, :, c0:c0 + FF_CHUNK], preferred_element_type=F32)
            u = jnp.dot(x, wgu_ref[0, :, D_FF + c0:D_FF + c0 + FF_CHUNK], preferred_element_type=F32)
            a = (_silu(g) * u).astype(BF16)
            part = jnp.dot(a, wdn_ref[0, c0:c0 + FF_CHUNK, :], preferred_element_type=F32)
            acc = part if acc is None else acc + part
            if ci < n_chunks // 2:
                gather_rows(nxt, other, ci * per_g, min((ci + 1) * per_g, MOE_BLOCK))
            else:
                cs = ci - n_chunks // 2
                scatter_rows(b, other, cs * per_s, min((cs + 1) * per_s, MOE_BLOCK))
        for j in range(D // 128):
            ybuf[slot, pl.ds(j, MOE_BLOCK, stride=D // 128), :] = acc[:, j * 128:(j + 1) * 128]

    @pl.when(b >= nact_ref[0])
    def _():
        gather_rows(nxt, other, 0, MOE_BLOCK)
        scatter_rows(b, other, 0, MOE_BLOCK)

    @pl.when(b == nb - 1)
    def _():
        wait_gather(other)
        wait_scatter(other)
        scatter_rows(nb, slot, 0, MOE_BLOCK)
        wait_scatter(slot)


def _experts(block_expert, n_active, tok3, dst3, h3, wgu, wdn, n_rows_out):
    n_blocks = block_expert.shape[0]
    grid_spec = pltpu.PrefetchScalarGridSpec(
        num_scalar_prefetch=4,
        grid=(n_blocks,),
        in_specs=[
            pl.BlockSpec(memory_space=pl.ANY),
            pl.BlockSpec((1, D, 2 * D_FF), lambda b, be, na, tk, ds: (be[b], 0, 0)),
            pl.BlockSpec((1, D_FF, D), lambda b, be, na, tk, ds: (be[b], 0, 0)),
        ],
        out_specs=pl.BlockSpec(memory_space=pl.ANY),
        scratch_shapes=[pltpu.VMEM((2, MOE_BLOCK * (D // 128), 128), F32), pltpu.VMEM((2, MOE_BLOCK * (D // 128), 128), F32),
                        pltpu.SemaphoreType.DMA((2,)), pltpu.SemaphoreType.DMA((2,))],
    )
    return pl.pallas_call(
        _expert_kernel,
        out_shape=jax.ShapeDtypeStruct((n_rows_out * (D // 128), 128), F32),
        grid_spec=grid_spec,
        compiler_params=_cparams(("arbitrary",)),
        name="moe_experts",
    )(block_expert, n_active, tok3.reshape(-1), dst3.reshape(-1), h3, wgu, wdn)


def _combine_kernel(yg_ref, r_ref, x_ref, gains_ref, tab_ref, o_ref, *, per):
    tile = pl.program_id(0) * per + 1 + pl.program_id(1)
    gates = r_ref[...]
    rows = lambda k: _tile_rows(yg_ref, lambda j: (pl.ds(k * (D // 128) + j, HALF, stride=2 * (D // 128)), slice(None)))
    y = gates[:, 2:3] * rows(0) + gates[:, 3:4] * rows(1)
    idx = _mod_index(tile, per)
    gate = tab_ref[5, pl.ds(idx, 1), :]
    o_ref[...] = x_ref[...] + gate * _rms(y, gains_ref[3:4, :])


def _combine_latents(yg, rout, xs, gains, tab, batch, s_len, per):
    nl = s_len // HALF
    tile = lambda b, j: b * per + 1 + j
    return pl.pallas_call(
        functools.partial(_combine_kernel, per=per),
        out_shape=jax.ShapeDtypeStruct((batch * s_len, D), F32),
        grid=(batch, nl),
        in_specs=[
            pl.BlockSpec((HALF * 2 * (D // 128), 128), lambda b, j: (tile(b, j), 0)),
            pl.BlockSpec((HALF, 128), lambda b, j: (tile(b, j), 0)),
            pl.BlockSpec((HALF, D), lambda b, j: (tile(b, j), 0)),
            _const_spec((4, D)), _const_spec((6, 8, D)),
        ],
        out_specs=pl.BlockSpec((HALF, D), lambda b, j: (b * nl + j, 0)),
        compiler_params=_cparams(("parallel", "parallel")),
        name="moe_combine",
    )(yg, rout, xs, gains, tab)


def _moe_plan(rout):
    n_tok = rout.shape[0]
    n_flat = 2 * n_tok
    n_blocks = -(-n_flat // MOE_BLOCK) + N_EXP
    assert n_blocks * MOE_BLOCK == n_flat + N_EXP * MOE_BLOCK and n_flat < 2 ** 19
    flat_e = rout[:, 0:2].astype(jnp.int32).reshape(-1)
    experts = jnp.arange(N_EXP, dtype=jnp.int32)
    counts = jnp.sum((flat_e[:, None] == experts[None, :]).astype(jnp.int32), axis=0)
    padded = (counts + MOE_BLOCK - 1) // MOE_BLOCK * MOE_BLOCK
    pend = jnp.cumsum(padded)
    blk = jnp.arange(n_blocks, dtype=jnp.int32)
    block_expert = jnp.minimum(jnp.searchsorted(pend, blk * MOE_BLOCK, side='right'), N_EXP - 1).astype(jnp.int32)
    n_active = (pend[-1] // MOE_BLOCK).astype(jnp.int32).reshape(1)
    flat = jnp.arange(n_flat, dtype=jnp.int32)
    pad_i = jnp.arange(MOE_BLOCK, dtype=jnp.int32)[None, :]
    pad_key = jnp.where(pad_i < (padded - counts)[:, None], experts[:, None] * 2 ** 20 + 2 ** 19 + pad_i,
                        jnp.iinfo(jnp.int32).max)
    keys = jnp.sort(jnp.concatenate([flat_e * 2 ** 20 + flat, pad_key.reshape(-1)]))
    rows = (keys & (2 ** 20 - 1)).reshape(n_blocks, MOE_BLOCK)
    valid = rows < 2 ** 19
    tok3 = (jnp.where(valid, rows // 2, 0) * (D // 128)).reshape(n_blocks, 1, MOE_BLOCK)
    dummy = n_flat + pad_i
    dst = jnp.where(valid, rows, dummy)
    dst3 = (jnp.concatenate([dummy, dst], axis=0) * (D // 128)).reshape(n_blocks + 1, 1, MOE_BLOCK)
    return block_expert, n_active, tok3, dst3, n_flat + MOE_BLOCK


def _rot_half_cols(w):
    half = A_ROPE // 2
    g = w.reshape(w.shape[0], -1, A_ROPE)
    return jnp.concatenate([-g[..., half:], g[..., :half]], axis=-1).reshape(w.shape)


def _pad_groups(w, width, to):
    k = w.shape[0]
    g = w.reshape(k, -1, width)
    return jnp.pad(g, ((0, 0), (0, 0), (0, to - width))).reshape(k, -1)


def _layer0_inproj_weight(w_in):
    splits = np.cumsum([A_QLORA, A_KVLORA, A_ROPE, M_HEADS * M_QK, M_HEADS * M_QK, M_HEADS * M_V, M_HEADS * M_V])
    cq, ckv, kr, mq, mk, mv, og, gt = jnp.split(w_in, [int(s) for s in splits], axis=1)
    pad64 = lambda a: jnp.pad(a, ((0, 0), (0, 64)))
    cols = [cq, ckv, pad64(kr), mv, _pad_groups(mq, M_QK, M_QKP), _pad_groups(mk, M_QK, M_QKP), og,
            jnp.pad(gt, ((0, 0), (0, 128 - gt.shape[1]))), pad64(_rot_half_cols(kr))]
    w = jnp.concatenate(cols, axis=1)
    assert w.shape[1] == P0_N
    return w.astype(BF16)


def _rope_tables(batch, s_len):
    t = jnp.arange(s_len)
    row = (t // GRID_W).astype(F32)
    col = (t % GRID_W).astype(F32)
    n_freq = A_ROPE // 4
    inv_freq = ROPE_THETA ** (-jnp.arange(n_freq, dtype=F32) / n_freq)
    ang = jnp.concatenate([row[:, None] * inv_freq, col[:, None] * inv_freq], axis=-1)
    ang = jnp.concatenate([ang, ang], axis=-1)

    def table(vals, fill):
        lat = jnp.pad(vals, ((0, 0), (0, 128 - A_ROPE)), constant_values=fill)
        one = jnp.concatenate([jnp.full((CTX, 128), fill, F32), lat], axis=0)
        return jnp.tile(one, (batch, 1))

    return table(jnp.cos(ang), 1.0), table(jnp.sin(ang), 0.0)


def kernel(x, c, ctx, c_ctx, w_mod, b_mod, norm_gains, w_in_ab, mla_q_gain, mla_w_uq, mla_kv_gain, mla_w_ukv,
           ml_conv_w, ml_conv_b, ml_gate_b, ml_head_gain, w_out_ab, ffn_w_gu, ffn_w_dn, w_in_c, hg_lb_logits,
           hg_head_gain, w_out_c, moe_router, moe_w_gu, moe_w_dn):
    batch, s_len, d = x.shape
    assert d == D and ctx.shape[1] == CTX and batch == 2
    assert s_len % 512 == 0 and (batch * (CTX + s_len)) % TM == 0
    t_len = CTX + s_len
    per = t_len // HALF

    xs = jnp.concatenate([ctx, x], axis=1).reshape(batch * t_len, D)
    cc = jnp.concatenate([c, c_ctx[None, :], jnp.zeros((8 - batch - 1, D), F32)], axis=0)
    tabs = _mod_table(cc, w_mod, b_mod)

    gains = norm_gains[0]
    tab = tabs[0]
    p0 = _inproj(xs, gains, tab, _layer0_inproj_weight(w_in_ab[0]), per, P0_N // 2)

    cos, sin = _rope_tables(batch, s_len)
    w_uq = mla_w_uq[0].reshape(A_QLORA, A_HEADS, A_NOPE + A_ROPE)
    wq_nope = w_uq[:, :, :A_NOPE].reshape(A_QLORA, A_HEADS * A_NOPE)
    wq_rope = w_uq[:, :, A_NOPE:].reshape(A_QLORA, A_HEADS * A_ROPE)
    w_ukv = mla_w_ukv[0].reshape(A_KVLORA, A_HEADS, A_NOPE + A_V)
    q, k, v = _mla_prep(
        p0, cos, sin, mla_q_gain[0][None, :], mla_kv_gain[0][None, :],
        wq_nope.astype(BF16),
        _pad_groups(wq_rope, A_ROPE, 128).astype(BF16),
        _pad_groups(_rot_half_cols(wq_rope), A_ROPE, 128).astype(BF16),
        w_ukv[:, :, :A_NOPE].reshape(A_KVLORA, -1).astype(BF16),
        w_ukv[:, :, A_NOPE:].reshape(A_KVLORA, -1).astype(BF16),
        batch, t_len)
    attn = _attention(q, k, v).reshape(batch * t_len, A_HEADS * A_V)

    conv_w = jnp.concatenate([_pad_groups(ml_conv_w[0][:, :M_HEADS * M_QK], M_QK, M_QKP),
                              _pad_groups(ml_conv_w[0][:, M_HEADS * M_QK:], M_QK, M_QKP)], axis=1)
    conv_b = jnp.concatenate([_pad_groups(ml_conv_b[0][None, :M_HEADS * M_QK], M_QK, M_QKP),
                              _pad_groups(ml_conv_b[0][None, M_HEADS * M_QK:], M_QK, M_QKP)], axis=1)
    post = jnp.concatenate([jnp.ones((1, 512), F32), jnp.full((1, 512), M_QK ** -0.5, F32)], axis=1)
    qk = _mlstm_conv(p0, conv_w, conv_b, post, t_len)
    gate_bias = jnp.pad(ml_gate_b[0][None, :], ((0, 0), (0, 128 - 4 * M_HEADS)))
    hf, hb = _mlstm_scan(qk, p0, gate_bias, batch, t_len)

    w_out = w_out_ab[0].astype(BF16)
    xs = _ab_out(xs, attn, hf, hb, p0, ml_head_gain[0][None, :], w_out[:A_HEADS * A_V], w_out[A_HEADS * A_V:],
                 gains, tab, per)
    xs = _ffn(xs, gains, tab, ffn_w_gu[0].astype(BF16), ffn_w_dn[0].astype(BF16), per)

    gains = norm_gains[1]
    tab = tabs[1]
    lbs = jnp.cumsum(jax.nn.softmax(hg_lb_logits.astype(F32), axis=0), axis=0)
    lb_row = (lbs - lbs[0])[1][None, :]
    p1 = _inproj(xs, gains, tab, w_in_c[0].astype(BF16), per, 1024)
    of, ob = _hgrn_scan(p1, lb_row, batch, t_len)
    xs = _c_out(xs, of, ob, p1, hg_head_gain[0][None, :], w_out_c[0].astype(BF16), gains, tab, per)

    h, rout = _router(xs, gains, tab, moe_router[0].T, per)
    block_expert, n_active, tok3, dst3, n_rows_out = _moe_plan(rout)
    yg = _experts(block_expert, n_active, tok3, dst3, h, moe_w_gu[0].astype(BF16), moe_w_dn[0].astype(BF16), n_rows_out)
    out = _combine_latents(yg, rout, xs, gains, tab, batch, s_len, per)
    return out.reshape(batch, s_len, D)
```

```python
import functools

import jax
import jax.numpy as jnp
import numpy as np
from jax import lax
from jax.experimental import pallas as pl
from jax.experimental.pallas import tpu as pltpu

F32 = jnp.float32
BF16 = jnp.bfloat16
HIGHEST = lax.Precision.HIGHEST
LOG2E = 1.4426950408889634

D = 1024
CTX = 256
EPS = 1e-6
HALF = 256
TM = 512
VMEM_LIMIT = 56 * 1024 * 1024

A_HEADS = 4
A_QLORA = 256
A_KVLORA = 128
A_NOPE = 128
A_ROPE = 64
A_V = 128
A_HD = 256
A_SCALE = (A_NOPE + A_ROPE) ** -0.5
ROPE_THETA = 10000.0
GRID_W = 64
A_QSCALE = A_SCALE * LOG2E
A_TQ = 256
A_TK = 4096
A_RS = 32

M_HEADS = 4
M_QK = 64
M_V = 128
M_CHUNK = 128
M_STEP = 2
M_QKP = 128

G_HEADS = 8
G_DK = 128
G_DV = 128
G_CHUNK = 64
G_STEP = 4
G_SUB = 8

D_FF = 2816
N_EXP = 8
MOE_BLOCK = 512
FF_CHUNK = 256

P0_CQ = 0
P0_CKV = 256
P0_KR = 384
P0_MV = 512
P0_MQ = 1024
P0_MK = 1536
P0_OG = 2048
P0_GATE = 2560
P0_KRR = 2688
P0_N = 2816


def _cparams(sem):
    return pltpu.CompilerParams(dimension_semantics=sem, vmem_limit_bytes=VMEM_LIMIT)


def _const_spec(shape):
    nd = len(shape)
    return pl.BlockSpec(shape, lambda *_: (0,) * nd, pipeline_mode=pl.Buffered(1))


def _rms(x, gain_row):
    ms = jnp.mean(x * x, axis=-1, keepdims=True)
    return x * lax.rsqrt(ms + EPS) * gain_row


def _mod_index(half_idx, per):
    return jnp.where(half_idx % per == 0, 2, half_idx // per)


def _silu(x):
    return x * _sigmoid(x)


def _sigmoid(x):
    return 1.0 / (1.0 + jnp.exp(-x))


def _norm_mod_rows(x, gain_row, tab_ref, k_shift, k_scale, idx):
    shift = tab_ref[k_shift, pl.ds(idx, 1), :]
    scale = tab_ref[k_scale, pl.ds(idx, 1), :]
    return _rms(x, gain_row) * (1.0 + scale) + shift


def _norm_mod_tile(x_ref, gain_row, tab_ref, k_shift, k_scale, tile, per, rows):
    parts = []
    for s in range(rows // HALF):
        idx = _mod_index(tile * (rows // HALF) + s, per)
        parts.append(_norm_mod_rows(x_ref[s * HALF:(s + 1) * HALF, :], gain_row, tab_ref, k_shift, k_scale, idx))
    return parts[0] if len(parts) == 1 else jnp.concatenate(parts, axis=0)


def _gated_residual_tile(x_ref, y, gain_row, tab_ref, k_gate, tile, per, rows, o_ref):
    yn = _rms(y, gain_row)
    for s in range(rows // HALF):
        idx = _mod_index(tile * (rows // HALF) + s, per)
        gate = tab_ref[k_gate, pl.ds(idx, 1), :]
        sl = slice(s * HALF, (s + 1) * HALF)
        o_ref[sl, :] = x_ref[sl, :] + gate * yn[sl, :]


def _mod_kernel(c_ref, w_ref, b_ref, o_ref):
    a = _silu(c_ref[...])
    o_ref[0, 0] = jnp.dot(a, w_ref[0], precision=HIGHEST, preferred_element_type=F32) + b_ref[0, 0]


def _mod_table(cc, w_mod, b_mod):
    depth = w_mod.shape[0]
    return pl.pallas_call(
        _mod_kernel,
        out_shape=jax.ShapeDtypeStruct((depth, 6, 8, D), F32),
        grid=(depth, 6),
        in_specs=[
            pl.BlockSpec((8, D), lambda l, k: (0, 0)),
            pl.BlockSpec((1, D, D), lambda l, k: (l, 0, k)),
            pl.BlockSpec((1, 1, 1, D), lambda l, k: (l, k, 0, 0)),
        ],
        out_specs=pl.BlockSpec((1, 1, 8, D), lambda l, k: (l, k, 0, 0)),
        compiler_params=_cparams(("arbitrary", "arbitrary")),
        name="mod_table",
    )(cc, w_mod, b_mod.reshape(depth, 6, 1, D))


def _inproj_kernel(x_ref, gains_ref, tab_ref, w_ref, o_ref, *, per, nchunk):
    i = pl.program_id(0)
    h = _norm_mod_tile(x_ref, gains_ref[0:1, :], tab_ref, 0, 1, i, per, TM).astype(BF16)
    n = w_ref.shape[1]
    for n0 in range(0, n, nchunk):
        o_ref[:, n0:n0 + nchunk] = jnp.dot(h, w_ref[:, n0:n0 + nchunk], preferred_element_type=F32)


def _inproj(xs, gains, tab, w, per, nchunk):
    rows = xs.shape[0]
    n = w.shape[1]
    return pl.pallas_call(
        functools.partial(_inproj_kernel, per=per, nchunk=nchunk),
        out_shape=jax.ShapeDtypeStruct((rows, n), F32),
        grid=(rows // TM,),
        in_specs=[
            pl.BlockSpec((TM, D), lambda i: (i, 0)),
            _const_spec((4, D)),
            _const_spec((6, 8, D)),
            _const_spec((D, n)),
        ],
        out_specs=pl.BlockSpec((TM, n), lambda i: (i, 0)),
        compiler_params=_cparams(("parallel",)),
        name="inproj",
    )(xs, gains, tab, w)


def _mla_prep_kernel(cq_ref, ckr_ref, krr_ref, cos_ref, sin_ref, qg_ref, kvg_ref, wqn_ref, wqr_ref, wqrr_ref,
                     wk_ref, wv_ref, q_ref, k_ref, v_ref):
    qn = _rms(cq_ref[...], qg_ref[...]).astype(BF16)
    ckr = ckr_ref[...]
    kvn = _rms(ckr[:, :A_KVLORA], kvg_ref[...]).astype(BF16)
    cos = cos_ref[...]
    sin = sin_ref[...]
    q_nope = jnp.dot(qn, wqn_ref[...], preferred_element_type=F32)
    q_r = jnp.dot(qn, wqr_ref[...], preferred_element_type=F32)
    q_rr = jnp.dot(qn, wqrr_ref[...], preferred_element_type=F32)
    k_nope = jnp.dot(kvn, wk_ref[...], preferred_element_type=F32)
    vv = jnp.dot(kvn, wv_ref[...], preferred_element_type=F32)
    k_rope = (ckr[:, A_KVLORA:] * cos + krr_ref[...] * sin).astype(BF16)
    lane = lax.broadcasted_iota(jnp.int32, (HALF, A_V), 1)
    ones_col = jnp.where(lane == 0, 1.0, 0.0).astype(BF16)
    for h in range(A_HEADS):
        sl = slice(h * 128, (h + 1) * 128)
        q_rope = q_r[:, sl] * cos + q_rr[:, sl] * sin
        q_ref[0, h, :, 0:128] = (q_nope[:, sl] * A_QSCALE).astype(BF16)
        q_ref[0, h, :, 128:256] = (q_rope * A_QSCALE).astype(BF16)
        k_ref[0, h, :, 0:128] = k_nope[:, sl].astype(BF16)
        k_ref[0, h, :, 128:256] = k_rope
        v_ref[0, h, :, 0:A_V] = vv[:, sl].astype(BF16)
        v_ref[0, h, :, A_V:2 * A_V] = ones_col


def _mla_prep(p0, cos, sin, q_gain, kv_gain, wqn, wqr, wqrr, wk, wv, batch, t_len):
    rows = p0.shape[0]
    per = t_len // HALF
    qk_shape = jax.ShapeDtypeStruct((batch, A_HEADS, t_len, A_HD), BF16)
    v_shape = jax.ShapeDtypeStruct((batch, A_HEADS, t_len, 2 * A_V), BF16)
    out_map = lambda i: (i // per, 0, i % per, 0)
    return pl.pallas_call(
        _mla_prep_kernel,
        out_shape=(qk_shape, qk_shape, v_shape),
        grid=(rows // HALF,),
        in_specs=[
            pl.BlockSpec((HALF, 256), lambda i: (i, P0_CQ // 256)),
            pl.BlockSpec((HALF, 256), lambda i: (i, P0_CKV // 256)),
            pl.BlockSpec((HALF, 128), lambda i: (i, P0_KRR // 128)),
            pl.BlockSpec((HALF, 128), lambda i: (i, 0)),
            pl.BlockSpec((HALF, 128), lambda i: (i, 0)),
            _const_spec((1, A_QLORA)),
            _const_spec((1, A_KVLORA)),
            _const_spec((A_QLORA, 512)),
            _const_spec((A_QLORA, 512)),
            _const_spec((A_QLORA, 512)),
            _const_spec((A_KVLORA, 512)),
            _const_spec((A_KVLORA, 512)),
        ],
        out_specs=(
            pl.BlockSpec((1, A_HEADS, HALF, A_HD), out_map),
            pl.BlockSpec((1, A_HEADS, HALF, A_HD), out_map),
            pl.BlockSpec((1, A_HEADS, HALF, 2 * A_V), out_map),
        ),
        compiler_params=_cparams(("parallel",)),
        name="mla_prep",
    )(p0, p0, p0, cos, sin, q_gain, kv_gain, wqn, wqr, wqrr, wk, wv)


def _attn_kernel(q_ref, qn_ref, k_ref, v_ref, o_ref, s_ref, c_ref, p_ref, m_ref, acc_ref, *, n_lat_chunks, tk):
    qi = pl.program_id(2)
    q = q_ref[0, 0]
    qn = qn_ref[0, 0]

    def scores(qq, start, width):
        return lax.dot_general(qq, k_ref[0, 0, pl.ds(start, width), :], (((1,), (1,)), ((), ())),
                               preferred_element_type=F32)

    def softmax_step(s_view, start, width, first):
        if not first:
            m_all = m_ref[...]
        m_parts, a_parts = [], []
        for r0 in range(0, A_TQ, A_RS):
            rows = slice(r0, r0 + A_RS)
            s = s_view[rows, :]
            s_max = jnp.max(s, axis=-1, keepdims=True)
            if first:
                m_new = s_max
            else:
                m_prev = m_all[rows]
                m_new = jnp.maximum(m_prev, s_max)
                a_parts.append(jnp.exp2(m_prev - m_new))
            m_parts.append(m_new)
            p_ref[rows, 0:width] = jnp.exp2(s - m_new).astype(BF16)
        m_ref[...] = jnp.concatenate(m_parts, axis=0)
        pv = jnp.dot(p_ref[:, 0:width], v_ref[0, 0, pl.ds(start, width), :], preferred_element_type=F32)
        acc_ref[...] = pv if first else jnp.concatenate(a_parts, axis=0) * acc_ref[...] + pv

    def kstart(c):
        return pl.multiple_of(CTX + c * tk, 256)

    def finish():
        acc = acc_ref[...]
        o_ref[0] = acc[:, 0:A_V] / acc[:, A_V:A_V + 1]

    @pl.when(qi == 0)
    def _():
        c_ref[...] = scores(q, 0, CTX)
        s_ref[1] = scores(qn, CTX, tk)
        softmax_step(c_ref, 0, CTX, True)
        finish()
        c_ref[...] = scores(qn, 0, CTX)

    @pl.when(qi > 0)
    def _():
        s_ref[0] = scores(q, kstart(1), tk)
        softmax_step(c_ref, 0, CTX, True)
        softmax_step(s_ref.at[1], kstart(0), tk, False)

        def body(c2, carry):
            c = 2 * c2 + 1
            s_ref[1] = scores(q, kstart(c + 1), tk)
            softmax_step(s_ref.at[0], kstart(c), tk, False)
            s_ref[0] = scores(q, kstart(c + 2), tk)
            softmax_step(s_ref.at[1], kstart(c + 1), tk, False)
            return carry

        lax.fori_loop(0, n_lat_chunks // 2 - 1, body, 0)
        s_ref[1] = scores(qn, CTX, tk)
        softmax_step(s_ref.at[0], kstart(n_lat_chunks - 1), tk, False)
        c_ref[...] = scores(qn, 0, CTX)
        finish()


def _attention(q, k, v):
    batch, heads, t_len, _ = q.shape
    tk = min(A_TK, (t_len - CTX) // 2)
    n_lat_chunks = (t_len - CTX) // tk
    n_q = t_len // A_TQ
    assert n_lat_chunks % 2 == 0 and n_lat_chunks * tk == t_len - CTX and tk % 256 == 0
    return pl.pallas_call(
        functools.partial(_attn_kernel, n_lat_chunks=n_lat_chunks, tk=tk),
        out_shape=jax.ShapeDtypeStruct((batch, t_len, heads * A_V), F32),
        grid=(batch, heads, n_q),
        in_specs=[
            pl.BlockSpec((1, 1, A_TQ, A_HD), lambda b, h, i: (b, h, i, 0)),
            pl.BlockSpec((1, 1, A_TQ, A_HD), lambda b, h, i: (b, h, jnp.minimum(i + 1, n_q - 1), 0)),
            pl.BlockSpec((1, 1, t_len, A_HD), lambda b, h, i: (b, h, 0, 0)),
            pl.BlockSpec((1, 1, t_len, 2 * A_V), lambda b, h, i: (b, h, 0, 0)),
        ],
        out_specs=pl.BlockSpec((1, A_TQ, A_V), lambda b, h, i: (b, i, h)),
        scratch_shapes=[
            pltpu.VMEM((2, A_TQ, tk), F32),
            pltpu.VMEM((A_TQ, CTX), F32),
            pltpu.VMEM((A_TQ, tk), BF16),
            pltpu.VMEM((A_TQ, 1), F32),
            pltpu.VMEM((A_TQ, 2 * A_V), F32),
        ],
        compiler_params=_cparams(("parallel", "parallel", "arbitrary")),
        name="mla_attention",
    )(q, q, k, v)


def _conv_kernel(x_ref, prev_ref, next_ref, w_ref, b_ref, post_ref, o_ref, *, t_len):
    i = pl.program_id(0)
    x = x_ref[...]
    row = lax.broadcasted_iota(jnp.int32, (TM, 1), 0)
    pos = (i * TM + row) % t_len
    x_prev = jnp.where(row == 0, prev_ref[7:8, :], pltpu.roll(x, 1, axis=0))
    x_next = jnp.where(row == TM - 1, next_ref[0:1, :], pltpu.roll(x, TM - 1, axis=0))
    seq_start = (pos == 0) | (pos == CTX)
    seq_end = (pos == CTX - 1) | (pos == t_len - 1)
    x_prev = jnp.where(seq_start, 0.0, x_prev)
    x_next = jnp.where(seq_end, 0.0, x_next)
    y = w_ref[0:1, :] * x_prev + w_ref[1:2, :] * x + w_ref[2:3, :] * x_next + b_ref[...]
    o_ref[...] = _silu(y) * post_ref[...]


def _mlstm_conv(p0, conv_w, conv_b, post, t_len):
    rows = p0.shape[0]
    n8 = rows // 8
    cb = P0_MQ // 1024
    return pl.pallas_call(
        functools.partial(_conv_kernel, t_len=t_len),
        out_shape=jax.ShapeDtypeStruct((rows, 1024), F32),
        grid=(rows // TM,),
        in_specs=[
            pl.BlockSpec((TM, 1024), lambda i: (i, cb)),
            pl.BlockSpec((8, 1024), lambda i: (jnp.maximum(i * (TM // 8) - 1, 0), cb)),
            pl.BlockSpec((8, 1024), lambda i: (jnp.minimum((i + 1) * (TM // 8), n8 - 1), cb)),
            _const_spec((3, 1024)),
            _const_spec((1, 1024)),
            _const_spec((1, 1024)),
        ],
        out_specs=pl.BlockSpec((TM, 1024), lambda i: (i, 0)),
        compiler_params=_cparams(("parallel",)),
        name="mlstm_conv",
    )(p0, p0, p0, conv_w, conv_b, post)


def _log_sigmoid(x):
    return jnp.minimum(x, 0.0) - jnp.log(1.0 + jnp.exp(-jnp.abs(x)))


def _mlstm_kernel(qkf_ref, qkb_ref, vf_ref, vb_ref, gf_ref, gb_ref, gbias_ref, hf_ref, hb_ref, c_ref, m_ref):
    j = pl.program_id(1)
    L = M_CHUNK

    @pl.when(j == 0)
    def _():
        c_ref[...] = jnp.zeros_like(c_ref)
        m_ref[...] = jnp.zeros_like(m_ref)

    r_i = lax.broadcasted_iota(jnp.int32, (L, L), 0)
    c_i = lax.broadcasted_iota(jnp.int32, (L, L), 1)
    row = lax.broadcasted_iota(jnp.int32, (L, 128), 0)
    lane = lax.broadcasted_iota(jnp.int32, (L, 128), 1)

    for step in range(M_STEP):
        for d, (qk_ref, v_ref, g_ref, h_ref) in enumerate(((qkf_ref, vf_ref, gf_ref, hf_ref),
                                                            (qkb_ref, vb_ref, gb_ref, hb_ref))):
            rev = d == 1
            part = M_STEP - 1 - step if rev else step
            rows = slice(part * L, (part + 1) * L)
            keep = (c_i >= r_i) if rev else (c_i <= r_i)
            tri = jnp.where(keep, LOG2E, 0.0).astype(F32)
            pre = g_ref[rows, :] + gbias_ref[...]
            csum = jnp.dot(tri, _log_sigmoid(pre), precision=HIGHEST, preferred_element_type=F32)
            g_run = pltpu.roll(csum, 124, axis=1)
            u = pre * LOG2E - g_run
            c_run = u
            k = 1
            while k < L:
                if rev:
                    shifted = jnp.where(row >= L - k, -jnp.inf, pltpu.roll(c_run, L - k, axis=0))
                else:
                    shifted = jnp.where(row < k, -jnp.inf, pltpu.roll(c_run, k, axis=0))
                c_run = jnp.maximum(c_run, shifted)
                k *= 2
            last = 0 if rev else L - 1
            m_prev = m_ref[d]
            m_run = jnp.maximum(c_run, m_prev)
            g_end = g_run[last:last + 1, :]
            m_new = g_end + jnp.maximum(m_prev, c_run[last:last + 1, :])
            w_inter_all = jnp.exp2(m_prev - m_run)
            floor_all = jnp.exp2(-(g_run + m_run))
            w_in_all = jnp.exp2(g_end + u - m_new)
            decay_all = jnp.exp2(g_end + m_prev - m_new)
            u_t = u.T
            m_ref[d] = m_new
            for h in range(M_HEADS):
                ci = 8 * d + h
                qh = qk_ref[rows, h * 128:(h + 1) * 128].astype(BF16)
                kh32 = qk_ref[rows, 512 + h * 128:512 + (h + 1) * 128]
                sc = lax.dot_general(qh, kh32.astype(BF16), (((1,), (1,)), ((), ())), preferred_element_type=F32)
                s = sc * jnp.exp2(jnp.where(keep, u_t[ci:ci + 1, :] - m_run[:, ci:ci + 1], -jnp.inf))
                ones_col = jnp.where(lane == ci, 1.0, 0.0).astype(BF16)
                v_ext = jnp.concatenate([v_ref[rows, h * 128:(h + 1) * 128].astype(BF16), ones_col], axis=1)
                c_st = c_ref[4 * d + h]
                r1 = jnp.dot(s.astype(BF16), v_ext, preferred_element_type=F32)
                r2 = jnp.dot(qh, c_st.astype(BF16), preferred_element_type=F32)
                den = r1[:, 128:256] + w_inter_all * r2[:, 128:256]
                inv = 1.0 / jnp.maximum(jnp.abs(den), floor_all)
                h_ref[rows, h * 128:(h + 1) * 128] = ((r1[:, 0:128] + w_inter_all[:, ci:ci + 1] * r2[:, 0:128])
                                                   * inv[:, ci:ci + 1])
                kw = (kh32 * w_in_all[:, ci:ci + 1]).astype(BF16)
                upd = lax.dot_general(kw, v_ext, (((0,), (0,)), ((), ())), preferred_element_type=F32)
                c_ref[4 * d + h] = decay_all[:, ci:ci + 1] * c_st + upd


def _mlstm_scan(qk, p0, gate_bias, batch, t_len):
    rows = qk.shape[0]
    blk = M_STEP * M_CHUNK
    nch = t_len // blk
    nctx = CTX // blk
    fwd = lambda b, j: b * nch + j
    bwd = lambda b, j: b * nch + jnp.where(j < nctx, nctx - 1 - j, nch + nctx - 1 - j)
    out = jax.ShapeDtypeStruct((rows, M_HEADS * M_V), F32)
    return pl.pallas_call(
        _mlstm_kernel,
        out_shape=(out, out),
        grid=(batch, nch),
        in_specs=[
            pl.BlockSpec((blk, 1024), lambda b, j: (fwd(b, j), 0)),
            pl.BlockSpec((blk, 1024), lambda b, j: (bwd(b, j), 0)),
            pl.BlockSpec((blk, 512), lambda b, j: (fwd(b, j), P0_MV // 512)),
            pl.BlockSpec((blk, 512), lambda b, j: (bwd(b, j), P0_MV // 512)),
            pl.BlockSpec((blk, 128), lambda b, j: (fwd(b, j), P0_GATE // 128)),
            pl.BlockSpec((blk, 128), lambda b, j: (bwd(b, j), P0_GATE // 128)),
            _const_spec((1, 128)),
        ],
        out_specs=(
            pl.BlockSpec((blk, 512), lambda b, j: (fwd(b, j), 0)),
            pl.BlockSpec((blk, 512), lambda b, j: (bwd(b, j), 0)),
        ),
        scratch_shapes=[
            pltpu.VMEM((2 * M_HEADS, M_QKP, 256), F32),
            pltpu.VMEM((2, 1, 128), F32),
        ],
        compiler_params=_cparams(("parallel", "arbitrary")),
        name="mlstm_scan",
    )(qk, qk, p0, p0, p0, p0, gate_bias)


def _head_rms(x, gain_row, n_heads, width):
    parts = []
    for h in range(n_heads):
        sl = slice(h * width, (h + 1) * width)
        parts.append(_rms(x[:, sl], gain_row[:, sl]))
    return jnp.concatenate(parts, axis=1)


def _ab_out_kernel(x_ref, a_ref, hf_ref, hb_ref, og_ref, hg_ref, wa_ref, wm_ref, gains_ref, tab_ref, o_ref, *, per):
    i = pl.program_id(0)
    m = _head_rms(hf_ref[...] + hb_ref[...], hg_ref[...], M_HEADS, M_V) * _sigmoid(og_ref[...])
    y = (jnp.dot(a_ref[...].astype(BF16), wa_ref[...], preferred_element_type=F32)
         + jnp.dot(m.astype(BF16), wm_ref[...], preferred_element_type=F32))
    _gated_residual_tile(x_ref, y, gains_ref[1:2, :], tab_ref, 2, i, per, TM, o_ref)


def _ab_out(xs, a, hf, hb, p0, head_gain, wa, wm, gains, tab, per):
    rows = xs.shape[0]
    row_spec = lambda w, cb=0: pl.BlockSpec((TM, w), lambda i: (i, cb))
    return pl.pallas_call(
        functools.partial(_ab_out_kernel, per=per),
        out_shape=jax.ShapeDtypeStruct((rows, D), F32),
        grid=(rows // TM,),
        in_specs=[
            row_spec(D), row_spec(512), row_spec(512), row_spec(512), row_spec(512, P0_OG // 512),
            _const_spec((1, 512)), _const_spec((512, D)), _const_spec((512, D)),
            _const_spec((4, D)), _const_spec((6, 8, D)),
        ],
        out_specs=row_spec(D),
        compiler_params=_cparams(("parallel",)),
        name="ab_out",
    )(xs, a, hf, hb, p0, head_gain, wa, wm, gains, tab)


def _swiglu(h, wgu_ref, wdn_ref, lead):
    acc = None
    for c0 in range(0, D_FF, FF_CHUNK):
        g = jnp.dot(h, wgu_ref[lead + (slice(None), slice(c0, c0 + FF_CHUNK))], preferred_element_type=F32)
        u = jnp.dot(h, wgu_ref[lead + (slice(None), slice(D_FF + c0, D_FF + c0 + FF_CHUNK))],
                    preferred_element_type=F32)
        a = (_silu(g) * u).astype(BF16)
        part = jnp.dot(a, wdn_ref[lead + (slice(c0, c0 + FF_CHUNK), slice(None))], preferred_element_type=F32)
        acc = part if acc is None else acc + part
    return acc


def _ffn_kernel(x_ref, gains_ref, tab_ref, wgu_ref, wdn_ref, o_ref, *, per):
    i = pl.program_id(0)
    h = _norm_mod_tile(x_ref, gains_ref[2:3, :], tab_ref, 3, 4, i, per, TM).astype(BF16)
    y = _swiglu(h, wgu_ref, wdn_ref, ())
    _gated_residual_tile(x_ref, y, gains_ref[3:4, :], tab_ref, 5, i, per, TM, o_ref)


def _ffn(xs, gains, tab, wgu, wdn, per):
    rows = xs.shape[0]
    return pl.pallas_call(
        functools.partial(_ffn_kernel, per=per),
        out_shape=jax.ShapeDtypeStruct((rows, D), F32),
        grid=(rows // TM,),
        in_specs=[
            pl.BlockSpec((TM, D), lambda i: (i, 0)),
            _const_spec((4, D)), _const_spec((6, 8, D)),
            _const_spec((D, 2 * D_FF)), _const_spec((D_FF, D)),
        ],
        out_specs=pl.BlockSpec((TM, D), lambda i: (i, 0)),
        compiler_params=_cparams(("parallel",)),
        name="ffn",
    )(xs, gains, tab, wgu, wdn)


def _hgrn_stream(q, v, g_ref, w_ref, sl, st, gsum, neg, lmask, rev):
    c = G_SUB
    ns = G_CHUNK // c
    blk = lambda ref, i: ref[c * i:c * (i + 1), sl]
    bcast = lambda ref, r: jnp.broadcast_to(ref[r:r + 1, sl], (c, 128))
    qb = lambda i: q[c * i:c * (i + 1)]
    bound = [bcast(g_ref, c * j if rev else c * j + c - 1) for j in range(ns)]
    khat = jnp.concatenate([jnp.exp2(bound[j] - blk(w_ref, j)) for j in range(ns)], axis=0)
    pairs = [(i, j) for i in range(ns) for j in range(ns) if (i < j if rev else i > j)]
    qst = jnp.concatenate([qb(i) * jnp.exp2(blk(g_ref, i) - bound[j]) for (i, j) in pairs], axis=0)
    cross = lax.dot_general(qst.astype(BF16), khat.astype(BF16), (((1,), (1,)), ((), ())),
                            preferred_element_type=F32)
    z_rows = []
    for i in range(ns):
        gi, qi = blk(g_ref, i), qb(i)
        z_rows.append(jnp.concatenate(
            [qi * jnp.exp2(gi + neg[s] - bcast(w_ref, c * i + s)) for s in range(c)], axis=1))
    diag = jnp.dot(jnp.concatenate(z_rows, axis=0).astype(BF16), gsum, preferred_element_type=F32)
    a_rows = []
    for i in range(ns):
        acc = diag[c * i:c * (i + 1)] * lmask[i]
        for p, (ii, j) in enumerate(pairs):
            if ii == i:
                acc = acc + cross[c * p:c * (p + 1)] * lmask[j]
        a_rows.append(acc)
    a = jnp.concatenate(a_rows, axis=0).astype(BF16)
    g_end = g_ref[0:1, sl] if rev else g_ref[G_CHUNK - 1:G_CHUNK, sl]
    o = (jnp.dot(a, v.astype(BF16), preferred_element_type=F32)
         + lax.dot_general((q * jnp.exp2(g_ref[:, sl])).astype(BF16), st.astype(BF16), (((1,), (1,)), ((), ())),
                           preferred_element_type=F32))
    kw = jnp.exp2(g_end - w_ref[:, sl]).astype(BF16)
    st_new = st * jnp.exp2(g_end) + lax.dot_general(v.astype(BF16), kw, (((0,), (0,)), ((), ())),
                                                     preferred_element_type=F32)
    return o, st_new


def _hgrn_kernel(qf_ref, zf_ref, vf_ref, qb_ref, zb_ref, vb_ref, lb_ref, of_ref, ob_ref, st_ref, g_ref, w_ref):
    j = pl.program_id(1)
    L = G_CHUNK

    @pl.when(j == 0)
    def _():
        st_ref[...] = jnp.zeros_like(st_ref)

    r_i = lax.broadcasted_iota(jnp.int32, (L, L), 0)
    c_i = lax.broadcasted_iota(jnp.int32, (L, L), 1)
    lb = lb_ref[...]
    log_1mlb = jnp.log(1.0 - lb)
    gs_r = lax.broadcasted_iota(jnp.int32, (G_SUB * 128, L), 0)
    gs_c = lax.broadcasted_iota(jnp.int32, (G_SUB * 128, L), 1)
    gsum = jnp.where(gs_r // 128 == gs_c % G_SUB, 1.0, 0.0).astype(BF16)
    sub = lax.broadcasted_iota(jnp.int32, (G_SUB, 128), 0)
    lane = lax.broadcasted_iota(jnp.int32, (G_SUB, L), 1)
    lmask = [jnp.where(lane // G_SUB == jj, 1.0, 0.0) for jj in range(L // G_SUB)]
    for step in range(G_STEP):
        for d, (q_ref, z_ref, v_ref, o_ref) in enumerate(((qf_ref, zf_ref, vf_ref, of_ref),
                                                           (qb_ref, zb_ref, vb_ref, ob_ref))):
            rev = d == 1
            part = G_STEP - 1 - step if rev else step
            rows = slice(part * L, (part + 1) * L)
            neg = [jnp.where((sub <= s) if rev else (sub >= s), 0.0, -jnp.inf) for s in range(G_SUB)]
            tri = jnp.where((c_i >= r_i) if rev else (c_i <= r_i), LOG2E, 0.0).astype(F32)
            z = z_ref[rows, :]
            e = jnp.exp(-jnp.abs(z))
            r = 1.0 / (1.0 + e)
            sig_pos = jnp.where(z >= 0, r, e * r)
            log_f = jnp.log(lb + (1.0 - lb) * sig_pos)
            log_k = log_1mlb - jnp.maximum(z, 0.0) - jnp.log(1.0 + e)
            g_all = jnp.dot(tri, log_f, precision=HIGHEST, preferred_element_type=F32)
            k_idx = step * 2 + d
            g_ref[k_idx] = g_all
            w_ref[k_idx] = g_all - LOG2E * log_k
            q_all = _silu(q_ref[rows, :])
            for h in range(G_HEADS):
                sl = slice(h * 128, (h + 1) * 128)
                o, st_new = _hgrn_stream(q_all[:, sl], v_ref[rows, sl], g_ref.at[k_idx], w_ref.at[k_idx], sl,
                                         st_ref[d * G_HEADS + h], gsum, neg, lmask, rev)
                o_ref[rows, sl] = o
                st_ref[d * G_HEADS + h] = st_new


def _hgrn_scan(p1, lb_row, batch, t_len):
    rows = p1.shape[0]
    blk = G_STEP * G_CHUNK
    nst = t_len // blk
    nctx = CTX // blk
    fwd = lambda b, j: b * nst + j
    bwd = lambda b, j: b * nst + jnp.where(j < nctx, nctx - 1 - j, nst + nctx - 1 - j)
    spec = lambda order, cb: pl.BlockSpec((blk, 1024), lambda b, j: (order(b, j), cb))
    out = jax.ShapeDtypeStruct((rows, G_HEADS * G_DV), F32)
    return pl.pallas_call(
        _hgrn_kernel,
        out_shape=(out, out),
        grid=(batch, nst),
        in_specs=[spec(fwd, 0), spec(fwd, 1), spec(fwd, 3), spec(bwd, 0), spec(bwd, 2), spec(bwd, 3),
                  _const_spec((1, 1024))],
        out_specs=(spec(fwd, 0), spec(bwd, 0)),
        scratch_shapes=[pltpu.VMEM((2 * G_HEADS, G_DV, G_DK), F32),
                        pltpu.VMEM((2 * G_STEP, G_CHUNK, 1024), F32),
                        pltpu.VMEM((2 * G_STEP, G_CHUNK, 1024), F32)],
        compiler_params=_cparams(("parallel", "arbitrary")),
        name="hgrn_scan",
    )(p1, p1, p1, p1, p1, p1, lb_row)


def _c_out_kernel(x_ref, of_ref, ob_ref, hg_ref, gain_ref, w_ref, gains_ref, tab_ref, o_ref, *, per):
    i = pl.program_id(0)
    o = _head_rms(of_ref[...] + ob_ref[...], gain_ref[...], G_HEADS, G_DV) * _silu(hg_ref[...])
    y = jnp.dot(o.astype(BF16), w_ref[...], preferred_element_type=F32)
    _gated_residual_tile(x_ref, y, gains_ref[1:2, :], tab_ref, 2, i, per, TM, o_ref)


def _c_out(xs, of, ob, p1, head_gain, w, gains, tab, per):
    rows = xs.shape[0]
    row_spec = lambda cb=0: pl.BlockSpec((TM, D), lambda i: (i, cb))
    return pl.pallas_call(
        functools.partial(_c_out_kernel, per=per),
        out_shape=jax.ShapeDtypeStruct((rows, D), F32),
        grid=(rows // TM,),
        in_specs=[row_spec(), row_spec(), row_spec(), row_spec(4), _const_spec((1, D)), _const_spec((D, D)),
                  _const_spec((4, D)), _const_spec((6, 8, D))],
        out_specs=row_spec(),
        compiler_params=_cparams(("parallel",)),
        name="c_out",
    )(xs, of, ob, p1, head_gain, w, gains, tab)


def _router_kernel(x_ref, gains_ref, tab_ref, wr_ref, h_ref, r_ref, *, per):
    i = pl.program_id(0)
    h = _norm_mod_tile(x_ref, gains_ref[2:3, :], tab_ref, 3, 4, i, per, TM)
    for j in range(D // 128):
        h_ref[pl.ds(j, TM, stride=D // 128), :] = h[:, j * 128:(j + 1) * 128]
    logits = lax.dot_general(wr_ref[...], h, (((1,), (1,)), ((), ())), precision=HIGHEST,
                             preferred_element_type=F32)
    eid = lax.broadcasted_iota(jnp.int32, logits.shape, 0).astype(F32)
    m1 = jnp.max(logits, axis=0, keepdims=True)
    i1 = jnp.min(jnp.where(logits == m1, eid, float(N_EXP)), axis=0, keepdims=True)
    rest = jnp.where(eid == i1, -jnp.inf, logits)
    m2 = jnp.max(rest, axis=0, keepdims=True)
    i2 = jnp.min(jnp.where(rest == m2, eid, float(N_EXP)), axis=0, keepdims=True)
    e2 = jnp.exp(m2 - m1)
    g1 = 1.0 / (1.0 + e2)
    g2 = e2 * g1
    rid = lax.broadcasted_iota(jnp.int32, (128, TM), 0)
    packed = jnp.where(rid == 0, i1, jnp.where(rid == 1, i2, jnp.where(rid == 2, g1, jnp.where(rid == 3, g2, 0.0))))
    r_ref[...] = packed.T


def _router(xs, gains, tab, wr_t, per):
    rows = xs.shape[0]
    return pl.pallas_call(
        functools.partial(_router_kernel, per=per),
        out_shape=(jax.ShapeDtypeStruct((rows * (D // 128), 128), F32), jax.ShapeDtypeStruct((rows, 128), F32)),
        grid=(rows // TM,),
        in_specs=[pl.BlockSpec((TM, D), lambda i: (i, 0)), _const_spec((4, D)), _const_spec((6, 8, D)),
                  _const_spec((N_EXP, D))],
        out_specs=(pl.BlockSpec((TM * (D // 128), 128), lambda i: (i, 0)), pl.BlockSpec((TM, 128), lambda i: (i, 0))),
        compiler_params=_cparams(("parallel",)),
        name="moe_router",
    )(xs, gains, tab, wr_t)


def _tile_rows(ref, index):
    return jnp.concatenate([ref[index(j)] for j in range(D // 128)], axis=1)


def _expert_kernel(be_ref, nact_ref, tok_ref, dstx_ref, h_hbm, wgu_ref, wdn_ref, yg_hbm,
                   xbuf, ybuf, gsem, ssem):
    b = pl.program_id(0)
    nb = pl.num_programs(0)
    slot = b % 2
    other = 1 - slot
    nxt = jnp.minimum(b + 1, nb - 1)

    def gather_rows(blk, dst_slot, r0, r1):
        for r in range(r0, r1):
            src = h_hbm.at[pl.ds(pl.multiple_of(tok_ref[blk * MOE_BLOCK + r], 8), 8)]
            pltpu.make_async_copy(src, xbuf.at[dst_slot, pl.ds(8 * r, 8)], gsem.at[dst_slot]).start(priority=1)

    def scatter_rows(blk, src_slot, r0, r1):
        for r in range(r0, r1):
            dst = yg_hbm.at[pl.ds(pl.multiple_of(dstx_ref[blk * MOE_BLOCK + r], 8), 8)]
            pltpu.make_async_copy(ybuf.at[src_slot, pl.ds(8 * r, 8)], dst, ssem.at[src_slot]).start()

    def wait_gather(s):
        pltpu.make_async_copy(h_hbm.at[pl.ds(0, MOE_BLOCK * (D // 128))], xbuf.at[s], gsem.at[s]).wait()

    def wait_scatter(s):
        pltpu.make_async_copy(ybuf.at[s], yg_hbm.at[pl.ds(0, MOE_BLOCK * (D // 128))], ssem.at[s]).wait()

    @pl.when(b == 0)
    def _():
        ybuf[...] = jnp.zeros_like(ybuf)
        gather_rows(0, 0, 0, MOE_BLOCK)

    wait_gather(slot)

    @pl.when(b >= 1)
    def _():
        wait_scatter(slot)

    n_chunks = D_FF // FF_CHUNK
    per_g = -(-MOE_BLOCK // (n_chunks // 2))
    per_s = -(-MOE_BLOCK // (n_chunks - n_chunks // 2))

    @pl.when(b < nact_ref[0])
    def _():
        x = _tile_rows(xbuf, lambda j: (slot, pl.ds(j, MOE_BLOCK, stride=D // 128), slice(None))).astype(BF16)
        acc = None
        for ci in range(n_chunks):
            c0 = ci * FF_CHUNK
            g = jnp.dot(x, wgu_ref[0, :, c0:c0 + FF_CHUNK], preferred_element_type=F32)
            u = jnp.dot(x, wgu_ref[0, :, D_FF + c0:D_FF + c0 + FF_CHUNK], preferred_element_type=F32)
            a = (_silu(g) * u).astype(BF16)
            part = jnp.dot(a, wdn_ref[0, c0:c0 + FF_CHUNK, :], preferred_element_type=F32)
            acc = part if acc is None else acc + part
            if ci < n_chunks // 2:
                gather_rows(nxt, other, ci * per_g, min((ci + 1) * per_g, MOE_BLOCK))
            else:
                cs = ci - n_chunks // 2
                scatter_rows(b, other, cs * per_s, min((cs + 1) * per_s, MOE_BLOCK))
        for j in range(D // 128):
            ybuf[slot, pl.ds(j, MOE_BLOCK, stride=D // 128), :] = acc[:, j * 128:(j + 1) * 128]

    @pl.when(b >= nact_ref[0])
    def _():
        gather_rows(nxt, other, 0, MOE_BLOCK)
        scatter_rows(b, other, 0, MOE_BLOCK)

    @pl.when(b == nb - 1)
    def _():
        wait_gather(other)
        wait_scatter(other)
        scatter_rows(nb, slot, 0, MOE_BLOCK)
        wait_scatter(slot)


def _experts(block_expert, n_active, tok3, dst3, h3, wgu, wdn, n_rows_out):
    n_blocks = block_expert.shape[0]
    grid_spec = pltpu.PrefetchScalarGridSpec(
        num_scalar_prefetch=4,
        grid=(n_blocks,),
        in_specs=[
            pl.BlockSpec(memory_space=pl.ANY),
            pl.BlockSpec((1, D, 2 * D_FF), lambda b, be, na, tk, ds: (be[b], 0, 0)),
            pl.BlockSpec((1, D_FF, D), lambda b, be, na, tk, ds: (be[b], 0, 0)),
        ],
        out_specs=pl.BlockSpec(memory_space=pl.ANY),
        scratch_shapes=[pltpu.VMEM((2, MOE_BLOCK * (D // 128), 128), F32), pltpu.VMEM((2, MOE_BLOCK * (D // 128), 128), F32),
                        pltpu.SemaphoreType.DMA((2,)), pltpu.SemaphoreType.DMA((2,))],
    )
    return pl.pallas_call(
        _expert_kernel,
        out_shape=jax.ShapeDtypeStruct((n_rows_out * (D // 128), 128), F32),
        grid_spec=grid_spec,
        compiler_params=_cparams(("arbitrary",)),
        name="moe_experts",
    )(block_expert, n_active, tok3.reshape(-1), dst3.reshape(-1), h3, wgu, wdn)


def _combine_kernel(yg_ref, r_ref, x_ref, gains_ref, tab_ref, o_ref, *, per):
    tile = pl.program_id(0) * per + 1 + pl.program_id(1)
    gates = r_ref[...]
    rows = lambda k: _tile_rows(yg_ref, lambda j: (pl.ds(k * (D // 128) + j, HALF, stride=2 * (D // 128)), slice(None)))
    y = gates[:, 2:3] * rows(0) + gates[:, 3:4] * rows(1)
    idx = _mod_index(tile, per)
    gate = tab_ref[5, pl.ds(idx, 1), :]
    o_ref[...] = x_ref[...] + gate * _rms(y, gains_ref[3:4, :])


def _combine_latents(yg, rout, xs, gains, tab, batch, s_len, per):
    nl = s_len // HALF
    tile = lambda b, j: b * per + 1 + j
    return pl.pallas_call(
        functools.partial(_combine_kernel, per=per),
        out_shape=jax.ShapeDtypeStruct((batch * s_len, D), F32),
        grid=(batch, nl),
        in_specs=[
            pl.BlockSpec((HALF * 2 * (D // 128), 128), lambda b, j: (tile(b, j), 0)),
            pl.BlockSpec((HALF, 128), lambda b, j: (tile(b, j), 0)),
            pl.BlockSpec((HALF, D), lambda b, j: (tile(b, j), 0)),
            _const_spec((4, D)), _const_spec((6, 8, D)),
        ],
        out_specs=pl.BlockSpec((HALF, D), lambda b, j: (b * nl + j, 0)),
        compiler_params=_cparams(("parallel", "parallel")),
        name="moe_combine",
    )(yg, rout, xs, gains, tab)


def _moe_plan(rout):
    n_tok = rout.shape[0]
    n_flat = 2 * n_tok
    n_blocks = -(-n_flat // MOE_BLOCK) + N_EXP
    assert n_blocks * MOE_BLOCK == n_flat + N_EXP * MOE_BLOCK and n_flat < 2 ** 19
    flat_e = rout[:, 0:2].astype(jnp.int32).reshape(-1)
    experts = jnp.arange(N_EXP, dtype=jnp.int32)
    counts = jnp.sum((flat_e[:, None] == experts[None, :]).astype(jnp.int32), axis=0)
    padded = (counts + MOE_BLOCK - 1) // MOE_BLOCK * MOE_BLOCK
    pend = jnp.cumsum(padded)
    blk = jnp.arange(n_blocks, dtype=jnp.int32)
    block_expert = jnp.minimum(jnp.searchsorted(pend, blk * MOE_BLOCK, side='right'), N_EXP - 1).astype(jnp.int32)
    n_active = (pend[-1] // MOE_BLOCK).astype(jnp.int32).reshape(1)
    flat = jnp.arange(n_flat, dtype=jnp.int32)
    pad_i = jnp.arange(MOE_BLOCK, dtype=jnp.int32)[None, :]
    pad_key = jnp.where(pad_i < (padded - counts)[:, None], experts[:, None] * 2 ** 20 + 2 ** 19 + pad_i,
                        jnp.iinfo(jnp.int32).max)
    keys = jnp.sort(jnp.concatenate([flat_e * 2 ** 20 + flat, pad_key.reshape(-1)]))
    rows = (keys & (2 ** 20 - 1)).reshape(n_blocks, MOE_BLOCK)
    valid = rows < 2 ** 19
    tok3 = (jnp.where(valid, rows // 2, 0) * (D // 128)).reshape(n_blocks, 1, MOE_BLOCK)
    dummy = n_flat + pad_i
    dst = jnp.where(valid, rows, dummy)
    dst3 = (jnp.concatenate([dummy, dst], axis=0) * (D // 128)).reshape(n_blocks + 1, 1, MOE_BLOCK)
    return block_expert, n_active, tok3, dst3, n_flat + MOE_BLOCK


def _rot_half_cols(w):
    half = A_ROPE // 2
    g = w.reshape(w.shape[0], -1, A_ROPE)
    return jnp.concatenate([-g[..., half:], g[..., :half]], axis=-1).reshape(w.shape)


def _pad_groups(w, width, to):
    k = w.shape[0]
    g = w.reshape(k, -1, width)
    return jnp.pad(g, ((0, 0), (0, 0), (0, to - width))).reshape(k, -1)


def _layer0_inproj_weight(w_in):
    splits = np.cumsum([A_QLORA, A_KVLORA, A_ROPE, M_HEADS * M_QK, M_HEADS * M_QK, M_HEADS * M_V, M_HEADS * M_V])
    cq, ckv, kr, mq, mk, mv, og, gt = jnp.split(w_in, [int(s) for s in splits], axis=1)
    pad64 = lambda a: jnp.pad(a, ((0, 0), (0, 64)))
    cols = [cq, ckv, pad64(kr), mv, _pad_groups(mq, M_QK, M_QKP), _pad_groups(mk, M_QK, M_QKP), og,
            jnp.pad(gt, ((0, 0), (0, 128 - gt.shape[1]))), pad64(_rot_half_cols(kr))]
    w = jnp.concatenate(cols, axis=1)
    assert w.shape[1] == P0_N
    return w.astype(BF16)


def _rope_tables(batch, s_len):
    t = jnp.arange(s_len)
    row = (t // GRID_W).astype(F32)
    col = (t % GRID_W).astype(F32)
    n_freq = A_ROPE // 4
    inv_freq = ROPE_THETA ** (-jnp.arange(n_freq, dtype=F32) / n_freq)
    ang = jnp.concatenate([row[:, None] * inv_freq, col[:, None] * inv_freq], axis=-1)
    ang = jnp.concatenate([ang, ang], axis=-1)

    def table(vals, fill):
        lat = jnp.pad(vals, ((0, 0), (0, 128 - A_ROPE)), constant_values=fill)
        one = jnp.concatenate([jnp.full((CTX, 128), fill, F32), lat], axis=0)
        return jnp.tile(one, (batch, 1))

    return table(jnp.cos(ang), 1.0), table(jnp.sin(ang), 0.0)


def kernel(x, c, ctx, c_ctx, w_mod, b_mod, norm_gains, w_in_ab, mla_q_gain, mla_w_uq, mla_kv_gain, mla_w_ukv,
           ml_conv_w, ml_conv_b, ml_gate_b, ml_head_gain, w_out_ab, ffn_w_gu, ffn_w_dn, w_in_c, hg_lb_logits,
           hg_head_gain, w_out_c, moe_router, moe_w_gu, moe_w_dn):
    batch, s_len, d = x.shape
    assert d == D and ctx.shape[1] == CTX and batch == 2
    assert s_len % 512 == 0 and (batch * (CTX + s_len)) % TM == 0
    t_len = CTX + s_len
    per = t_len // HALF

    xs = jnp.concatenate([ctx, x], axis=1).reshape(batch * t_len, D)
    cc = jnp.concatenate([c, c_ctx[None, :], jnp.zeros((8 - batch - 1, D), F32)], axis=0)
    tabs = _mod_table(cc, w_mod, b_mod)

    gains = norm_gains[0]
    tab = tabs[0]
    p0 = _inproj(xs, gains, tab, _layer0_inproj_weight(w_in_ab[0]), per, P0_N // 2)

    cos, sin = _rope_tables(batch, s_len)
    w_uq = mla_w_uq[0].reshape(A_QLORA, A_HEADS, A_NOPE + A_ROPE)
    wq_nope = w_uq[:, :, :A_NOPE].reshape(A_QLORA, A_HEADS * A_NOPE)
    wq_rope = w_uq[:, :, A_NOPE:].reshape(A_QLORA, A_HEADS * A_ROPE)
    w_ukv = mla_w_ukv[0].reshape(A_KVLORA, A_HEADS, A_NOPE + A_V)
    q, k, v = _mla_prep(
        p0, cos, sin, mla_q_gain[0][None, :], mla_kv_gain[0][None, :],
        wq_nope.astype(BF16),
        _pad_groups(wq_rope, A_ROPE, 128).astype(BF16),
        _pad_groups(_rot_half_cols(wq_rope), A_ROPE, 128).astype(BF16),
        w_ukv[:, :, :A_NOPE].reshape(A_KVLORA, -1).astype(BF16),
        w_ukv[:, :, A_NOPE:].reshape(A_KVLORA, -1).astype(BF16),
        batch, t_len)
    attn = _attention(q, k, v).reshape(batch * t_len, A_HEADS * A_V)

    conv_w = jnp.concatenate([_pad_groups(ml_conv_w[0][:, :M_HEADS * M_QK], M_QK, M_QKP),
                              _pad_groups(ml_conv_w[0][:, M_HEADS * M_QK:], M_QK, M_QKP)], axis=1)
    conv_b = jnp.concatenate([_pad_groups(ml_conv_b[0][None, :M_HEADS * M_QK], M_QK, M_QKP),
                              _pad_groups(ml_conv_b[0][None, M_HEADS * M_QK:], M_QK, M_QKP)], axis=1)
    post = jnp.concatenate([jnp.ones((1, 512), F32), jnp.full((1, 512), M_QK ** -0.5, F32)], axis=1)
    qk = _mlstm_conv(p0, conv_w, conv_b, post, t_len)
    gate_bias = jnp.pad(ml_gate_b[0][None, :], ((0, 0), (0, 128 - 4 * M_HEADS)))
    hf, hb = _mlstm_scan(qk, p0, gate_bias, batch, t_len)

    w_out = w_out_ab[0].astype(BF16)
    xs = _ab_out(xs, attn, hf, hb, p0, ml_head_gain[0][None, :], w_out[:A_HEADS * A_V], w_out[A_HEADS * A_V:],
                 gains, tab, per)
    xs = _ffn(xs, gains, tab, ffn_w_gu[0].astype(BF16), ffn_w_dn[0].astype(BF16), per)

    gains = norm_gains[1]
    tab = tabs[1]
    lbs = jnp.cumsum(jax.nn.softmax(hg_lb_logits.astype(F32), axis=0), axis=0)
    lb_row = (lbs - lbs[0])[1][None, :]
    p1 = _inproj(xs, gains, tab, w_in_c[0].astype(BF16), per, 1024)
    of, ob = _hgrn_scan(p1, lb_row, batch, t_len)
    xs = _c_out(xs, of, ob, p1, hg_head_gain[0][None, :], w_out_c[0].astype(BF16), gains, tab, per)

    h, rout = _router(xs, gains, tab, moe_router[0].T, per)
    block_expert, n_active, tok3, dst3, n_rows_out = _moe_plan(rout)
    yg = _experts(block_expert, n_active, tok3, dst3, h, moe_w_gu[0].astype(BF16), moe_w_dn[0].astype(BF16), n_rows_out)
    out = _combine_latents(yg, rout, xs, gains, tab, batch, s_len, per)
    return out.reshape(batch, s_len, D)
```

```python
import functools

import jax
import jax.numpy as jnp
import numpy as np
from jax import lax
from jax.experimental import pallas as pl
from jax.experimental.pallas import tpu as pltpu

F32 = jnp.float32
BF16 = jnp.bfloat16
HIGHEST = lax.Precision.HIGHEST
LOG2E = 1.4426950408889634

D = 1024
CTX = 256
EPS = 1e-6
HALF = 256
TM = 512
VMEM_LIMIT = 56 * 1024 * 1024

A_HEADS = 4
A_QLORA = 256
A_KVLORA = 128
A_NOPE = 128
A_ROPE = 64
A_V = 128
A_HD = 256
A_SCALE = (A_NOPE + A_ROPE) ** -0.5
ROPE_THETA = 10000.0
GRID_W = 64
A_QSCALE = A_SCALE * LOG2E
A_TQ = 256
A_TK = 4096
A_RS = 32

M_HEADS = 4
M_QK = 64
M_V = 128
M_CHUNK = 128
M_STEP = 2
M_QKP = 128

G_HEADS = 8
G_DK = 128
G_DV = 128
G_CHUNK = 64
G_STEP = 4
G_SUB = 8

D_FF = 2816
N_EXP = 8
MOE_BLOCK = 512
FF_CHUNK = 256

P0_CQ = 0
P0_CKV = 256
P0_KR = 384
P0_MV = 512
P0_MQ = 1024
P0_MK = 1536
P0_OG = 2048
P0_GATE = 2560
P0_KRR = 2688
P0_N = 2816


def _cparams(sem):
    return pltpu.CompilerParams(dimension_semantics=sem, vmem_limit_bytes=VMEM_LIMIT)


def _const_spec(shape):
    nd = len(shape)
    return pl.BlockSpec(shape, lambda *_: (0,) * nd, pipeline_mode=pl.Buffered(1))


def _rms(x, gain_row):
    ms = jnp.mean(x * x, axis=-1, keepdims=True)
    return x * lax.rsqrt(ms + EPS) * gain_row


def _mod_index(half_idx, per):
    return jnp.where(half_idx % per == 0, 2, half_idx // per)


def _silu(x):
    return x * _sigmoid(x)


def _sigmoid(x):
    return 1.0 / (1.0 + jnp.exp(-x))


def _norm_mod_rows(x, gain_row, tab_ref, k_shift, k_scale, idx):
    shift = tab_ref[k_shift, pl.ds(idx, 1), :]
    scale = tab_ref[k_scale, pl.ds(idx, 1), :]
    return _rms(x, gain_row) * (1.0 + scale) + shift


def _norm_mod_tile(x_ref, gain_row, tab_ref, k_shift, k_scale, tile, per, rows):
    parts = []
    for s in range(rows // HALF):
        idx = _mod_index(tile * (rows // HALF) + s, per)
        parts.append(_norm_mod_rows(x_ref[s * HALF:(s + 1) * HALF, :], gain_row, tab_ref, k_shift, k_scale, idx))
    return parts[0] if len(parts) == 1 else jnp.concatenate(parts, axis=0)


def _gated_residual_tile(x_ref, y, gain_row, tab_ref, k_gate, tile, per, rows, o_ref):
    yn = _rms(y, gain_row)
    for s in range(rows // HALF):
        idx = _mod_index(tile * (rows // HALF) + s, per)
        gate = tab_ref[k_gate, pl.ds(idx, 1), :]
        sl = slice(s * HALF, (s + 1) * HALF)
        o_ref[sl, :] = x_ref[sl, :] + gate * yn[sl, :]


def _mod_kernel(c_ref, w_ref, b_ref, o_ref):
    a = _silu(c_ref[...])
    o_ref[0, 0] = jnp.dot(a, w_ref[0], precision=HIGHEST, preferred_element_type=F32) + b_ref[0, 0]


def _mod_table(cc, w_mod, b_mod):
    depth = w_mod.shape[0]
    return pl.pallas_call(
        _mod_kernel,
        out_shape=jax.ShapeDtypeStruct((depth, 6, 8, D), F32),
        grid=(depth, 6),
        in_specs=[
            pl.BlockSpec((8, D), lambda l, k: (0, 0)),
            pl.BlockSpec((1, D, D), lambda l, k: (l, 0, k)),
            pl.BlockSpec((1, 1, 1, D), lambda l, k: (l, k, 0, 0)),
        ],
        out_specs=pl.BlockSpec((1, 1, 8, D), lambda l, k: (l, k, 0, 0)),
        compiler_params=_cparams(("arbitrary", "arbitrary")),
        name="mod_table",
    )(cc, w_mod, b_mod.reshape(depth, 6, 1, D))


def _inproj_kernel(x_ref, gains_ref, tab_ref, w_ref, o_ref, *, per, nchunk):
    i = pl.program_id(0)
    h = _norm_mod_tile(x_ref, gains_ref[0:1, :], tab_ref, 0, 1, i, per, TM).astype(BF16)
    n = w_ref.shape[1]
    for n0 in range(0, n, nchunk):
        o_ref[:, n0:n0 + nchunk] = jnp.dot(h, w_ref[:, n0:n0 + nchunk], preferred_element_type=F32)


def _inproj(xs, gains, tab, w, per, nchunk):
    rows = xs.shape[0]
    n = w.shape[1]
    return pl.pallas_call(
        functools.partial(_inproj_kernel, per=per, nchunk=nchunk),
        out_shape=jax.ShapeDtypeStruct((rows, n), F32),
        grid=(rows // TM,),
        in_specs=[
            pl.BlockSpec((TM, D), lambda i: (i, 0)),
            _const_spec((4, D)),
            _const_spec((6, 8, D)),
            _const_spec((D, n)),
        ],
        out_specs=pl.BlockSpec((TM, n), lambda i: (i, 0)),
        compiler_params=_cparams(("parallel",)),
        name="inproj",
    )(xs, gains, tab, w)


def _mla_prep_kernel(cq_ref, ckr_ref, krr_ref, cos_ref, sin_ref, qg_ref, kvg_ref, wqn_ref, wqr_ref, wqrr_ref,
                     wk_ref, wv_ref, q_ref, k_ref, v_ref):
    qn = _rms(cq_ref[...], qg_ref[...]).astype(BF16)
    ckr = ckr_ref[...]
    kvn = _rms(ckr[:, :A_KVLORA], kvg_ref[...]).astype(BF16)
    cos = cos_ref[...]
    sin = sin_ref[...]
    q_nope = jnp.dot(qn, wqn_ref[...], preferred_element_type=F32)
    q_r = jnp.dot(qn, wqr_ref[...], preferred_element_type=F32)
    q_rr = jnp.dot(qn, wqrr_ref[...], preferred_element_type=F32)
    k_nope = jnp.dot(kvn, wk_ref[...], preferred_element_type=F32)
    vv = jnp.dot(kvn, wv_ref[...], preferred_element_type=F32)
    k_rope = (ckr[:, A_KVLORA:] * cos + krr_ref[...] * sin).astype(BF16)
    lane = lax.broadcasted_iota(jnp.int32, (HALF, A_V), 1)
    ones_col = jnp.where(lane == 0, 1.0, 0.0).astype(BF16)
    for h in range(A_HEADS):
        sl = slice(h * 128, (h + 1) * 128)
        q_rope = q_r[:, sl] * cos + q_rr[:, sl] * sin
        q_ref[0, h, :, 0:128] = (q_nope[:, sl] * A_QSCALE).astype(BF16)
        q_ref[0, h, :, 128:256] = (q_rope * A_QSCALE).astype(BF16)
        k_ref[0, h, :, 0:128] = k_nope[:, sl].astype(BF16)
        k_ref[0, h, :, 128:256] = k_rope
        v_ref[0, h, :, 0:A_V] = vv[:, sl].astype(BF16)
        v_ref[0, h, :, A_V:2 * A_V] = ones_col


def _mla_prep(p0, cos, sin, q_gain, kv_gain, wqn, wqr, wqrr, wk, wv, batch, t_len):
    rows = p0.shape[0]
    per = t_len // HALF
    qk_shape = jax.ShapeDtypeStruct((batch, A_HEADS, t_len, A_HD), BF16)
    v_shape = jax.ShapeDtypeStruct((batch, A_HEADS, t_len, 2 * A_V), BF16)
    out_map = lambda i: (i // per, 0, i % per, 0)
    return pl.pallas_call(
        _mla_prep_kernel,
        out_shape=(qk_shape, qk_shape, v_shape),
        grid=(rows // HALF,),
        in_specs=[
            pl.BlockSpec((HALF, 256), lambda i: (i, P0_CQ // 256)),
            pl.BlockSpec((HALF, 256), lambda i: (i, P0_CKV // 256)),
            pl.BlockSpec((HALF, 128), lambda i: (i, P0_KRR // 128)),
            pl.BlockSpec((HALF, 128), lambda i: (i, 0)),
            pl.BlockSpec((HALF, 128), lambda i: (i, 0)),
            _const_spec((1, A_QLORA)),
            _const_spec((1, A_KVLORA)),
            _const_spec((A_QLORA, 512)),
            _const_spec((A_QLORA, 512)),
            _const_spec((A_QLORA, 512)),
            _const_spec((A_KVLORA, 512)),
            _const_spec((A_KVLORA, 512)),
        ],
        out_specs=(
            pl.BlockSpec((1, A_HEADS, HALF, A_HD), out_map),
            pl.BlockSpec((1, A_HEADS, HALF, A_HD), out_map),
            pl.BlockSpec((1, A_HEADS, HALF, 2 * A_V), out_map),
        ),
        compiler_params=_cparams(("parallel",)),
        name="mla_prep",
    )(p0, p0, p0, cos, sin, q_gain, kv_gain, wqn, wqr, wqrr, wk, wv)


def _attn_kernel(q_ref, qn_ref, k_ref, v_ref, o_ref, s_ref, c_ref, p_ref, m_ref, acc_ref, *, n_lat_chunks, tk):
    qi = pl.program_id(2)
    q = q_ref[0, 0]
    qn = qn_ref[0, 0]

    def scores(qq, start, width):
        return lax.dot_general(qq, k_ref[0, 0, pl.ds(start, width), :], (((1,), (1,)), ((), ())),
                               preferred_element_type=F32)

    def softmax_step(s_view, start, width, first):
        if not first:
            m_all = m_ref[...]
        m_parts, a_parts = [], []
        for r0 in range(0, A_TQ, A_RS):
            rows = slice(r0, r0 + A_RS)
            s = s_view[rows, :]
            s_max = jnp.max(s, axis=-1, keepdims=True)
            if first:
                m_new = s_max
            else:
                m_prev = m_all[rows]
                m_new = jnp.maximum(m_prev, s_max)
                a_parts.append(jnp.exp2(m_prev - m_new))
            m_parts.append(m_new)
            p_ref[rows, 0:width] = jnp.exp2(s - m_new).astype(BF16)
        m_ref[...] = jnp.concatenate(m_parts, axis=0)
        pv = jnp.dot(p_ref[:, 0:width], v_ref[0, 0, pl.ds(start, width), :], preferred_element_type=F32)
        acc_ref[...] = pv if first else jnp.concatenate(a_parts, axis=0) * acc_ref[...] + pv

    def kstart(c):
        return pl.multiple_of(CTX + c * tk, 256)

    def finish():
        acc = acc_ref[...]
        o_ref[0] = (acc[:, 0:A_V] / acc[:, A_V:A_V + 1]).astype(BF16)

    @pl.when(qi == 0)
    def _():
        c_ref[...] = scores(q, 0, CTX)
        s_ref[1] = scores(qn, CTX, tk)
        softmax_step(c_ref, 0, CTX, True)
        finish()
        c_ref[...] = scores(qn, 0, CTX)

    @pl.when(qi > 0)
    def _():
        s_ref[0] = scores(q, kstart(1), tk)
        softmax_step(c_ref, 0, CTX, True)
        softmax_step(s_ref.at[1], kstart(0), tk, False)

        def body(c2, carry):
            c = 2 * c2 + 1
            s_ref[1] = scores(q, kstart(c + 1), tk)
            softmax_step(s_ref.at[0], kstart(c), tk, False)
            s_ref[0] = scores(q, kstart(c + 2), tk)
            softmax_step(s_ref.at[1], kstart(c + 1), tk, False)
            return carry

        lax.fori_loop(0, n_lat_chunks // 2 - 1, body, 0)
        s_ref[1] = scores(qn, CTX, tk)
        softmax_step(s_ref.at[0], kstart(n_lat_chunks - 1), tk, False)
        c_ref[...] = scores(qn, 0, CTX)
        finish()


def _attention(q, k, v):
    batch, heads, t_len, _ = q.shape
    tk = min(A_TK, (t_len - CTX) // 2)
    n_lat_chunks = (t_len - CTX) // tk
    n_q = t_len // A_TQ
    assert n_lat_chunks % 2 == 0 and n_lat_chunks * tk == t_len - CTX and tk % 256 == 0
    return pl.pallas_call(
        functools.partial(_attn_kernel, n_lat_chunks=n_lat_chunks, tk=tk),
        out_shape=jax.ShapeDtypeStruct((batch, t_len, heads * A_V), BF16),
        grid=(batch, heads, n_q),
        in_specs=[
            pl.BlockSpec((1, 1, A_TQ, A_HD), lambda b, h, i: (b, h, i, 0)),
            pl.BlockSpec((1, 1, A_TQ, A_HD), lambda b, h, i: (b, h, jnp.minimum(i + 1, n_q - 1), 0)),
            pl.BlockSpec((1, 1, t_len, A_HD), lambda b, h, i: (b, h, 0, 0)),
            pl.BlockSpec((1, 1, t_len, 2 * A_V), lambda b, h, i: (b, h, 0, 0)),
        ],
        out_specs=pl.BlockSpec((1, A_TQ, A_V), lambda b, h, i: (b, i, h)),
        scratch_shapes=[
            pltpu.VMEM((2, A_TQ, tk), F32),
            pltpu.VMEM((A_TQ, CTX), F32),
            pltpu.VMEM((A_TQ, tk), BF16),
            pltpu.VMEM((A_TQ, 1), F32),
            pltpu.VMEM((A_TQ, 2 * A_V), F32),
        ],
        compiler_params=_cparams(("parallel", "parallel", "arbitrary")),
        name="mla_attention",
    )(q, q, k, v)


def _conv_kernel(x_ref, prev_ref, next_ref, w_ref, b_ref, post_ref, o_ref, *, t_len):
    i = pl.program_id(0)
    x = x_ref[...]
    row = lax.broadcasted_iota(jnp.int32, (TM, 1), 0)
    pos = (i * TM + row) % t_len
    x_prev = jnp.where(row == 0, prev_ref[7:8, :], pltpu.roll(x, 1, axis=0))
    x_next = jnp.where(row == TM - 1, next_ref[0:1, :], pltpu.roll(x, TM - 1, axis=0))
    seq_start = (pos == 0) | (pos == CTX)
    seq_end = (pos == CTX - 1) | (pos == t_len - 1)
    x_prev = jnp.where(seq_start, 0.0, x_prev)
    x_next = jnp.where(seq_end, 0.0, x_next)
    y = w_ref[0:1, :] * x_prev + w_ref[1:2, :] * x + w_ref[2:3, :] * x_next + b_ref[...]
    o_ref[...] = _silu(y) * post_ref[...]


def _mlstm_conv(p0, conv_w, conv_b, post, t_len):
    rows = p0.shape[0]
    n8 = rows // 8
    cb = P0_MQ // 1024
    return pl.pallas_call(
        functools.partial(_conv_kernel, t_len=t_len),
        out_shape=jax.ShapeDtypeStruct((rows, 1024), F32),
        grid=(rows // TM,),
        in_specs=[
            pl.BlockSpec((TM, 1024), lambda i: (i, cb)),
            pl.BlockSpec((8, 1024), lambda i: (jnp.maximum(i * (TM // 8) - 1, 0), cb)),
            pl.BlockSpec((8, 1024), lambda i: (jnp.minimum((i + 1) * (TM // 8), n8 - 1), cb)),
            _const_spec((3, 1024)),
            _const_spec((1, 1024)),
            _const_spec((1, 1024)),
        ],
        out_specs=pl.BlockSpec((TM, 1024), lambda i: (i, 0)),
        compiler_params=_cparams(("parallel",)),
        name="mlstm_conv",
    )(p0, p0, p0, conv_w, conv_b, post)


def _log_sigmoid(x):
    return jnp.minimum(x, 0.0) - jnp.log(1.0 + jnp.exp(-jnp.abs(x)))


def _mlstm_kernel(qkf_ref, qkb_ref, vf_ref, vb_ref, gf_ref, gb_ref, gbias_ref, hf_ref, hb_ref, c_ref, m_ref):
    j = pl.program_id(1)
    L = M_CHUNK

    @pl.when(j == 0)
    def _():
        c_ref[...] = jnp.zeros_like(c_ref)
        m_ref[...] = jnp.zeros_like(m_ref)

    r_i = lax.broadcasted_iota(jnp.int32, (L, L), 0)
    c_i = lax.broadcasted_iota(jnp.int32, (L, L), 1)
    row = lax.broadcasted_iota(jnp.int32, (L, 128), 0)
    lane = lax.broadcasted_iota(jnp.int32, (L, 128), 1)

    for step in range(M_STEP):
        for d, (qk_ref, v_ref, g_ref, h_ref) in enumerate(((qkf_ref, vf_ref, gf_ref, hf_ref),
                                                            (qkb_ref, vb_ref, gb_ref, hb_ref))):
            rev = d == 1
            part = M_STEP - 1 - step if rev else step
            rows = slice(part * L, (part + 1) * L)
            keep = (c_i >= r_i) if rev else (c_i <= r_i)
            tri = jnp.where(keep, LOG2E, 0.0).astype(F32)
            pre = g_ref[rows, :] + gbias_ref[...]
            csum = jnp.dot(tri, _log_sigmoid(pre), precision=HIGHEST, preferred_element_type=F32)
            g_run = pltpu.roll(csum, 124, axis=1)
            u = pre * LOG2E - g_run
            c_run = u
            k = 1
            while k < L:
                if rev:
                    shifted = jnp.where(row >= L - k, -jnp.inf, pltpu.roll(c_run, L - k, axis=0))
                else:
                    shifted = jnp.where(row < k, -jnp.inf, pltpu.roll(c_run, k, axis=0))
                c_run = jnp.maximum(c_run, shifted)
                k *= 2
            last = 0 if rev else L - 1
            m_prev = m_ref[d]
            m_run = jnp.maximum(c_run, m_prev)
            g_end = g_run[last:last + 1, :]
            m_new = g_end + jnp.maximum(m_prev, c_run[last:last + 1, :])
            w_inter_all = jnp.exp2(m_prev - m_run)
            floor_all = jnp.exp2(-(g_run + m_run))
            w_in_all = jnp.exp2(g_end + u - m_new)
            decay_all = jnp.exp2(g_end + m_prev - m_new)
            u_t = u.T
            m_ref[d] = m_new
            for h in range(M_HEADS):
                ci = 8 * d + h
                qh = qk_ref[rows, h * 128:(h + 1) * 128].astype(BF16)
                kh32 = qk_ref[rows, 512 + h * 128:512 + (h + 1) * 128]
                sc = lax.dot_general(qh, kh32.astype(BF16), (((1,), (1,)), ((), ())), preferred_element_type=F32)
                s = sc * jnp.exp2(jnp.where(keep, u_t[ci:ci + 1, :] - m_run[:, ci:ci + 1], -jnp.inf))
                ones_col = jnp.where(lane == ci, 1.0, 0.0).astype(BF16)
                v_ext = jnp.concatenate([v_ref[rows, h * 128:(h + 1) * 128].astype(BF16), ones_col], axis=1)
                c_st = c_ref[4 * d + h]
                r1 = jnp.dot(s.astype(BF16), v_ext, preferred_element_type=F32)
                r2 = jnp.dot(qh, c_st.astype(BF16), preferred_element_type=F32)
                den = r1[:, 128:256] + w_inter_all * r2[:, 128:256]
                inv = 1.0 / jnp.maximum(jnp.abs(den), floor_all)
                h_ref[rows, h * 128:(h + 1) * 128] = ((r1[:, 0:128] + w_inter_all[:, ci:ci + 1] * r2[:, 0:128])
                                                   * inv[:, ci:ci + 1]).astype(BF16)
                kw = (kh32 * w_in_all[:, ci:ci + 1]).astype(BF16)
                upd = lax.dot_general(kw, v_ext, (((0,), (0,)), ((), ())), preferred_element_type=F32)
                c_ref[4 * d + h] = decay_all[:, ci:ci + 1] * c_st + upd


def _mlstm_scan(qk, p0, gate_bias, batch, t_len):
    rows = qk.shape[0]
    blk = M_STEP * M_CHUNK
    nch = t_len // blk
    nctx = CTX // blk
    fwd = lambda b, j: b * nch + j
    bwd = lambda b, j: b * nch + jnp.where(j < nctx, nctx - 1 - j, nch + nctx - 1 - j)
    out = jax.ShapeDtypeStruct((rows, M_HEADS * M_V), BF16)
    return pl.pallas_call(
        _mlstm_kernel,
        out_shape=(out, out),
        grid=(batch, nch),
        in_specs=[
            pl.BlockSpec((blk, 1024), lambda b, j: (fwd(b, j), 0)),
            pl.BlockSpec((blk, 1024), lambda b, j: (bwd(b, j), 0)),
            pl.BlockSpec((blk, 512), lambda b, j: (fwd(b, j), P0_MV // 512)),
            pl.BlockSpec((blk, 512), lambda b, j: (bwd(b, j), P0_MV // 512)),
            pl.BlockSpec((blk, 128), lambda b, j: (fwd(b, j), P0_GATE // 128)),
            pl.BlockSpec((blk, 128), lambda b, j: (bwd(b, j), P0_GATE // 128)),
            _const_spec((1, 128)),
        ],
        out_specs=(
            pl.BlockSpec((blk, 512), lambda b, j: (fwd(b, j), 0)),
            pl.BlockSpec((blk, 512), lambda b, j: (bwd(b, j), 0)),
        ),
        scratch_shapes=[
            pltpu.VMEM((2 * M_HEADS, M_QKP, 256), F32),
            pltpu.VMEM((2, 1, 128), F32),
        ],
        compiler_params=_cparams(("parallel", "arbitrary")),
        name="mlstm_scan",
    )(qk, qk, p0, p0, p0, p0, gate_bias)


def _head_rms(x, gain_row, n_heads, width):
    parts = []
    for h in range(n_heads):
        sl = slice(h * width, (h + 1) * width)
        parts.append(_rms(x[:, sl], gain_row[:, sl]))
    return jnp.concatenate(parts, axis=1)


def _ab_out_kernel(x_ref, a_ref, hf_ref, hb_ref, og_ref, hg_ref, wa_ref, wm_ref, gains_ref, tab_ref, o_ref, *, per):
    i = pl.program_id(0)
    m = _head_rms(hf_ref[...].astype(F32) + hb_ref[...].astype(F32), hg_ref[...], M_HEADS, M_V) * _sigmoid(og_ref[...])
    y = (jnp.dot(a_ref[...], wa_ref[...], preferred_element_type=F32)
         + jnp.dot(m.astype(BF16), wm_ref[...], preferred_element_type=F32))
    _gated_residual_tile(x_ref, y, gains_ref[1:2, :], tab_ref, 2, i, per, TM, o_ref)


def _ab_out(xs, a, hf, hb, p0, head_gain, wa, wm, gains, tab, per):
    rows = xs.shape[0]
    row_spec = lambda w, cb=0: pl.BlockSpec((TM, w), lambda i: (i, cb))
    return pl.pallas_call(
        functools.partial(_ab_out_kernel, per=per),
        out_shape=jax.ShapeDtypeStruct((rows, D), F32),
        grid=(rows // TM,),
        in_specs=[
            row_spec(D), row_spec(512), row_spec(512), row_spec(512), row_spec(512, P0_OG // 512),
            _const_spec((1, 512)), _const_spec((512, D)), _const_spec((512, D)),
            _const_spec((4, D)), _const_spec((6, 8, D)),
        ],
        out_specs=row_spec(D),
        compiler_params=_cparams(("parallel",)),
        name="ab_out",
    )(xs, a, hf, hb, p0, head_gain, wa, wm, gains, tab)


def _swiglu(h, wgu_ref, wdn_ref, lead):
    acc = None
    for c0 in range(0, D_FF, FF_CHUNK):
        g = jnp.dot(h, wgu_ref[lead + (slice(None), slice(c0, c0 + FF_CHUNK))], preferred_element_type=F32)
        u = jnp.dot(h, wgu_ref[lead + (slice(None), slice(D_FF + c0, D_FF + c0 + FF_CHUNK))],
                    preferred_element_type=F32)
        a = (_silu(g) * u).astype(BF16)
        part = jnp.dot(a, wdn_ref[lead + (slice(c0, c0 + FF_CHUNK), slice(None))], preferred_element_type=F32)
        acc = part if acc is None else acc + part
    return acc


def _ffn_kernel(x_ref, gains_ref, tab_ref, wgu_ref, wdn_ref, o_ref, *, per):
    i = pl.program_id(0)
    h = _norm_mod_tile(x_ref, gains_ref[2:3, :], tab_ref, 3, 4, i, per, TM).astype(BF16)
    y = _swiglu(h, wgu_ref, wdn_ref, ())
    _gated_residual_tile(x_ref, y, gains_ref[3:4, :], tab_ref, 5, i, per, TM, o_ref)


def _ffn(xs, gains, tab, wgu, wdn, per):
    rows = xs.shape[0]
    return pl.pallas_call(
        functools.partial(_ffn_kernel, per=per),
        out_shape=jax.ShapeDtypeStruct((rows, D), F32),
        grid=(rows // TM,),
        in_specs=[
            pl.BlockSpec((TM, D), lambda i: (i, 0)),
            _const_spec((4, D)), _const_spec((6, 8, D)),
            _const_spec((D, 2 * D_FF)), _const_spec((D_FF, D)),
        ],
        out_specs=pl.BlockSpec((TM, D), lambda i: (i, 0)),
        compiler_params=_cparams(("parallel",)),
        name="ffn",
    )(xs, gains, tab, wgu, wdn)


def _hgrn_stream(q, v, g_ref, w_ref, sl, st, gsum, neg, lmask, rev):
    c = G_SUB
    ns = G_CHUNK // c
    blk = lambda ref, i: ref[c * i:c * (i + 1), sl]
    bcast = lambda ref, r: jnp.broadcast_to(ref[r:r + 1, sl], (c, 128))
    qb = lambda i: q[c * i:c * (i + 1)]
    bound = [bcast(g_ref, c * j if rev else c * j + c - 1) for j in range(ns)]
    khat = jnp.concatenate([jnp.exp2(bound[j] - blk(w_ref, j)) for j in range(ns)], axis=0)
    pairs = [(i, j) for i in range(ns) for j in range(ns) if (i < j if rev else i > j)]
    qst = jnp.concatenate([qb(i) * jnp.exp2(blk(g_ref, i) - bound[j]) for (i, j) in pairs], axis=0)
    cross = lax.dot_general(qst.astype(BF16), khat.astype(BF16), (((1,), (1,)), ((), ())),
                            preferred_element_type=F32)
    z_rows = []
    for i in range(ns):
        gi, qi = blk(g_ref, i), qb(i)
        z_rows.append(jnp.concatenate(
            [qi * jnp.exp2(gi + neg[s] - bcast(w_ref, c * i + s)) for s in range(c)], axis=1))
    diag = jnp.dot(jnp.concatenate(z_rows, axis=0).astype(BF16), gsum, preferred_element_type=F32)
    a_rows = []
    for i in range(ns):
        acc = diag[c * i:c * (i + 1)] * lmask[i]
        for p, (ii, j) in enumerate(pairs):
            if ii == i:
                acc = acc + cross[c * p:c * (p + 1)] * lmask[j]
        a_rows.append(acc)
    a = jnp.concatenate(a_rows, axis=0).astype(BF16)
    g_end = g_ref[0:1, sl] if rev else g_ref[G_CHUNK - 1:G_CHUNK, sl]
    o = (jnp.dot(a, v.astype(BF16), preferred_element_type=F32)
         + lax.dot_general((q * jnp.exp2(g_ref[:, sl])).astype(BF16), st.astype(BF16), (((1,), (1,)), ((), ())),
                           preferred_element_type=F32))
    kw = jnp.exp2(g_end - w_ref[:, sl]).astype(BF16)
    st_new = st * jnp.exp2(g_end) + lax.dot_general(v.astype(BF16), kw, (((0,), (0,)), ((), ())),
                                                     preferred_element_type=F32)
    return o, st_new


def _hgrn_kernel(qf_ref, zf_ref, vf_ref, qb_ref, zb_ref, vb_ref, lb_ref, of_ref, ob_ref, st_ref, g_ref, w_ref):
    j = pl.program_id(1)
    L = G_CHUNK

    @pl.when(j == 0)
    def _():
        st_ref[...] = jnp.zeros_like(st_ref)

    r_i = lax.broadcasted_iota(jnp.int32, (L, L), 0)
    c_i = lax.broadcasted_iota(jnp.int32, (L, L), 1)
    lb = lb_ref[...]
    log_1mlb = jnp.log(1.0 - lb)
    gs_r = lax.broadcasted_iota(jnp.int32, (G_SUB * 128, L), 0)
    gs_c = lax.broadcasted_iota(jnp.int32, (G_SUB * 128, L), 1)
    gsum = jnp.where(gs_r // 128 == gs_c % G_SUB, 1.0, 0.0).astype(BF16)
    sub = lax.broadcasted_iota(jnp.int32, (G_SUB, 128), 0)
    lane = lax.broadcasted_iota(jnp.int32, (G_SUB, L), 1)
    lmask = [jnp.where(lane // G_SUB == jj, 1.0, 0.0) for jj in range(L // G_SUB)]
    for step in range(G_STEP):
        for d, (q_ref, z_ref, v_ref, o_ref) in enumerate(((qf_ref, zf_ref, vf_ref, of_ref),
                                                           (qb_ref, zb_ref, vb_ref, ob_ref))):
            rev = d == 1
            part = G_STEP - 1 - step if rev else step
            rows = slice(part * L, (part + 1) * L)
            neg = [jnp.where((sub <= s) if rev else (sub >= s), 0.0, -jnp.inf) for s in range(G_SUB)]
            tri = jnp.where((c_i >= r_i) if rev else (c_i <= r_i), LOG2E, 0.0).astype(F32)
            z = z_ref[rows, :]
            e = jnp.exp(-jnp.abs(z))
            r = 1.0 / (1.0 + e)
            sig_pos = jnp.where(z >= 0, r, e * r)
            log_f = jnp.log(lb + (1.0 - lb) * sig_pos)
            log_k = log_1mlb - jnp.maximum(z, 0.0) - jnp.log(1.0 + e)
            g_all = jnp.dot(tri, log_f, precision=HIGHEST, preferred_element_type=F32)
            k_idx = step * 2 + d
            g_ref[k_idx] = g_all
            w_ref[k_idx] = g_all - LOG2E * log_k
            q_all = _silu(q_ref[rows, :])
            for h in range(G_HEADS):
                sl = slice(h * 128, (h + 1) * 128)
                o, st_new = _hgrn_stream(q_all[:, sl], v_ref[rows, sl], g_ref.at[k_idx], w_ref.at[k_idx], sl,
                                         st_ref[d * G_HEADS + h], gsum, neg, lmask, rev)
                o_ref[rows, sl] = o.astype(BF16)
                st_ref[d * G_HEADS + h] = st_new


def _hgrn_scan(p1, lb_row, batch, t_len):
    rows = p1.shape[0]
    blk = G_STEP * G_CHUNK
    nst = t_len // blk
    nctx = CTX // blk
    fwd = lambda b, j: b * nst + j
    bwd = lambda b, j: b * nst + jnp.where(j < nctx, nctx - 1 - j, nst + nctx - 1 - j)
    spec = lambda order, cb: pl.BlockSpec((blk, 1024), lambda b, j: (order(b, j), cb))
    out = jax.ShapeDtypeStruct((rows, G_HEADS * G_DV), BF16)
    return pl.pallas_call(
        _hgrn_kernel,
        out_shape=(out, out),
        grid=(batch, nst),
        in_specs=[spec(fwd, 0), spec(fwd, 1), spec(fwd, 3), spec(bwd, 0), spec(bwd, 2), spec(bwd, 3),
                  _const_spec((1, 1024))],
        out_specs=(spec(fwd, 0), spec(bwd, 0)),
        scratch_shapes=[pltpu.VMEM((2 * G_HEADS, G_DV, G_DK), F32),
                        pltpu.VMEM((2 * G_STEP, G_CHUNK, 1024), F32),
                        pltpu.VMEM((2 * G_STEP, G_CHUNK, 1024), F32)],
        compiler_params=_cparams(("parallel", "arbitrary")),
        name="hgrn_scan",
    )(p1, p1, p1, p1, p1, p1, lb_row)


def _c_out_kernel(x_ref, of_ref, ob_ref, hg_ref, gain_ref, w_ref, gains_ref, tab_ref, o_ref, *, per):
    i = pl.program_id(0)
    o = _head_rms(of_ref[...].astype(F32) + ob_ref[...].astype(F32), gain_ref[...], G_HEADS, G_DV) * _silu(hg_ref[...])
    y = jnp.dot(o.astype(BF16), w_ref[...], preferred_element_type=F32)
    _gated_residual_tile(x_ref, y, gains_ref[1:2, :], tab_ref, 2, i, per, TM, o_ref)


def _c_out(xs, of, ob, p1, head_gain, w, gains, tab, per):
    rows = xs.shape[0]
    row_spec = lambda cb=0: pl.BlockSpec((TM, D), lambda i: (i, cb))
    return pl.pallas_call(
        functools.partial(_c_out_kernel, per=per),
        out_shape=jax.ShapeDtypeStruct((rows, D), F32),
        grid=(rows // TM,),
        in_specs=[row_spec(), row_spec(), row_spec(), row_spec(4), _const_spec((1, D)), _const_spec((D, D)),
                  _const_spec((4, D)), _const_spec((6, 8, D))],
        out_specs=row_spec(),
        compiler_params=_cparams(("parallel",)),
        name="c_out",
    )(xs, of, ob, p1, head_gain, w, gains, tab)


def _router_kernel(x_ref, gains_ref, tab_ref, wr_ref, h_ref, r_ref, *, per):
    i = pl.program_id(0)
    h = _norm_mod_tile(x_ref, gains_ref[2:3, :], tab_ref, 3, 4, i, per, TM)
    for j in range(D // 128):
        h_ref[pl.ds(j, TM, stride=D // 128), :] = h[:, j * 128:(j + 1) * 128]
    logits = lax.dot_general(wr_ref[...], h, (((1,), (1,)), ((), ())), precision=HIGHEST,
                             preferred_element_type=F32)
    eid = lax.broadcasted_iota(jnp.int32, logits.shape, 0).astype(F32)
    m1 = jnp.max(logits, axis=0, keepdims=True)
    i1 = jnp.min(jnp.where(logits == m1, eid, float(N_EXP)), axis=0, keepdims=True)
    rest = jnp.where(eid == i1, -jnp.inf, logits)
    m2 = jnp.max(rest, axis=0, keepdims=True)
    i2 = jnp.min(jnp.where(rest == m2, eid, float(N_EXP)), axis=0, keepdims=True)
    e2 = jnp.exp(m2 - m1)
    g1 = 1.0 / (1.0 + e2)
    g2 = e2 * g1
    rid = lax.broadcasted_iota(jnp.int32, (128, TM), 0)
    packed = jnp.where(rid == 0, i1, jnp.where(rid == 1, i2, jnp.where(rid == 2, g1, jnp.where(rid == 3, g2, 0.0))))
    r_ref[...] = packed.T


def _router(xs, gains, tab, wr_t, per):
    rows = xs.shape[0]
    return pl.pallas_call(
        functools.partial(_router_kernel, per=per),
        out_shape=(jax.ShapeDtypeStruct((rows * (D // 128), 128), F32), jax.ShapeDtypeStruct((rows, 128), F32)),
        grid=(rows // TM,),
        in_specs=[pl.BlockSpec((TM, D), lambda i: (i, 0)), _const_spec((4, D)), _const_spec((6, 8, D)),
                  _const_spec((N_EXP, D))],
        out_specs=(pl.BlockSpec((TM * (D // 128), 128), lambda i: (i, 0)), pl.BlockSpec((TM, 128), lambda i: (i, 0))),
        compiler_params=_cparams(("parallel",)),
        name="moe_router",
    )(xs, gains, tab, wr_t)


def _tile_rows(ref, index):
    return jnp.concatenate([ref[index(j)] for j in range(D // 128)], axis=1)


def _expert_kernel(be_ref, nact_ref, tok_ref, dstx_ref, h_hbm, wgu_ref, wdn_ref, yg_hbm,
                   xbuf, ybuf, gsem, ssem):
    b = pl.program_id(0)
    nb = pl.num_programs(0)
    slot = b % 2
    other = 1 - slot
    nxt = jnp.minimum(b + 1, nb - 1)

    def gather_rows(blk, dst_slot, r0, r1):
        for r in range(r0, r1):
            src = h_hbm.at[pl.ds(pl.multiple_of(tok_ref[blk * MOE_BLOCK + r], 8), 8)]
            pltpu.make_async_copy(src, xbuf.at[dst_slot, pl.ds(8 * r, 8)], gsem.at[dst_slot]).start(priority=1)

    def scatter_rows(blk, src_slot, r0, r1):
        for r in range(r0, r1):
            dst = yg_hbm.at[pl.ds(pl.multiple_of(dstx_ref[blk * MOE_BLOCK + r], 8), 8)]
            pltpu.make_async_copy(ybuf.at[src_slot, pl.ds(8 * r, 8)], dst, ssem.at[src_slot]).start()

    def wait_gather(s):
        pltpu.make_async_copy(h_hbm.at[pl.ds(0, MOE_BLOCK * (D // 128))], xbuf.at[s], gsem.at[s]).wait()

    def wait_scatter(s):
        pltpu.make_async_copy(ybuf.at[s], yg_hbm.at[pl.ds(0, MOE_BLOCK * (D // 128))], ssem.at[s]).wait()

    @pl.when(b == 0)
    def _():
        ybuf[...] = jnp.zeros_like(ybuf)
        gather_rows(0, 0, 0, MOE_BLOCK)

    wait_gather(slot)

    @pl.when(b >= 1)
    def _():
        wait_scatter(slot)

    n_chunks = D_FF // FF_CHUNK
    per_g = -(-MOE_BLOCK // (n_chunks // 2))
    per_s = -(-MOE_BLOCK // (n_chunks - n_chunks // 2))

    @pl.when(b < nact_ref[0])
    def _():
        x = _tile_rows(xbuf, lambda j: (slot, pl.ds(j, MOE_BLOCK, stride=D // 128), slice(None))).astype(BF16)
        acc = None
        for ci in range(n_chunks):
            c0 = ci * FF_CHUNK
            g = jnp.dot(x, wgu_ref[0, :, c0:c0 + FF_CHUNK], preferred_element_type=F32)
            u = jnp.dot(x, wgu_ref[0, :, D_FF + c0:D_FF + c0 + FF_CHUNK], preferred_element_type=F32)
            a = (_silu(g) * u).astype(BF16)
            part = jnp.dot(a, wdn_ref[0, c0:c0 + FF_CHUNK, :], preferred_element_type=F32)
            acc = part if acc is None else acc + part
            if ci < n_chunks // 2:
                gather_rows(nxt, other, ci * per_g, min((ci + 1) * per_g, MOE_BLOCK))
            else:
                cs = ci - n_chunks // 2
                scatter_rows(b, other, cs * per_s, min((cs + 1) * per_s, MOE_BLOCK))
        for j in range(D // 128):
            ybuf[slot, pl.ds(j, MOE_BLOCK, stride=D // 128), :] = acc[:, j * 128:(j + 1) * 128]

    @pl.when(b >= nact_ref[0])
    def _():
        gather_rows(nxt, other, 0, MOE_BLOCK)
        scatter_rows(b, other, 0, MOE_BLOCK)

    @pl.when(b == nb - 1)
    def _():
        wait_gather(other)
        wait_scatter(other)
        scatter_rows(nb, slot, 0, MOE_BLOCK)
        wait_scatter(slot)


def _experts(block_expert, n_active, tok3, dst3, h3, wgu, wdn, n_rows_out):
    n_blocks = block_expert.shape[0]
    grid_spec = pltpu.PrefetchScalarGridSpec(
        num_scalar_prefetch=4,
        grid=(n_blocks,),
        in_specs=[
            pl.BlockSpec(memory_space=pl.ANY),
            pl.BlockSpec((1, D, 2 * D_FF), lambda b, be, na, tk, ds: (be[b], 0, 0)),
            pl.BlockSpec((1, D_FF, D), lambda b, be, na, tk, ds: (be[b], 0, 0)),
        ],
        out_specs=pl.BlockSpec(memory_space=pl.ANY),
        scratch_shapes=[pltpu.VMEM((2, MOE_BLOCK * (D // 128), 128), F32), pltpu.VMEM((2, MOE_BLOCK * (D // 128), 128), F32),
                        pltpu.SemaphoreType.DMA((2,)), pltpu.SemaphoreType.DMA((2,))],
    )
    return pl.pallas_call(
        _expert_kernel,
        out_shape=jax.ShapeDtypeStruct((n_rows_out * (D // 128), 128), F32),
        grid_spec=grid_spec,
        compiler_params=_cparams(("arbitrary",)),
        name="moe_experts",
    )(block_expert, n_active, tok3.reshape(-1), dst3.reshape(-1), h3, wgu, wdn)


def _combine_kernel(yg_ref, r_ref, x_ref, gains_ref, tab_ref, o_ref, *, per):
    tile = pl.program_id(0) * per + 1 + pl.program_id(1)
    gates = r_ref[...]
    rows = lambda k: _tile_rows(yg_ref, lambda j: (pl.ds(k * (D // 128) + j, HALF, stride=2 * (D // 128)), slice(None)))
    y = gates[:, 2:3] * rows(0) + gates[:, 3:4] * rows(1)
    idx = _mod_index(tile, per)
    gate = tab_ref[5, pl.ds(idx, 1), :]
    o_ref[...] = x_ref[...] + gate * _rms(y, gains_ref[3:4, :])


def _combine_latents(yg, rout, xs, gains, tab, batch, s_len, per):
    nl = s_len // HALF
    tile = lambda b, j: b * per + 1 + j
    return pl.pallas_call(
        functools.partial(_combine_kernel, per=per),
        out_shape=jax.ShapeDtypeStruct((batch * s_len, D), F32),
        grid=(batch, nl),
        in_specs=[
            pl.BlockSpec((HALF * 2 * (D // 128), 128), lambda b, j: (tile(b, j), 0)),
            pl.BlockSpec((HALF, 128), lambda b, j: (tile(b, j), 0)),
            pl.BlockSpec((HALF, D), lambda b, j: (tile(b, j), 0)),
            _const_spec((4, D)), _const_spec((6, 8, D)),
        ],
        out_specs=pl.BlockSpec((HALF, D), lambda b, j: (b * nl + j, 0)),
        compiler_params=_cparams(("parallel", "parallel")),
        name="moe_combine",
    )(yg, rout, xs, gains, tab)


def _moe_plan(rout):
    n_tok = rout.shape[0]
    n_flat = 2 * n_tok
    n_blocks = -(-n_flat // MOE_BLOCK) + N_EXP
    assert n_blocks * MOE_BLOCK == n_flat + N_EXP * MOE_BLOCK and n_flat < 2 ** 19
    flat_e = rout[:, 0:2].astype(jnp.int32).reshape(-1)
    experts = jnp.arange(N_EXP, dtype=jnp.int32)
    counts = jnp.sum((flat_e[:, None] == experts[None, :]).astype(jnp.int32), axis=0)
    padded = (counts + MOE_BLOCK - 1) // MOE_BLOCK * MOE_BLOCK
    pend = jnp.cumsum(padded)
    blk = jnp.arange(n_blocks, dtype=jnp.int32)
    block_expert = jnp.minimum(jnp.searchsorted(pend, blk * MOE_BLOCK, side='right'), N_EXP - 1).astype(jnp.int32)
    n_active = (pend[-1] // MOE_BLOCK).astype(jnp.int32).reshape(1)
    flat = jnp.arange(n_flat, dtype=jnp.int32)
    pad_i = jnp.arange(MOE_BLOCK, dtype=jnp.int32)[None, :]
    pad_key = jnp.where(pad_i < (padded - counts)[:, None], experts[:, None] * 2 ** 20 + 2 ** 19 + pad_i,
                        jnp.iinfo(jnp.int32).max)
    keys = jnp.sort(jnp.concatenate([flat_e * 2 ** 20 + flat, pad_key.reshape(-1)]))
    rows = (keys & (2 ** 20 - 1)).reshape(n_blocks, MOE_BLOCK)
    valid = rows < 2 ** 19
    tok3 = (jnp.where(valid, rows // 2, 0) * (D // 128)).reshape(n_blocks, 1, MOE_BLOCK)
    dummy = n_flat + pad_i
    dst = jnp.where(valid, rows, dummy)
    dst3 = (jnp.concatenate([dummy, dst], axis=0) * (D // 128)).reshape(n_blocks + 1, 1, MOE_BLOCK)
    return block_expert, n_active, tok3, dst3, n_flat + MOE_BLOCK


def _rot_half_cols(w):
    half = A_ROPE // 2
    g = w.reshape(w.shape[0], -1, A_ROPE)
    return jnp.concatenate([-g[..., half:], g[..., :half]], axis=-1).reshape(w.shape)


def _pad_groups(w, width, to):
    k = w.shape[0]
    g = w.reshape(k, -1, width)
    return jnp.pad(g, ((0, 0), (0, 0), (0, to - width))).reshape(k, -1)


def _layer0_inproj_weight(w_in):
    splits = np.cumsum([A_QLORA, A_KVLORA, A_ROPE, M_HEADS * M_QK, M_HEADS * M_QK, M_HEADS * M_V, M_HEADS * M_V])
    cq, ckv, kr, mq, mk, mv, og, gt = jnp.split(w_in, [int(s) for s in splits], axis=1)
    pad64 = lambda a: jnp.pad(a, ((0, 0), (0, 64)))
    cols = [cq, ckv, pad64(kr), mv, _pad_groups(mq, M_QK, M_QKP), _pad_groups(mk, M_QK, M_QKP), og,
            jnp.pad(gt, ((0, 0), (0, 128 - gt.shape[1]))), pad64(_rot_half_cols(kr))]
    w = jnp.concatenate(cols, axis=1)
    assert w.shape[1] == P0_N
    return w.astype(BF16)


def _rope_tables(batch, s_len):
    t = jnp.arange(s_len)
    row = (t // GRID_W).astype(F32)
    col = (t % GRID_W).astype(F32)
    n_freq = A_ROPE // 4
    inv_freq = ROPE_THETA ** (-jnp.arange(n_freq, dtype=F32) / n_freq)
    ang = jnp.concatenate([row[:, None] * inv_freq, col[:, None] * inv_freq], axis=-1)
    ang = jnp.concatenate([ang, ang], axis=-1)

    def table(vals, fill):
        lat = jnp.pad(vals, ((0, 0), (0, 128 - A_ROPE)), constant_values=fill)
        one = jnp.concatenate([jnp.full((CTX, 128), fill, F32), lat], axis=0)
        return jnp.tile(one, (batch, 1))

    return table(jnp.cos(ang), 1.0), table(jnp.sin(ang), 0.0)


def kernel(x, c, ctx, c_ctx, w_mod, b_mod, norm_gains, w_in_ab, mla_q_gain, mla_w_uq, mla_kv_gain, mla_w_ukv,
           ml_conv_w, ml_conv_b, ml_gate_b, ml_head_gain, w_out_ab, ffn_w_gu, ffn_w_dn, w_in_c, hg_lb_logits,
           hg_head_gain, w_out_c, moe_router, moe_w_gu, moe_w_dn):
    batch, s_len, d = x.shape
    assert d == D and ctx.shape[1] == CTX and batch == 2
    assert s_len % 512 == 0 and (batch * (CTX + s_len)) % TM == 0
    t_len = CTX + s_len
    per = t_len // HALF

    xs = jnp.concatenate([ctx, x], axis=1).reshape(batch * t_len, D)
    cc = jnp.concatenate([c, c_ctx[None, :], jnp.zeros((8 - batch - 1, D), F32)], axis=0)
    tabs = _mod_table(cc, w_mod, b_mod)

    gains = norm_gains[0]
    tab = tabs[0]
    p0 = _inproj(xs, gains, tab, _layer0_inproj_weight(w_in_ab[0]), per, P0_N // 2)

    cos, sin = _rope_tables(batch, s_len)
    w_uq = mla_w_uq[0].reshape(A_QLORA, A_HEADS, A_NOPE + A_ROPE)
    wq_nope = w_uq[:, :, :A_NOPE].reshape(A_QLORA, A_HEADS * A_NOPE)
    wq_rope = w_uq[:, :, A_NOPE:].reshape(A_QLORA, A_HEADS * A_ROPE)
    w_ukv = mla_w_ukv[0].reshape(A_KVLORA, A_HEADS, A_NOPE + A_V)
    q, k, v = _mla_prep(
        p0, cos, sin, mla_q_gain[0][None, :], mla_kv_gain[0][None, :],
        wq_nope.astype(BF16),
        _pad_groups(wq_rope, A_ROPE, 128).astype(BF16),
        _pad_groups(_rot_half_cols(wq_rope), A_ROPE, 128).astype(BF16),
        w_ukv[:, :, :A_NOPE].reshape(A_KVLORA, -1).astype(BF16),
        w_ukv[:, :, A_NOPE:].reshape(A_KVLORA, -1).astype(BF16),
        batch, t_len)
    attn = _attention(q, k, v).reshape(batch * t_len, A_HEADS * A_V)

    conv_w = jnp.concatenate([_pad_groups(ml_conv_w[0][:, :M_HEADS * M_QK], M_QK, M_QKP),
                              _pad_groups(ml_conv_w[0][:, M_HEADS * M_QK:], M_QK, M_QKP)], axis=1)
    conv_b = jnp.concatenate([_pad_groups(ml_conv_b[0][None, :M_HEADS * M_QK], M_QK, M_QKP),
                              _pad_groups(ml_conv_b[0][None, M_HEADS * M_QK:], M_QK, M_QKP)], axis=1)
    post = jnp.concatenate([jnp.ones((1, 512), F32), jnp.full((1, 512), M_QK ** -0.5, F32)], axis=1)
    qk = _mlstm_conv(p0, conv_w, conv_b, post, t_len)
    gate_bias = jnp.pad(ml_gate_b[0][None, :], ((0, 0), (0, 128 - 4 * M_HEADS)))
    hf, hb = _mlstm_scan(qk, p0, gate_bias, batch, t_len)

    w_out = w_out_ab[0].astype(BF16)
    xs = _ab_out(xs, attn, hf, hb, p0, ml_head_gain[0][None, :], w_out[:A_HEADS * A_V], w_out[A_HEADS * A_V:],
                 gains, tab, per)
    xs = _ffn(xs, gains, tab, ffn_w_gu[0].astype(BF16), ffn_w_dn[0].astype(BF16), per)

    gains = norm_gains[1]
    tab = tabs[1]
    lbs = jnp.cumsum(jax.nn.softmax(hg_lb_logits.astype(F32), axis=0), axis=0)
    lb_row = (lbs - lbs[0])[1][None, :]
    p1 = _inproj(xs, gains, tab, w_in_c[0].astype(BF16), per, 1024)
    of, ob = _hgrn_scan(p1, lb_row, batch, t_len)
    xs = _c_out(xs, of, ob, p1, hg_head_gain[0][None, :], w_out_c[0].astype(BF16), gains, tab, per)

    h, rout = _router(xs, gains, tab, moe_router[0].T, per)
    block_expert, n_active, tok3, dst3, n_rows_out = _moe_plan(rout)
    yg = _experts(block_expert, n_active, tok3, dst3, h, moe_w_gu[0].astype(BF16), moe_w_dn[0].astype(BF16), n_rows_out)
    out = _combine_latents(yg, rout, xs, gains, tab, batch, s_len, per)
    return out.reshape(batch, s_len, D)
```

```python
import functools

import jax
import jax.numpy as jnp
import numpy as np
from jax import lax
from jax.experimental import pallas as pl
from jax.experimental.pallas import tpu as pltpu

F32 = jnp.float32
BF16 = jnp.bfloat16
HIGHEST = lax.Precision.HIGHEST
LOG2E = 1.4426950408889634

D = 1024
CTX = 256
EPS = 1e-6
HALF = 256
TM = 512
VMEM_LIMIT = 56 * 1024 * 1024

A_HEADS = 4
A_QLORA = 256
A_KVLORA = 128
A_NOPE = 128
A_ROPE = 64
A_V = 128
A_HD = 256
A_SCALE = (A_NOPE + A_ROPE) ** -0.5
ROPE_THETA = 10000.0
GRID_W = 64
A_QSCALE = A_SCALE * LOG2E
A_TQ = 256
A_TK = 4096
A_RS = 32

M_HEADS = 4
M_QK = 64
M_V = 128
M_CHUNK = 128
M_STEP = 2
M_QKP = 128

G_HEADS = 8
G_DK = 128
G_DV = 128
G_CHUNK = 64
G_STEP = 4
G_SUB = 8

D_FF = 2816
N_EXP = 8
MOE_BLOCK = 512
FF_CHUNK = 256

P0_CQ = 0
P0_CKV = 256
P0_KR = 384
P0_MV = 512
P0_MQ = 1024
P0_MK = 1536
P0_OG = 2048
P0_GATE = 2560
P0_KRR = 2688
P0_N = 2816


def _cparams(sem):
    return pltpu.CompilerParams(dimension_semantics=sem, vmem_limit_bytes=VMEM_LIMIT)


def _const_spec(shape):
    nd = len(shape)
    return pl.BlockSpec(shape, lambda *_: (0,) * nd, pipeline_mode=pl.Buffered(1))


def _rms(x, gain_row):
    ms = jnp.mean(x * x, axis=-1, keepdims=True)
    return x * lax.rsqrt(ms + EPS) * gain_row


def _mod_index(half_idx, per):
    return jnp.where(half_idx % per == 0, 2, half_idx // per)


def _silu(x):
    return x * _sigmoid(x)


def _sigmoid(x):
    return 1.0 / (1.0 + jnp.exp(-x))


def _running_sum(tri01, x):
    p0 = x.astype(BF16)
    x1 = x - p0.astype(F32)
    p1 = x1.astype(BF16)
    p2 = (x1 - p1.astype(F32)).astype(BF16)
    return (jnp.dot(tri01, p0, preferred_element_type=F32) + jnp.dot(tri01, p1, preferred_element_type=F32)
            + jnp.dot(tri01, p2, preferred_element_type=F32))


def _norm_mod_rows(x, gain_row, tab_ref, k_shift, k_scale, idx):
    shift = tab_ref[k_shift, pl.ds(idx, 1), :]
    scale = tab_ref[k_scale, pl.ds(idx, 1), :]
    return _rms(x, gain_row) * (1.0 + scale) + shift


def _norm_mod_tile(x_ref, gain_row, tab_ref, k_shift, k_scale, tile, per, rows):
    parts = []
    for s in range(rows // HALF):
        idx = _mod_index(tile * (rows // HALF) + s, per)
        parts.append(_norm_mod_rows(x_ref[s * HALF:(s + 1) * HALF, :], gain_row, tab_ref, k_shift, k_scale, idx))
    return parts[0] if len(parts) == 1 else jnp.concatenate(parts, axis=0)


def _gated_residual_tile(x_ref, y, gain_row, tab_ref, k_gate, tile, per, rows, o_ref):
    yn = _rms(y, gain_row)
    for s in range(rows // HALF):
        idx = _mod_index(tile * (rows // HALF) + s, per)
        gate = tab_ref[k_gate, pl.ds(idx, 1), :]
        sl = slice(s * HALF, (s + 1) * HALF)
        o_ref[sl, :] = x_ref[sl, :] + gate * yn[sl, :]


def _mod_kernel(c_ref, w_ref, b_ref, o_ref):
    a = _silu(c_ref[...])
    o_ref[0, 0] = jnp.dot(a, w_ref[0], precision=HIGHEST, preferred_element_type=F32) + b_ref[0, 0]


def _mod_table(cc, w_mod, b_mod):
    depth = w_mod.shape[0]
    return pl.pallas_call(
        _mod_kernel,
        out_shape=jax.ShapeDtypeStruct((depth, 6, 8, D), F32),
        grid=(depth, 6),
        in_specs=[
            pl.BlockSpec((8, D), lambda l, k: (0, 0)),
            pl.BlockSpec((1, D, D), lambda l, k: (l, 0, k)),
            pl.BlockSpec((1, 1, 1, D), lambda l, k: (l, k, 0, 0)),
        ],
        out_specs=pl.BlockSpec((1, 1, 8, D), lambda l, k: (l, k, 0, 0)),
        compiler_params=_cparams(("arbitrary", "arbitrary")),
        name="mod_table",
    )(cc, w_mod, b_mod.reshape(depth, 6, 1, D))


def _inproj_kernel(x_ref, gains_ref, tab_ref, w_ref, o_ref, *, per, nchunk):
    i = pl.program_id(0)
    h = _norm_mod_tile(x_ref, gains_ref[0:1, :], tab_ref, 0, 1, i, per, TM).astype(BF16)
    n = w_ref.shape[1]
    for n0 in range(0, n, nchunk):
        o_ref[:, n0:n0 + nchunk] = jnp.dot(h, w_ref[:, n0:n0 + nchunk], preferred_element_type=F32)


def _inproj(xs, gains, tab, w, per, nchunk):
    rows = xs.shape[0]
    n = w.shape[1]
    return pl.pallas_call(
        functools.partial(_inproj_kernel, per=per, nchunk=nchunk),
        out_shape=jax.ShapeDtypeStruct((rows, n), F32),
        grid=(rows // TM,),
        in_specs=[
            pl.BlockSpec((TM, D), lambda i: (i, 0)),
            _const_spec((4, D)),
            _const_spec((6, 8, D)),
            _const_spec((D, n)),
        ],
        out_specs=pl.BlockSpec((TM, n), lambda i: (i, 0)),
        compiler_params=_cparams(("parallel",)),
        name="inproj",
    )(xs, gains, tab, w)


def _mla_prep_kernel(cq_ref, ckr_ref, krr_ref, cos_ref, sin_ref, qg_ref, kvg_ref, wqn_ref, wqr_ref, wqrr_ref,
                     wk_ref, wv_ref, q_ref, k_ref, v_ref):
    qn = _rms(cq_ref[...], qg_ref[...]).astype(BF16)
    ckr = ckr_ref[...]
    kvn = _rms(ckr[:, :A_KVLORA], kvg_ref[...]).astype(BF16)
    cos = cos_ref[...]
    sin = sin_ref[...]
    q_nope = jnp.dot(qn, wqn_ref[...], preferred_element_type=F32)
    q_r = jnp.dot(qn, wqr_ref[...], preferred_element_type=F32)
    q_rr = jnp.dot(qn, wqrr_ref[...], preferred_element_type=F32)
    k_nope = jnp.dot(kvn, wk_ref[...], preferred_element_type=F32)
    vv = jnp.dot(kvn, wv_ref[...], preferred_element_type=F32)
    k_rope = (ckr[:, A_KVLORA:] * cos + krr_ref[...] * sin).astype(BF16)
    lane = lax.broadcasted_iota(jnp.int32, (HALF, A_V), 1)
    ones_col = jnp.where(lane == 0, 1.0, 0.0).astype(BF16)
    for h in range(A_HEADS):
        sl = slice(h * 128, (h + 1) * 128)
        q_rope = q_r[:, sl] * cos + q_rr[:, sl] * sin
        q_ref[0, h, :, 0:128] = (q_nope[:, sl] * A_QSCALE).astype(BF16)
        q_ref[0, h, :, 128:256] = (q_rope * A_QSCALE).astype(BF16)
        k_ref[0, h, :, 0:128] = k_nope[:, sl].astype(BF16)
        k_ref[0, h, :, 128:256] = k_rope
        v_ref[0, h, :, 0:A_V] = vv[:, sl].astype(BF16)
        v_ref[0, h, :, A_V:2 * A_V] = ones_col


def _mla_prep(p0, cos, sin, q_gain, kv_gain, wqn, wqr, wqrr, wk, wv, batch, t_len):
    rows = p0.shape[0]
    per = t_len // HALF
    qk_shape = jax.ShapeDtypeStruct((batch, A_HEADS, t_len, A_HD), BF16)
    v_shape = jax.ShapeDtypeStruct((batch, A_HEADS, t_len, 2 * A_V), BF16)
    out_map = lambda i: (i // per, 0, i % per, 0)
    return pl.pallas_call(
        _mla_prep_kernel,
        out_shape=(qk_shape, qk_shape, v_shape),
        grid=(rows // HALF,),
        in_specs=[
            pl.BlockSpec((HALF, 256), lambda i: (i, P0_CQ // 256)),
            pl.BlockSpec((HALF, 256), lambda i: (i, P0_CKV // 256)),
            pl.BlockSpec((HALF, 128), lambda i: (i, P0_KRR // 128)),
            pl.BlockSpec((HALF, 128), lambda i: (i, 0)),
            pl.BlockSpec((HALF, 128), lambda i: (i, 0)),
            _const_spec((1, A_QLORA)),
            _const_spec((1, A_KVLORA)),
            _const_spec((A_QLORA, 512)),
            _const_spec((A_QLORA, 512)),
            _const_spec((A_QLORA, 512)),
            _const_spec((A_KVLORA, 512)),
            _const_spec((A_KVLORA, 512)),
        ],
        out_specs=(
            pl.BlockSpec((1, A_HEADS, HALF, A_HD), out_map),
            pl.BlockSpec((1, A_HEADS, HALF, A_HD), out_map),
            pl.BlockSpec((1, A_HEADS, HALF, 2 * A_V), out_map),
        ),
        compiler_params=_cparams(("parallel",)),
        name="mla_prep",
    )(p0, p0, p0, cos, sin, q_gain, kv_gain, wqn, wqr, wqrr, wk, wv)


def _attn_kernel(q_ref, qn_ref, k_ref, v_ref, o_ref, s_ref, c_ref, p_ref, m_ref, acc_ref, *, n_lat_chunks, tk):
    qi = pl.program_id(2)
    q = q_ref[0, 0]
    qn = qn_ref[0, 0]

    def scores(qq, start, width):
        return lax.dot_general(qq, k_ref[0, 0, pl.ds(start, width), :], (((1,), (1,)), ((), ())),
                               preferred_element_type=F32)

    def softmax_step(s_view, start, width, first):
        if not first:
            m_all = m_ref[...]
        m_parts, a_parts = [], []
        for r0 in range(0, A_TQ, A_RS):
            rows = slice(r0, r0 + A_RS)
            s = s_view[rows, :]
            s_max = jnp.max(s, axis=-1, keepdims=True)
            if first:
                m_new = s_max
            else:
                m_prev = m_all[rows]
                m_new = jnp.maximum(m_prev, s_max)
                a_parts.append(jnp.exp2(m_prev - m_new))
            m_parts.append(m_new)
            p_ref[rows, 0:width] = jnp.exp2(s - m_new).astype(BF16)
        m_ref[...] = jnp.concatenate(m_parts, axis=0)
        pv = jnp.dot(p_ref[:, 0:width], v_ref[0, 0, pl.ds(start, width), :], preferred_element_type=F32)
        acc_ref[...] = pv if first else jnp.concatenate(a_parts, axis=0) * acc_ref[...] + pv

    def kstart(c):
        return pl.multiple_of(CTX + c * tk, 256)

    def finish():
        acc = acc_ref[...]
        o_ref[0] = acc[:, 0:A_V] / acc[:, A_V:A_V + 1]

    @pl.when(qi == 0)
    def _():
        c_ref[...] = scores(q, 0, CTX)
        s_ref[1] = scores(qn, CTX, tk)
        softmax_step(c_ref, 0, CTX, True)
        finish()
        c_ref[...] = scores(qn, 0, CTX)

    @pl.when(qi > 0)
    def _():
        s_ref[0] = scores(q, kstart(1), tk)
        softmax_step(c_ref, 0, CTX, True)
        softmax_step(s_ref.at[1], kstart(0), tk, False)

        def body(c2, carry):
            c = 2 * c2 + 1
            s_ref[1] = scores(q, kstart(c + 1), tk)
            softmax_step(s_ref.at[0], kstart(c), tk, False)
            s_ref[0] = scores(q, kstart(c + 2), tk)
            softmax_step(s_ref.at[1], kstart(c + 1), tk, False)
            return carry

        lax.fori_loop(0, n_lat_chunks // 2 - 1, body, 0)
        s_ref[1] = scores(qn, CTX, tk)
        softmax_step(s_ref.at[0], kstart(n_lat_chunks - 1), tk, False)
        c_ref[...] = scores(qn, 0, CTX)
        finish()


def _attention(q, k, v):
    batch, heads, t_len, _ = q.shape
    tk = min(A_TK, (t_len - CTX) // 2)
    n_lat_chunks = (t_len - CTX) // tk
    n_q = t_len // A_TQ
    assert n_lat_chunks % 2 == 0 and n_lat_chunks * tk == t_len - CTX and tk % 256 == 0
    return pl.pallas_call(
        functools.partial(_attn_kernel, n_lat_chunks=n_lat_chunks, tk=tk),
        out_shape=jax.ShapeDtypeStruct((batch, t_len, heads * A_V), F32),
        grid=(batch, heads, n_q),
        in_specs=[
            pl.BlockSpec((1, 1, A_TQ, A_HD), lambda b, h, i: (b, h, i, 0)),
            pl.BlockSpec((1, 1, A_TQ, A_HD), lambda b, h, i: (b, h, jnp.minimum(i + 1, n_q - 1), 0)),
            pl.BlockSpec((1, 1, t_len, A_HD), lambda b, h, i: (b, h, 0, 0)),
            pl.BlockSpec((1, 1, t_len, 2 * A_V), lambda b, h, i: (b, h, 0, 0)),
        ],
        out_specs=pl.BlockSpec((1, A_TQ, A_V), lambda b, h, i: (b, i, h)),
        scratch_shapes=[
            pltpu.VMEM((2, A_TQ, tk), F32),
            pltpu.VMEM((A_TQ, CTX), F32),
            pltpu.VMEM((A_TQ, tk), BF16),
            pltpu.VMEM((A_TQ, 1), F32),
            pltpu.VMEM((A_TQ, 2 * A_V), F32),
        ],
        compiler_params=_cparams(("parallel", "parallel", "arbitrary")),
        name="mla_attention",
    )(q, q, k, v)


def _conv_kernel(x_ref, prev_ref, next_ref, w_ref, b_ref, post_ref, o_ref, *, t_len):
    i = pl.program_id(0)
    x = x_ref[...]
    row = lax.broadcasted_iota(jnp.int32, (TM, 1), 0)
    pos = (i * TM + row) % t_len
    x_prev = jnp.where(row == 0, prev_ref[7:8, :], pltpu.roll(x, 1, axis=0))
    x_next = jnp.where(row == TM - 1, next_ref[0:1, :], pltpu.roll(x, TM - 1, axis=0))
    seq_start = (pos == 0) | (pos == CTX)
    seq_end = (pos == CTX - 1) | (pos == t_len - 1)
    x_prev = jnp.where(seq_start, 0.0, x_prev)
    x_next = jnp.where(seq_end, 0.0, x_next)
    y = w_ref[0:1, :] * x_prev + w_ref[1:2, :] * x + w_ref[2:3, :] * x_next + b_ref[...]
    o_ref[...] = _silu(y) * post_ref[...]


def _mlstm_conv(p0, conv_w, conv_b, post, t_len):
    rows = p0.shape[0]
    n8 = rows // 8
    cb = P0_MQ // 1024
    return pl.pallas_call(
        functools.partial(_conv_kernel, t_len=t_len),
        out_shape=jax.ShapeDtypeStruct((rows, 1024), F32),
        grid=(rows // TM,),
        in_specs=[
            pl.BlockSpec((TM, 1024), lambda i: (i, cb)),
            pl.BlockSpec((8, 1024), lambda i: (jnp.maximum(i * (TM // 8) - 1, 0), cb)),
            pl.BlockSpec((8, 1024), lambda i: (jnp.minimum((i + 1) * (TM // 8), n8 - 1), cb)),
            _const_spec((3, 1024)),
            _const_spec((1, 1024)),
            _const_spec((1, 1024)),
        ],
        out_specs=pl.BlockSpec((TM, 1024), lambda i: (i, 0)),
        compiler_params=_cparams(("parallel",)),
        name="mlstm_conv",
    )(p0, p0, p0, conv_w, conv_b, post)


def _log_sigmoid(x):
    return jnp.minimum(x, 0.0) - jnp.log(1.0 + jnp.exp(-jnp.abs(x)))


def _mlstm_kernel(qkf_ref, qkb_ref, vf_ref, vb_ref, gf_ref, gb_ref, gbias_ref, hf_ref, hb_ref, c_ref, m_ref):
    j = pl.program_id(1)
    L = M_CHUNK

    @pl.when(j == 0)
    def _():
        c_ref[...] = jnp.zeros_like(c_ref)
        m_ref[...] = jnp.zeros_like(m_ref)

    r_i = lax.broadcasted_iota(jnp.int32, (L, L), 0)
    c_i = lax.broadcasted_iota(jnp.int32, (L, L), 1)
    row = lax.broadcasted_iota(jnp.int32, (L, 128), 0)
    lane = lax.broadcasted_iota(jnp.int32, (L, 128), 1)

    for step in range(M_STEP):
        for d, (qk_ref, v_ref, g_ref, h_ref) in enumerate(((qkf_ref, vf_ref, gf_ref, hf_ref),
                                                            (qkb_ref, vb_ref, gb_ref, hb_ref))):
            rev = d == 1
            part = M_STEP - 1 - step if rev else step
            rows = slice(part * L, (part + 1) * L)
            keep = (c_i >= r_i) if rev else (c_i <= r_i)
            tri = jnp.where(keep, 1.0, 0.0).astype(BF16)
            pre = g_ref[rows, :] + gbias_ref[...]
            csum = _running_sum(tri, _log_sigmoid(pre) * LOG2E)
            g_run = pltpu.roll(csum, 124, axis=1)
            u = pre * LOG2E - g_run
            c_run = u
            k = 1
            while k < L:
                if rev:
                    shifted = jnp.where(row >= L - k, -jnp.inf, pltpu.roll(c_run, L - k, axis=0))
                else:
                    shifted = jnp.where(row < k, -jnp.inf, pltpu.roll(c_run, k, axis=0))
                c_run = jnp.maximum(c_run, shifted)
                k *= 2
            last = 0 if rev else L - 1
            m_prev = m_ref[d]
            m_run = jnp.maximum(c_run, m_prev)
            g_end = g_run[last:last + 1, :]
            m_new = g_end + jnp.maximum(m_prev, c_run[last:last + 1, :])
            w_inter_all = jnp.exp2(m_prev - m_run)
            floor_all = jnp.exp2(-(g_run + m_run))
            w_in_all = jnp.exp2(g_end + u - m_new)
            decay_all = jnp.exp2(g_end + m_prev - m_new)
            u_t = u.T
            m_ref[d] = m_new
            for h in range(M_HEADS):
                ci = 8 * d + h
                qh = qk_ref[rows, h * 128:(h + 1) * 128].astype(BF16)
                kh32 = qk_ref[rows, 512 + h * 128:512 + (h + 1) * 128]
                sc = lax.dot_general(qh, kh32.astype(BF16), (((1,), (1,)), ((), ())), preferred_element_type=F32)
                s = sc * jnp.exp2(jnp.where(keep, u_t[ci:ci + 1, :] - m_run[:, ci:ci + 1], -jnp.inf))
                ones_col = jnp.where(lane == ci, 1.0, 0.0).astype(BF16)
                v_ext = jnp.concatenate([v_ref[rows, h * 128:(h + 1) * 128].astype(BF16), ones_col], axis=1)
                c_st = c_ref[4 * d + h]
                r1 = jnp.dot(s.astype(BF16), v_ext, preferred_element_type=F32)
                r2 = jnp.dot(qh, c_st.astype(BF16), preferred_element_type=F32)
                den = r1[:, 128:256] + w_inter_all * r2[:, 128:256]
                inv = 1.0 / jnp.maximum(jnp.abs(den), floor_all)
                h_ref[rows, h * 128:(h + 1) * 128] = ((r1[:, 0:128] + w_inter_all[:, ci:ci + 1] * r2[:, 0:128])
                                                   * inv[:, ci:ci + 1])
                kw = (kh32 * w_in_all[:, ci:ci + 1]).astype(BF16)
                upd = lax.dot_general(kw, v_ext, (((0,), (0,)), ((), ())), preferred_element_type=F32)
                c_ref[4 * d + h] = decay_all[:, ci:ci + 1] * c_st + upd


def _mlstm_scan(qk, p0, gate_bias, batch, t_len):
    rows = qk.shape[0]
    blk = M_STEP * M_CHUNK
    nch = t_len // blk
    nctx = CTX // blk
    fwd = lambda b, j: b * nch + j
    bwd = lambda b, j: b * nch + jnp.where(j < nctx, nctx - 1 - j, nch + nctx - 1 - j)
    out = jax.ShapeDtypeStruct((rows, M_HEADS * M_V), F32)
    return pl.pallas_call(
        _mlstm_kernel,
        out_shape=(out, out),
        grid=(batch, nch),
        in_specs=[
            pl.BlockSpec((blk, 1024), lambda b, j: (fwd(b, j), 0)),
            pl.BlockSpec((blk, 1024), lambda b, j: (bwd(b, j), 0)),
            pl.BlockSpec((blk, 512), lambda b, j: (fwd(b, j), P0_MV // 512)),
            pl.BlockSpec((blk, 512), lambda b, j: (bwd(b, j), P0_MV // 512)),
            pl.BlockSpec((blk, 128), lambda b, j: (fwd(b, j), P0_GATE // 128)),
            pl.BlockSpec((blk, 128), lambda b, j: (bwd(b, j), P0_GATE // 128)),
            _const_spec((1, 128)),
        ],
        out_specs=(
            pl.BlockSpec((blk, 512), lambda b, j: (fwd(b, j), 0)),
            pl.BlockSpec((blk, 512), lambda b, j: (bwd(b, j), 0)),
        ),
        scratch_shapes=[
            pltpu.VMEM((2 * M_HEADS, M_QKP, 256), F32),
            pltpu.VMEM((2, 1, 128), F32),
        ],
        compiler_params=_cparams(("parallel", "arbitrary")),
        name="mlstm_scan",
    )(qk, qk, p0, p0, p0, p0, gate_bias)


def _head_rms(x, gain_row, n_heads, width):
    parts = []
    for h in range(n_heads):
        sl = slice(h * width, (h + 1) * width)
        parts.append(_rms(x[:, sl], gain_row[:, sl]))
    return jnp.concatenate(parts, axis=1)


def _ab_out_kernel(x_ref, a_ref, hf_ref, hb_ref, og_ref, hg_ref, wa_ref, wm_ref, gains_ref, tab_ref, o_ref, *, per):
    i = pl.program_id(0)
    m = _head_rms(hf_ref[...] + hb_ref[...], hg_ref[...], M_HEADS, M_V) * _sigmoid(og_ref[...])
    y = (jnp.dot(a_ref[...].astype(BF16), wa_ref[...], preferred_element_type=F32)
         + jnp.dot(m.astype(BF16), wm_ref[...], preferred_element_type=F32))
    _gated_residual_tile(x_ref, y, gains_ref[1:2, :], tab_ref, 2, i, per, TM, o_ref)


def _ab_out(xs, a, hf, hb, p0, head_gain, wa, wm, gains, tab, per):
    rows = xs.shape[0]
    row_spec = lambda w, cb=0: pl.BlockSpec((TM, w), lambda i: (i, cb))
    return pl.pallas_call(
        functools.partial(_ab_out_kernel, per=per),
        out_shape=jax.ShapeDtypeStruct((rows, D), F32),
        grid=(rows // TM,),
        in_specs=[
            row_spec(D), row_spec(512), row_spec(512), row_spec(512), row_spec(512, P0_OG // 512),
            _const_spec((1, 512)), _const_spec((512, D)), _const_spec((512, D)),
            _const_spec((4, D)), _const_spec((6, 8, D)),
        ],
        out_specs=row_spec(D),
        compiler_params=_cparams(("parallel",)),
        name="ab_out",
    )(xs, a, hf, hb, p0, head_gain, wa, wm, gains, tab)


def _swiglu(h, wgu_ref, wdn_ref, lead):
    acc = None
    for c0 in range(0, D_FF, FF_CHUNK):
        g = jnp.dot(h, wgu_ref[lead + (slice(None), slice(c0, c0 + FF_CHUNK))], preferred_element_type=F32)
        u = jnp.dot(h, wgu_ref[lead + (slice(None), slice(D_FF + c0, D_FF + c0 + FF_CHUNK))],
                    preferred_element_type=F32)
        a = (_silu(g) * u).astype(BF16)
        part = jnp.dot(a, wdn_ref[lead + (slice(c0, c0 + FF_CHUNK), slice(None))], preferred_element_type=F32)
        acc = part if acc is None else acc + part
    return acc


def _ffn_kernel(x_ref, gains_ref, tab_ref, wgu_ref, wdn_ref, o_ref, *, per):
    i = pl.program_id(0)
    h = _norm_mod_tile(x_ref, gains_ref[2:3, :], tab_ref, 3, 4, i, per, TM).astype(BF16)
    y = _swiglu(h, wgu_ref, wdn_ref, ())
    _gated_residual_tile(x_ref, y, gains_ref[3:4, :], tab_ref, 5, i, per, TM, o_ref)


def _ffn(xs, gains, tab, wgu, wdn, per):
    rows = xs.shape[0]
    return pl.pallas_call(
        functools.partial(_ffn_kernel, per=per),
        out_shape=jax.ShapeDtypeStruct((rows, D), F32),
        grid=(rows // TM,),
        in_specs=[
            pl.BlockSpec((TM, D), lambda i: (i, 0)),
            _const_spec((4, D)), _const_spec((6, 8, D)),
            _const_spec((D, 2 * D_FF)), _const_spec((D_FF, D)),
        ],
        out_specs=pl.BlockSpec((TM, D), lambda i: (i, 0)),
        compiler_params=_cparams(("parallel",)),
        name="ffn",
    )(xs, gains, tab, wgu, wdn)


def _hgrn_stream(q, v, g_ref, w_ref, sl, st, gsum, neg, lmask, rev):
    c = G_SUB
    ns = G_CHUNK // c
    blk = lambda ref, i: ref[c * i:c * (i + 1), sl]
    bcast = lambda ref, r: jnp.broadcast_to(ref[r:r + 1, sl], (c, 128))
    qb = lambda i: q[c * i:c * (i + 1)]
    bound = [bcast(g_ref, c * j if rev else c * j + c - 1) for j in range(ns)]
    khat = jnp.concatenate([jnp.exp2(bound[j] - blk(w_ref, j)) for j in range(ns)], axis=0)
    pairs = [(i, j) for i in range(ns) for j in range(ns) if (i < j if rev else i > j)]
    qst = jnp.concatenate([qb(i) * jnp.exp2(blk(g_ref, i) - bound[j]) for (i, j) in pairs], axis=0)
    cross = lax.dot_general(qst.astype(BF16), khat.astype(BF16), (((1,), (1,)), ((), ())),
                            preferred_element_type=F32)
    z_rows = []
    for i in range(ns):
        gi, qi = blk(g_ref, i), qb(i)
        z_rows.append(jnp.concatenate(
            [qi * jnp.exp2(gi + neg[s] - bcast(w_ref, c * i + s)) for s in range(c)], axis=1))
    diag = jnp.dot(jnp.concatenate(z_rows, axis=0).astype(BF16), gsum, preferred_element_type=F32)
    a_rows = []
    for i in range(ns):
        acc = diag[c * i:c * (i + 1)] * lmask[i]
        for p, (ii, j) in enumerate(pairs):
            if ii == i:
                acc = acc + cross[c * p:c * (p + 1)] * lmask[j]
        a_rows.append(acc)
    a = jnp.concatenate(a_rows, axis=0).astype(BF16)
    g_end = g_ref[0:1, sl] if rev else g_ref[G_CHUNK - 1:G_CHUNK, sl]
    o = (jnp.dot(a, v.astype(BF16), preferred_element_type=F32)
         + lax.dot_general((q * jnp.exp2(g_ref[:, sl])).astype(BF16), st.astype(BF16), (((1,), (1,)), ((), ())),
                           preferred_element_type=F32))
    kw = jnp.exp2(g_end - w_ref[:, sl]).astype(BF16)
    st_new = st * jnp.exp2(g_end) + lax.dot_general(v.astype(BF16), kw, (((0,), (0,)), ((), ())),
                                                     preferred_element_type=F32)
    return o, st_new


def _hgrn_kernel(qf_ref, zf_ref, vf_ref, qb_ref, zb_ref, vb_ref, lb_ref, of_ref, ob_ref, st_ref, g_ref, w_ref):
    j = pl.program_id(1)
    L = G_CHUNK

    @pl.when(j == 0)
    def _():
        st_ref[...] = jnp.zeros_like(st_ref)

    r_i = lax.broadcasted_iota(jnp.int32, (L, L), 0)
    c_i = lax.broadcasted_iota(jnp.int32, (L, L), 1)
    lb = lb_ref[...]
    log_1mlb = jnp.log(1.0 - lb)
    gs_r = lax.broadcasted_iota(jnp.int32, (G_SUB * 128, L), 0)
    gs_c = lax.broadcasted_iota(jnp.int32, (G_SUB * 128, L), 1)
    gsum = jnp.where(gs_r // 128 == gs_c % G_SUB, 1.0, 0.0).astype(BF16)
    sub = lax.broadcasted_iota(jnp.int32, (G_SUB, 128), 0)
    lane = lax.broadcasted_iota(jnp.int32, (G_SUB, L), 1)
    lmask = [jnp.where(lane // G_SUB == jj, 1.0, 0.0) for jj in range(L // G_SUB)]
    for step in range(G_STEP):
        for d, (q_ref, z_ref, v_ref, o_ref) in enumerate(((qf_ref, zf_ref, vf_ref, of_ref),
                                                           (qb_ref, zb_ref, vb_ref, ob_ref))):
            rev = d == 1
            part = G_STEP - 1 - step if rev else step
            rows = slice(part * L, (part + 1) * L)
            neg = [jnp.where((sub <= s) if rev else (sub >= s), 0.0, -jnp.inf) for s in range(G_SUB)]
            tri = jnp.where((c_i >= r_i) if rev else (c_i <= r_i), 1.0, 0.0).astype(BF16)
            z = z_ref[rows, :]
            e = jnp.exp(-jnp.abs(z))
            r = 1.0 / (1.0 + e)
            sig_pos = jnp.where(z >= 0, r, e * r)
            log_f = jnp.log(lb + (1.0 - lb) * sig_pos)
            log_k = log_1mlb - jnp.maximum(z, 0.0) - jnp.log(1.0 + e)
            g_all = _running_sum(tri, log_f * LOG2E)
            k_idx = step * 2 + d
            g_ref[k_idx] = g_all
            w_ref[k_idx] = g_all - LOG2E * log_k
            q_all = _silu(q_ref[rows, :])
            for h in range(G_HEADS):
                sl = slice(h * 128, (h + 1) * 128)
                o, st_new = _hgrn_stream(q_all[:, sl], v_ref[rows, sl], g_ref.at[k_idx], w_ref.at[k_idx], sl,
                                         st_ref[d * G_HEADS + h], gsum, neg, lmask, rev)
                o_ref[rows, sl] = o
                st_ref[d * G_HEADS + h] = st_new


def _hgrn_scan(p1, lb_row, batch, t_len):
    rows = p1.shape[0]
    blk = G_STEP * G_CHUNK
    nst = t_len // blk
    nctx = CTX // blk
    fwd = lambda b, j: b * nst + j
    bwd = lambda b, j: b * nst + jnp.where(j < nctx, nctx - 1 - j, nst + nctx - 1 - j)
    spec = lambda order, cb: pl.BlockSpec((blk, 1024), lambda b, j: (order(b, j), cb))
    out = jax.ShapeDtypeStruct((rows, G_HEADS * G_DV), F32)
    return pl.pallas_call(
        _hgrn_kernel,
        out_shape=(out, out),
        grid=(batch, nst),
        in_specs=[spec(fwd, 0), spec(fwd, 1), spec(fwd, 3), spec(bwd, 0), spec(bwd, 2), spec(bwd, 3),
                  _const_spec((1, 1024))],
        out_specs=(spec(fwd, 0), spec(bwd, 0)),
        scratch_shapes=[pltpu.VMEM((2 * G_HEADS, G_DV, G_DK), F32),
                        pltpu.VMEM((2 * G_STEP, G_CHUNK, 1024), F32),
                        pltpu.VMEM((2 * G_STEP, G_CHUNK, 1024), F32)],
        compiler_params=_cparams(("parallel", "arbitrary")),
        name="hgrn_scan",
    )(p1, p1, p1, p1, p1, p1, lb_row)


def _c_out_kernel(x_ref, of_ref, ob_ref, hg_ref, gain_ref, w_ref, gains_ref, tab_ref, o_ref, *, per):
    i = pl.program_id(0)
    o = _head_rms(of_ref[...] + ob_ref[...], gain_ref[...], G_HEADS, G_DV) * _silu(hg_ref[...])
    y = jnp.dot(o.astype(BF16), w_ref[...], preferred_element_type=F32)
    _gated_residual_tile(x_ref, y, gains_ref[1:2, :], tab_ref, 2, i, per, TM, o_ref)


def _c_out(xs, of, ob, p1, head_gain, w, gains, tab, per):
    rows = xs.shape[0]
    row_spec = lambda cb=0: pl.BlockSpec((TM, D), lambda i: (i, cb))
    return pl.pallas_call(
        functools.partial(_c_out_kernel, per=per),
        out_shape=jax.ShapeDtypeStruct((rows, D), F32),
        grid=(rows // TM,),
        in_specs=[row_spec(), row_spec(), row_spec(), row_spec(4), _const_spec((1, D)), _const_spec((D, D)),
                  _const_spec((4, D)), _const_spec((6, 8, D))],
        out_specs=row_spec(),
        compiler_params=_cparams(("parallel",)),
        name="c_out",
    )(xs, of, ob, p1, head_gain, w, gains, tab)


def _router_kernel(x_ref, gains_ref, tab_ref, wr_ref, h_ref, r_ref, *, per):
    i = pl.program_id(0)
    h = _norm_mod_tile(x_ref, gains_ref[2:3, :], tab_ref, 3, 4, i, per, TM)
    for j in range(D // 128):
        h_ref[pl.ds(j, TM, stride=D // 128), :] = h[:, j * 128:(j + 1) * 128]
    logits = lax.dot_general(wr_ref[...], h, (((1,), (1,)), ((), ())), precision=HIGHEST,
                             preferred_element_type=F32)
    eid = lax.broadcasted_iota(jnp.int32, logits.shape, 0).astype(F32)
    m1 = jnp.max(logits, axis=0, keepdims=True)
    i1 = jnp.min(jnp.where(logits == m1, eid, float(N_EXP)), axis=0, keepdims=True)
    rest = jnp.where(eid == i1, -jnp.inf, logits)
    m2 = jnp.max(rest, axis=0, keepdims=True)
    i2 = jnp.min(jnp.where(rest == m2, eid, float(N_EXP)), axis=0, keepdims=True)
    e2 = jnp.exp(m2 - m1)
    g1 = 1.0 / (1.0 + e2)
    g2 = e2 * g1
    rid = lax.broadcasted_iota(jnp.int32, (128, TM), 0)
    packed = jnp.where(rid == 0, i1, jnp.where(rid == 1, i2, jnp.where(rid == 2, g1, jnp.where(rid == 3, g2, 0.0))))
    r_ref[...] = packed.T


def _router(xs, gains, tab, wr_t, per):
    rows = xs.shape[0]
    return pl.pallas_call(
        functools.partial(_router_kernel, per=per),
        out_shape=(jax.ShapeDtypeStruct((rows * (D // 128), 128), F32), jax.ShapeDtypeStruct((rows, 128), F32)),
        grid=(rows // TM,),
        in_specs=[pl.BlockSpec((TM, D), lambda i: (i, 0)), _const_spec((4, D)), _const_spec((6, 8, D)),
                  _const_spec((N_EXP, D))],
        out_specs=(pl.BlockSpec((TM * (D // 128), 128), lambda i: (i, 0)), pl.BlockSpec((TM, 128), lambda i: (i, 0))),
        compiler_params=_cparams(("parallel",)),
        name="moe_router",
    )(xs, gains, tab, wr_t)


def _tile_rows(ref, index):
    return jnp.concatenate([ref[index(j)] for j in range(D // 128)], axis=1)


def _expert_kernel(be_ref, nact_ref, tok_ref, dstx_ref, h_hbm, wgu_ref, wdn_ref, yg_hbm,
                   xbuf, ybuf, gsem, ssem):
    b = pl.program_id(0)
    nb = pl.num_programs(0)
    slot = b % 2
    other = 1 - slot
    nxt = jnp.minimum(b + 1, nb - 1)

    def gather_rows(blk, dst_slot, r0, r1):
        for r in range(r0, r1):
            src = h_hbm.at[pl.ds(pl.multiple_of(tok_ref[blk * MOE_BLOCK + r], 8), 8)]
            pltpu.make_async_copy(src, xbuf.at[dst_slot, pl.ds(8 * r, 8)], gsem.at[dst_slot]).start(priority=1)

    def scatter_rows(blk, src_slot, r0, r1):
        for r in range(r0, r1):
            dst = yg_hbm.at[pl.ds(pl.multiple_of(dstx_ref[blk * MOE_BLOCK + r], 8), 8)]
            pltpu.make_async_copy(ybuf.at[src_slot, pl.ds(8 * r, 8)], dst, ssem.at[src_slot]).start()

    def wait_gather(s):
        pltpu.make_async_copy(h_hbm.at[pl.ds(0, MOE_BLOCK * (D // 128))], xbuf.at[s], gsem.at[s]).wait()

    def wait_scatter(s):
        pltpu.make_async_copy(ybuf.at[s], yg_hbm.at[pl.ds(0, MOE_BLOCK * (D // 128))], ssem.at[s]).wait()

    @pl.when(b == 0)
    def _():
        ybuf[...] = jnp.zeros_like(ybuf)
        gather_rows(0, 0, 0, MOE_BLOCK)

    wait_gather(slot)

    @pl.when(b >= 1)
    def _():
        wait_scatter(slot)

    n_chunks = D_FF // FF_CHUNK
    per_g = -(-MOE_BLOCK // (n_chunks // 2))
    per_s = -(-MOE_BLOCK // (n_chunks - n_chunks // 2))

    @pl.when(b < nact_ref[0])
    def _():
        x = _tile_rows(xbuf, lambda j: (slot, pl.ds(j, MOE_BLOCK, stride=D // 128), slice(None))).astype(BF16)
        acc = None
        for ci in range(n_chunks):
            c0 = ci * FF_CHUNK
            g = jnp.dot(x, wgu_ref[0, :, c0:c0 + FF_CHUNK], preferred_element_type=F32)
            u = jnp.dot(x, wgu_ref[0, :, D_FF + c0:D_FF + c0 + FF_CHUNK], preferred_element_type=F32)
            a = (_silu(g) * u).astype(BF16)
            part = jnp.dot(a, wdn_ref[0, c0:c0 + FF_CHUNK, :], preferred_element_type=F32)
            acc = part if acc is None else acc + part
            if ci < n_chunks // 2:
                gather_rows(nxt, other, ci * per_g, min((ci + 1) * per_g, MOE_BLOCK))
            else:
                cs = ci - n_chunks // 2
                scatter_rows(b, other, cs * per_s, min((cs + 1) * per_s, MOE_BLOCK))
        for j in range(D // 128):
            ybuf[slot, pl.ds(j, MOE_BLOCK, stride=D // 128), :] = acc[:, j * 128:(j + 1) * 128]

    @pl.when(b >= nact_ref[0])
    def _():
        gather_rows(nxt, other, 0, MOE_BLOCK)
        scatter_rows(b, other, 0, MOE_BLOCK)

    @pl.when(b == nb - 1)
    def _():
        wait_gather(other)
        wait_scatter(other)
        scatter_rows(nb, slot, 0, MOE_BLOCK)
        wait_scatter(slot)


def _experts(block_expert, n_active, tok3, dst3, h3, wgu, wdn, n_rows_out):
    n_blocks = block_expert.shape[0]
    grid_spec = pltpu.PrefetchScalarGridSpec(
        num_scalar_prefetch=4,
        grid=(n_blocks,),
        in_specs=[
            pl.BlockSpec(memory_space=pl.ANY),
            pl.BlockSpec((1, D, 2 * D_FF), lambda b, be, na, tk, ds: (be[b], 0, 0)),
            pl.BlockSpec((1, D_FF, D), lambda b, be, na, tk, ds: (be[b], 0, 0)),
        ],
        out_specs=pl.BlockSpec(memory_space=pl.ANY),
        scratch_shapes=[pltpu.VMEM((2, MOE_BLOCK * (D // 128), 128), F32), pltpu.VMEM((2, MOE_BLOCK * (D // 128), 128), F32),
                        pltpu.SemaphoreType.DMA((2,)), pltpu.SemaphoreType.DMA((2,))],
    )
    return pl.pallas_call(
        _expert_kernel,
        out_shape=jax.ShapeDtypeStruct((n_rows_out * (D // 128), 128), F32),
        grid_spec=grid_spec,
        compiler_params=_cparams(("arbitrary",)),
        name="moe_experts",
    )(block_expert, n_active, tok3.reshape(-1), dst3.reshape(-1), h3, wgu, wdn)


def _combine_kernel(yg_ref, r_ref, x_ref, gains_ref, tab_ref, o_ref, *, per):
    tile = pl.program_id(0) * per + 1 + pl.program_id(1)
    gates = r_ref[...]
    rows = lambda k: _tile_rows(yg_ref, lambda j: (pl.ds(k * (D // 128) + j, HALF, stride=2 * (D // 128)), slice(None)))
    y = gates[:, 2:3] * rows(0) + gates[:, 3:4] * rows(1)
    idx = _mod_index(tile, per)
    gate = tab_ref[5, pl.ds(idx, 1), :]
    o_ref[...] = x_ref[...] + gate * _rms(y, gains_ref[3:4, :])


def _combine_latents(yg, rout, xs, gains, tab, batch, s_len, per):
    nl = s_len // HALF
    tile = lambda b, j: b * per + 1 + j
    return pl.pallas_call(
        functools.partial(_combine_kernel, per=per),
        out_shape=jax.ShapeDtypeStruct((batch * s_len, D), F32),
        grid=(batch, nl),
        in_specs=[
            pl.BlockSpec((HALF * 2 * (D // 128), 128), lambda b, j: (tile(b, j), 0)),
            pl.BlockSpec((HALF, 128), lambda b, j: (tile(b, j), 0)),
            pl.BlockSpec((HALF, D), lambda b, j: (tile(b, j), 0)),
            _const_spec((4, D)), _const_spec((6, 8, D)),
        ],
        out_specs=pl.BlockSpec((HALF, D), lambda b, j: (b * nl + j, 0)),
        compiler_params=_cparams(("parallel", "parallel")),
        name="moe_combine",
    )(yg, rout, xs, gains, tab)


def _moe_plan(rout):
    n_tok = rout.shape[0]
    n_flat = 2 * n_tok
    n_blocks = -(-n_flat // MOE_BLOCK) + N_EXP
    assert n_blocks * MOE_BLOCK == n_flat + N_EXP * MOE_BLOCK and n_flat < 2 ** 19
    flat_e = rout[:, 0:2].astype(jnp.int32).reshape(-1)
    experts = jnp.arange(N_EXP, dtype=jnp.int32)
    counts = jnp.sum((flat_e[:, None] == experts[None, :]).astype(jnp.int32), axis=0)
    padded = (counts + MOE_BLOCK - 1) // MOE_BLOCK * MOE_BLOCK
    pend = jnp.cumsum(padded)
    blk = jnp.arange(n_blocks, dtype=jnp.int32)
    block_expert = jnp.minimum(jnp.searchsorted(pend, blk * MOE_BLOCK, side='right'), N_EXP - 1).astype(jnp.int32)
    n_active = (pend[-1] // MOE_BLOCK).astype(jnp.int32).reshape(1)
    flat = jnp.arange(n_flat, dtype=jnp.int32)
    pad_i = jnp.arange(MOE_BLOCK, dtype=jnp.int32)[None, :]
    pad_key = jnp.where(pad_i < (padded - counts)[:, None], experts[:, None] * 2 ** 20 + 2 ** 19 + pad_i,
                        jnp.iinfo(jnp.int32).max)
    keys = jnp.sort(jnp.concatenate([flat_e * 2 ** 20 + flat, pad_key.reshape(-1)]))
    rows = (keys & (2 ** 20 - 1)).reshape(n_blocks, MOE_BLOCK)
    valid = rows < 2 ** 19
    tok3 = (jnp.where(valid, rows // 2, 0) * (D // 128)).reshape(n_blocks, 1, MOE_BLOCK)
    dummy = n_flat + pad_i
    dst = jnp.where(valid, rows, dummy)
    dst3 = (jnp.concatenate([dummy, dst], axis=0) * (D // 128)).reshape(n_blocks + 1, 1, MOE_BLOCK)
    return block_expert, n_active, tok3, dst3, n_flat + MOE_BLOCK


def _rot_half_cols(w):
    half = A_ROPE // 2
    g = w.reshape(w.shape[0], -1, A_ROPE)
    return jnp.concatenate([-g[..., half:], g[..., :half]], axis=-1).reshape(w.shape)


def _pad_groups(w, width, to):
    k = w.shape[0]
    g = w.reshape(k, -1, width)
    return jnp.pad(g, ((0, 0), (0, 0), (0, to - width))).reshape(k, -1)


def _layer0_inproj_weight(w_in):
    splits = np.cumsum([A_QLORA, A_KVLORA, A_ROPE, M_HEADS * M_QK, M_HEADS * M_QK, M_HEADS * M_V, M_HEADS * M_V])
    cq, ckv, kr, mq, mk, mv, og, gt = jnp.split(w_in, [int(s) for s in splits], axis=1)
    pad64 = lambda a: jnp.pad(a, ((0, 0), (0, 64)))
    cols = [cq, ckv, pad64(kr), mv, _pad_groups(mq, M_QK, M_QKP), _pad_groups(mk, M_QK, M_QKP), og,
            jnp.pad(gt, ((0, 0), (0, 128 - gt.shape[1]))), pad64(_rot_half_cols(kr))]
    w = jnp.concatenate(cols, axis=1)
    assert w.shape[1] == P0_N
    return w.astype(BF16)


def _rope_tables(batch, s_len):
    t = jnp.arange(s_len)
    row = (t // GRID_W).astype(F32)
    col = (t % GRID_W).astype(F32)
    n_freq = A_ROPE // 4
    inv_freq = ROPE_THETA ** (-jnp.arange(n_freq, dtype=F32) / n_freq)
    ang = jnp.concatenate([row[:, None] * inv_freq, col[:, None] * inv_freq], axis=-1)
    ang = jnp.concatenate([ang, ang], axis=-1)

    def table(vals, fill):
        lat = jnp.pad(vals, ((0, 0), (0, 128 - A_ROPE)), constant_values=fill)
        one = jnp.concatenate([jnp.full((CTX, 128), fill, F32), lat], axis=0)
        return jnp.tile(one, (batch, 1))

    return table(jnp.cos(ang), 1.0), table(jnp.sin(ang), 0.0)


def kernel(x, c, ctx, c_ctx, w_mod, b_mod, norm_gains, w_in_ab, mla_q_gain, mla_w_uq, mla_kv_gain, mla_w_ukv,
           ml_conv_w, ml_conv_b, ml_gate_b, ml_head_gain, w_out_ab, ffn_w_gu, ffn_w_dn, w_in_c, hg_lb_logits,
           hg_head_gain, w_out_c, moe_router, moe_w_gu, moe_w_dn):
    batch, s_len, d = x.shape
    assert d == D and ctx.shape[1] == CTX and batch == 2
    assert s_len % 512 == 0 and (batch * (CTX + s_len)) % TM == 0
    t_len = CTX + s_len
    per = t_len // HALF

    xs = jnp.concatenate([ctx, x], axis=1).reshape(batch * t_len, D)
    cc = jnp.concatenate([c, c_ctx[None, :], jnp.zeros((8 - batch - 1, D), F32)], axis=0)
    tabs = _mod_table(cc, w_mod, b_mod)

    gains = norm_gains[0]
    tab = tabs[0]
    p0 = _inproj(xs, gains, tab, _layer0_inproj_weight(w_in_ab[0]), per, P0_N // 2)

    cos, sin = _rope_tables(batch, s_len)
    w_uq = mla_w_uq[0].reshape(A_QLORA, A_HEADS, A_NOPE + A_ROPE)
    wq_nope = w_uq[:, :, :A_NOPE].reshape(A_QLORA, A_HEADS * A_NOPE)
    wq_rope = w_uq[:, :, A_NOPE:].reshape(A_QLORA, A_HEADS * A_ROPE)
    w_ukv = mla_w_ukv[0].reshape(A_KVLORA, A_HEADS, A_NOPE + A_V)
    q, k, v = _mla_prep(
        p0, cos, sin, mla_q_gain[0][None, :], mla_kv_gain[0][None, :],
        wq_nope.astype(BF16),
        _pad_groups(wq_rope, A_ROPE, 128).astype(BF16),
        _pad_groups(_rot_half_cols(wq_rope), A_ROPE, 128).astype(BF16),
        w_ukv[:, :, :A_NOPE].reshape(A_KVLORA, -1).astype(BF16),
        w_ukv[:, :, A_NOPE:].reshape(A_KVLORA, -1).astype(BF16),
        batch, t_len)
    attn = _attention(q, k, v).reshape(batch * t_len, A_HEADS * A_V)

    conv_w = jnp.concatenate([_pad_groups(ml_conv_w[0][:, :M_HEADS * M_QK], M_QK, M_QKP),
                              _pad_groups(ml_conv_w[0][:, M_HEADS * M_QK:], M_QK, M_QKP)], axis=1)
    conv_b = jnp.concatenate([_pad_groups(ml_conv_b[0][None, :M_HEADS * M_QK], M_QK, M_QKP),
                              _pad_groups(ml_conv_b[0][None, M_HEADS * M_QK:], M_QK, M_QKP)], axis=1)
    post = jnp.concatenate([jnp.ones((1, 512), F32), jnp.full((1, 512), M_QK ** -0.5, F32)], axis=1)
    qk = _mlstm_conv(p0, conv_w, conv_b, post, t_len)
    gate_bias = jnp.pad(ml_gate_b[0][None, :], ((0, 0), (0, 128 - 4 * M_HEADS)))
    hf, hb = _mlstm_scan(qk, p0, gate_bias, batch, t_len)

    w_out = w_out_ab[0].astype(BF16)
    xs = _ab_out(xs, attn, hf, hb, p0, ml_head_gain[0][None, :], w_out[:A_HEADS * A_V], w_out[A_HEADS * A_V:],
                 gains, tab, per)
    xs = _ffn(xs, gains, tab, ffn_w_gu[0].astype(BF16), ffn_w_dn[0].astype(BF16), per)

    gains = norm_gains[1]
    tab = tabs[1]
    lbs = jnp.cumsum(jax.nn.softmax(hg_lb_logits.astype(F32), axis=0), axis=0)
    lb_row = (lbs - lbs[0])[1][None, :]
    p1 = _inproj(xs, gains, tab, w_in_c[0].astype(BF16), per, 1024)
    of, ob = _hgrn_scan(p1, lb_row, batch, t_len)
    xs = _c_out(xs, of, ob, p1, hg_head_gain[0][None, :], w_out_c[0].astype(BF16), gains, tab, per)

    h, rout = _router(xs, gains, tab, moe_router[0].T, per)
    block_expert, n_active, tok3, dst3, n_rows_out = _moe_plan(rout)
    yg = _experts(block_expert, n_active, tok3, dst3, h, moe_w_gu[0].astype(BF16), moe_w_dn[0].astype(BF16), n_rows_out)
    out = _combine_latents(yg, rout, xs, gains, tab, batch, s_len, per)
    return out.reshape(batch, s_len, D)
```
